```python
import jax
import jax.numpy as jnp
from jax import lax
import numpy as np

D_MODEL = 2048
BATCH = 4
SEQ = 2048
DEPTH = 1
DEC_BATCH = 128
DEC_SEQ = 4
PAST_LEN = 16384
PAGE_SIZE = 128

GLA_HEADS = 4
GLA_DK = D_MODEL // 4
GLA_DV = D_MODEL // 2
GLA_HK = GLA_DK // GLA_HEADS
GLA_HV = GLA_DV // GLA_HEADS
GLA_LOW_RANK = 16
GLA_TAU = 16.0
GLA_CHUNK = 64
CONV_CH = D_MODEL // 2
CONV_K = 3
N_EXPERTS = 32
TOP_K = 4
D_FF = D_MODEL
SWIGLU_LIMIT = 7.0
SWIGLU_ALPHA = 1.702
MOE_BLOCK = 128
N_MOD = 6
EPS = 1e-6
IN_SPLITS = (GLA_DK, GLA_DK, GLA_DV, GLA_LOW_RANK, GLA_DV, CONV_CH, CONV_CH, CONV_CH, D_MODEL, D_MODEL)
D_IN = sum(IN_SPLITS)
SPLIT_POINTS = tuple(int(s) for s in np.cumsum(IN_SPLITS)[:-1])

kernel_name = 'gla_shortconv_moe_adaln_decode_step'


def rmsnorm(x, w):
    xf = x.astype(jnp.float32)
    y = xf * lax.rsqrt(jnp.mean(xf * xf, axis=-1, keepdims=True) + EPS)
    return (y * w.astype(jnp.float32)).astype(x.dtype)


def adaln(c, w, b):
    m = (jax.nn.silu(c) @ w + b).reshape(c.shape[0], N_MOD, D_MODEL)
    return [m[:, i, None, :] for i in range(N_MOD)]


def gla_chunked(q, k, v, log_a, s0):
    bsz, seq_len = q.shape[:2]
    c = min(GLA_CHUNK, seq_len)
    pad = (-seq_len) % c
    f32 = jnp.float32

    def blocks(t):
        t = jnp.pad(t.astype(f32), ((0, 0), (0, pad), (0, 0), (0, 0)))
        return t.reshape(bsz, -1, c, GLA_HEADS, t.shape[-1]).transpose(0, 3, 1, 2, 4)

    q, k, v, log_a = blocks(q), blocks(k), blocks(v), blocks(log_a)
    b = jnp.cumsum(log_a, axis=3)
    b_mid = b[:, :, :, c // 2:c // 2 + 1]
    b_last = b[:, :, :, -1:]
    mask = jnp.tril(jnp.ones((c, c), dtype=bool))
    scores = jnp.einsum('bhnid,bhnjd->bhnij', q * jnp.exp(b - b_mid), k * jnp.exp(b_mid - b))
    o_intra = jnp.einsum('bhnij,bhnje->bhnie', jnp.where(mask, scores, 0.0), v)
    ds = jnp.einsum('bhncd,bhnce->bhnde', k * jnp.exp(b_last - b), v)
    decay = jnp.exp(b_last[:, :, :, 0])

    def step(s, inp):
        dec, d = inp
        return dec[..., None] * s + d, s

    s_final, s_prev = lax.scan(step, s0.astype(f32),
                               (jnp.moveaxis(decay, 2, 0), jnp.moveaxis(ds, 2, 0)))
    o_inter = jnp.einsum('bhncd,nbhde->bhnce', q * jnp.exp(b), s_prev)
    o = (o_intra + o_inter).transpose(0, 2, 3, 1, 4).reshape(bsz, -1, GLA_HEADS, v.shape[-1])
    return o[:, :seq_len], s_final


def token_mixer(xn, gla_s0, conv_buf, w_in, w_gk_up, b_gk, gla_norm_w, w_conv, w_out):
    bsz, seq_len, _ = xn.shape
    q, k, v, a_low, g, cb, cc, ch, ga, gb = jnp.split(xn @ w_in, SPLIT_POINTS, axis=-1)

    def heads(t, d):
        return t.reshape(bsz, seq_len, GLA_HEADS, d)

    log_a = jax.nn.log_sigmoid((a_low @ w_gk_up + b_gk).astype(jnp.float32)) / GLA_TAU
    o, s_new = gla_chunked(heads(q, GLA_HK) * (GLA_HK ** -0.5), heads(k, GLA_HK),
                           heads(v, GLA_HV), heads(log_a, GLA_HK), gla_s0)
    o = o * lax.rsqrt(jnp.mean(o * o, axis=-1, keepdims=True) + EPS) * gla_norm_w.astype(jnp.float32)
    o = o * jax.nn.silu(heads(g, GLA_HV).astype(jnp.float32))
    o_a = o.reshape(bsz, seq_len, GLA_DV).astype(xn.dtype)

    u = cc * ch
    u_ext = jnp.concatenate([conv_buf.astype(u.dtype), u], axis=1)
    conv = sum(w_conv[j] * u_ext[:, j:j + seq_len] for j in range(CONV_K))
    o_b = cb * conv

    y = (jax.nn.sigmoid(ga) * (o_a @ w_out[:GLA_DV])
         + jax.nn.sigmoid(gb) * (o_b @ w_out[GLA_DV:]))
    return y, s_new, u_ext[:, seq_len:]


def moe(x2, w_router, b_router, w_up, b_up, w_down, b_down):
    n_tok = x2.shape[0]
    logits = (x2 @ w_router + b_router).astype(jnp.float32)
    top_v, top_e = lax.top_k(logits, TOP_K)
    gate_w = jax.nn.softmax(top_v, axis=-1).reshape(-1)
    n_pairs = n_tok * TOP_K
    e_flat = top_e.reshape(-1)
    order = jnp.argsort(e_flat)
    e_sorted = e_flat[order]
    tok_sorted = order // TOP_K
    counts = jnp.bincount(e_flat, length=N_EXPERTS)
    padded = (counts + MOE_BLOCK - 1) // MOE_BLOCK * MOE_BLOCK
    start_raw = jnp.cumsum(counts) - counts
    end_pad = jnp.cumsum(padded)
    start_pad = end_pad - padded
    dest = start_pad[e_sorted] + (jnp.arange(n_pairs) - start_raw[e_sorted])
    n_blocks = -(-n_pairs // MOE_BLOCK) + N_EXPERTS
    rows = jnp.zeros((n_blocks * MOE_BLOCK, D_MODEL), x2.dtype).at[dest].set(x2[tok_sorted])
    block_e = jnp.minimum(jnp.searchsorted(end_pad, jnp.arange(n_blocks) * MOE_BLOCK, side='right'),
                          N_EXPERTS - 1)

    def expert_block(args):
        xb, e = args
        h = xb @ w_up[e] + b_up[e]
        h_gate, h_lin = h[:, :D_FF], h[:, D_FF:]
        h_gate = jnp.minimum(h_gate, SWIGLU_LIMIT)
        h_lin = jnp.clip(h_lin, -SWIGLU_LIMIT, SWIGLU_LIMIT)
        act = (h_lin + 1) * h_gate * jax.nn.sigmoid(SWIGLU_ALPHA * h_gate)
        return act @ w_down[e] + b_down[e]

    out_rows = lax.map(expert_block, (rows.reshape(n_blocks, MOE_BLOCK, D_MODEL), block_e))
    out_rows = out_rows.reshape(-1, D_MODEL)
    contrib = (out_rows[dest].astype(jnp.float32) * gate_w[order][:, None]).astype(x2.dtype)
    return jnp.zeros_like(x2).at[tok_sorted].add(contrib)


def trunk(x, c, gla_state, conv_state, params):
    (w_ada, b_ada, norm1_w, w_in, w_gk_up, b_gk, gla_norm_w, w_conv, w_out,
     norm2_w, w_router, b_router, w_up, b_up, w_down, b_down, final_norm_w) = params
    h = x
    new_gla, new_conv = [], []
    for l in range(DEPTH):
        sh1, sc1, g1, sh2, sc2, g2 = adaln(c, w_ada[l], b_ada[l])
        xn = rmsnorm(h, norm1_w[l]) * (1 + sc1) + sh1
        mix, s_new, buf_new = token_mixer(xn, gla_state[l], conv_state[l], w_in[l], w_gk_up[l],
                                          b_gk[l], gla_norm_w[l], w_conv[l], w_out[l])
        h = h + g1 * mix
        hn = rmsnorm(h, norm2_w[l]) * (1 + sc2) + sh2
        ff = moe(hn.reshape(-1, D_MODEL), w_router[l], b_router[l], w_up[l], b_up[l],
                 w_down[l], b_down[l]).reshape(h.shape)
        h = h + g2 * ff
        new_gla.append(s_new.astype(gla_state.dtype))
        new_conv.append(buf_new.astype(conv_state.dtype))
    return rmsnorm(h, final_norm_w), jnp.stack(new_gla), jnp.stack(new_conv)


def setup_inputs(seed: int = 0) -> dict:
    key = jax.random.key(seed)
    ks = jax.random.split(key, 24)
    nrm = jax.random.normal
    f32 = jnp.float32
    return {
        'x_prompt': nrm(ks[0], (BATCH, SEQ, D_MODEL), f32),
        'x_sample': nrm(ks[1], (DEC_BATCH, DEC_SEQ, D_MODEL), f32),
        'state_gla': nrm(ks[2], (DEPTH, DEC_BATCH, GLA_HEADS, GLA_HK, GLA_HV), f32),
        'state_conv': nrm(ks[3], (DEPTH, DEC_BATCH, CONV_K - 1, CONV_CH), f32),
        'c_prompt': nrm(ks[4], (BATCH, D_MODEL), f32),
        'c_sample': nrm(ks[5], (DEC_BATCH, D_MODEL), f32),
        'w_ada': nrm(ks[6], (DEPTH, D_MODEL, N_MOD * D_MODEL), f32) * (0.5 * D_MODEL ** -0.5),
        'b_ada': nrm(ks[7], (DEPTH, N_MOD * D_MODEL), f32) * 0.02,
        'norm1_w': 1.0 + 0.02 * nrm(ks[8], (DEPTH, D_MODEL), f32),
        'w_in': nrm(ks[9], (DEPTH, D_MODEL, D_IN), f32) * D_MODEL ** -0.5,
        'w_gk_up': nrm(ks[10], (DEPTH, GLA_LOW_RANK, GLA_DK), f32) * GLA_LOW_RANK ** -0.5,
        'b_gk': nrm(ks[11], (DEPTH, GLA_DK), f32) * 0.1,
        'gla_norm_w': 1.0 + 0.02 * nrm(ks[12], (DEPTH, GLA_HV), f32),
        'w_conv': nrm(ks[13], (DEPTH, CONV_K, CONV_CH), f32) * CONV_K ** -0.5,
        'w_out': nrm(ks[14], (DEPTH, GLA_DV + CONV_CH, D_MODEL), f32) * (GLA_DV + CONV_CH) ** -0.5,
        'norm2_w': 1.0 + 0.02 * nrm(ks[15], (DEPTH, D_MODEL), f32),
        'w_router': nrm(ks[16], (DEPTH, D_MODEL, N_EXPERTS), f32) * D_MODEL ** -0.5,
        'b_router': nrm(ks[17], (DEPTH, N_EXPERTS), f32) * 0.01,
        'w_up': nrm(ks[18], (DEPTH, N_EXPERTS, D_MODEL, 2 * D_FF), f32) * D_MODEL ** -0.5,
        'b_up': nrm(ks[19], (DEPTH, N_EXPERTS, 2 * D_FF), f32) * 0.02,
        'w_down': nrm(ks[20], (DEPTH, N_EXPERTS, D_FF, D_MODEL), f32) * D_FF ** -0.5,
        'b_down': nrm(ks[21], (DEPTH, N_EXPERTS, D_MODEL), f32) * 0.02,
        'final_norm_w': 1.0 + 0.02 * nrm(ks[22], (D_MODEL,), f32),
    }


def reference(x_prompt, x_sample, state_gla, state_conv, c_prompt, c_sample, w_ada, b_ada, norm1_w,
              w_in, w_gk_up, b_gk, gla_norm_w, w_conv, w_out, norm2_w, w_router, b_router, w_up, b_up,
              w_down, b_down, final_norm_w):
    params = (w_ada, b_ada, norm1_w, w_in, w_gk_up, b_gk, gla_norm_w, w_conv, w_out,
              norm2_w, w_router, b_router, w_up, b_up, w_down, b_down, final_norm_w)
    n_prompt = x_prompt.shape[0]
    gla0 = jnp.zeros((DEPTH, n_prompt, GLA_HEADS, GLA_HK, GLA_HV), state_gla.dtype)
    conv0 = jnp.zeros((DEPTH, n_prompt, CONV_K - 1, CONV_CH), state_conv.dtype)
    y_prompt, gla_prompt, conv_prompt = trunk(x_prompt, c_prompt, gla0, conv0, params)
    y_sample, gla_sample, conv_sample = trunk(x_sample, c_sample, state_gla, state_conv, params)
    return (y_prompt, y_sample, gla_prompt, conv_prompt, gla_sample, conv_sample)
```

```python
import functools

import jax
import jax.numpy as jnp
from jax import lax
from jax.experimental import pallas as pl
from jax.experimental.pallas import tpu as pltpu

f32 = jnp.float32
bf16 = jnp.bfloat16

D_MODEL = 2048
N_MOD = 6
GLA_HEADS = 4
GLA_DK = 512
GLA_DV = 1024
GLA_HK = 128
GLA_HV = 256
GLA_LOW_RANK = 16
GLA_TAU = 16.0
GLA_CHUNK = 64
CONV_CH = 1024
CONV_K = 3
N_EXPERTS = 32
TOP_K = 4
D_FF = 2048
SWIGLU_LIMIT = 7.0
SWIGLU_ALPHA = 1.702
EPS = 1e-6

LANES = 128
D_MAIN = 10240
VMEM_LIMIT = 56 * 1024 * 1024

OFF_Q, OFF_K, OFF_V, OFF_G = 0, 512, 1024, 2048
OFF_CB, OFF_CC, OFF_CH, OFF_GA, OFF_GB = 3072, 4096, 5120, 6144, 8192

HIGHEST = lax.Precision.HIGHEST


def _cparams(sem):
    return pltpu.CompilerParams(dimension_semantics=sem, vmem_limit_bytes=VMEM_LIMIT)


def _dot(a, b):
    return jnp.dot(a, b, preferred_element_type=f32)


def _dot_nt(a, b):
    return lax.dot_general(a, b, (((1,), (1,)), ((), ())), preferred_element_type=f32)


def _dot_exact(a, b):
    return jnp.dot(a, b, precision=HIGHEST, preferred_element_type=f32)


def _rms(x):
    return x * lax.rsqrt(jnp.mean(x * x, axis=-1, keepdims=True) + EPS)


ADA_TN = 1024


def _ada_kernel(c_ref, w_ref, b_ref, o_ref):
    c = c_ref[...]
    s = (c * jax.nn.sigmoid(c)).astype(bf16)
    o_ref[...] = _dot(s, w_ref[...].astype(bf16)) + b_ref[...]


def _ada(c, w, b):
    rows = c.shape[0]
    n = w.shape[1]
    return pl.pallas_call(
        _ada_kernel,
        grid=(n // ADA_TN,),
        in_specs=[pl.BlockSpec((rows, D_MODEL), lambda j: (0, 0)),
                  pl.BlockSpec((D_MODEL, ADA_TN), lambda j: (0, j)),
                  pl.BlockSpec((1, ADA_TN), lambda j: (0, j))],
        out_specs=pl.BlockSpec((rows, ADA_TN), lambda j: (0, j)),
        out_shape=jax.ShapeDtypeStruct((rows, n), f32),
        compiler_params=_cparams(("arbitrary",)),
        name="adaln",
    )(c, w, b.reshape(1, n))


IN_TN = 1024


def _in_kernel(x_ref, sc_ref, sh_ref, nw_ref, w_ref, wa_ref, wgk_ref, bgk_ref,
               proj_ref, la_ref, xn_ref):
    @pl.when(pl.program_id(1) == 0)
    def _():
        xn = _rms(x_ref[...]) * nw_ref[...] * (1.0 + sc_ref[...]) + sh_ref[...]
        xnb = xn.astype(bf16)
        xn_ref[...] = xnb
        a_low = _dot(xnb, wa_ref[...])
        z = _dot(a_low.astype(bf16), wgk_ref[...]) + bgk_ref[...]
        la_ref[...] = (jnp.minimum(z, 0.0) - jnp.log(1.0 + jnp.exp(-jnp.abs(z)))) * (1.0 / GLA_TAU)

    proj_ref[...] = _dot(xn_ref[...], w_ref[...]).astype(bf16)


def _in_proj(x2, sc, sh, nw, w_main, w_alow, w_gk, b_gk, tm, per_row_mod):
    t = x2.shape[0]
    if per_row_mod:
        mod_spec = pl.BlockSpec((tm, D_MODEL), lambda m, n: (m, 0))
    else:
        tiles_per_seq = (t // sc.shape[0]) // tm
        mod_spec = pl.BlockSpec((None, 1, D_MODEL), lambda m, n: (m // tiles_per_seq, 0, 0))
    return pl.pallas_call(
        _in_kernel,
        grid=(t // tm, D_MAIN // IN_TN),
        in_specs=[pl.BlockSpec((tm, D_MODEL), lambda m, n: (m, 0)),
                  mod_spec, mod_spec,
                  pl.BlockSpec((1, D_MODEL), lambda m, n: (0, 0)),
                  pl.BlockSpec((D_MODEL, IN_TN), lambda m, n: (0, n)),
                  pl.BlockSpec((D_MODEL, LANES), lambda m, n: (0, 0)),
                  pl.BlockSpec((LANES, GLA_DK), lambda m, n: (0, 0)),
                  pl.BlockSpec((1, GLA_DK), lambda m, n: (0, 0))],
        out_specs=[pl.BlockSpec((tm, IN_TN), lambda m, n: (m, n)),
                   pl.BlockSpec((tm, GLA_DK), lambda m, n: (m, 0))],
        out_shape=[jax.ShapeDtypeStruct((t, D_MAIN), bf16),
                   jax.ShapeDtypeStruct((t, GLA_DK), f32)],
        scratch_shapes=[pltpu.VMEM((tm, D_MODEL), bf16)],
        compiler_params=_cparams(("arbitrary", "arbitrary")),
        name="in_proj",
    )(x2, sc, sh, nw, w_main, w_alow, w_gk, b_gk)


GLA_R = 256


def _gla_norm_gate(o, nw, g):
    gf = g.astype(f32)
    return _rms(o) * nw * (gf * jax.nn.sigmoid(gf))


def _gla_scan_kernel(q_ref, k_ref, v_ref, g_ref, la_ref, nw_ref, s0_ref, o_ref, sout_ref, s_ref):
    step = pl.program_id(1)

    @pl.when(step == 0)
    def _():
        s_ref[...] = s0_ref[...]

    r, c = GLA_R, GLA_CHUNK
    n_chunks = r // c
    row = lax.broadcasted_iota(jnp.int32, (r, r), 0)
    col = lax.broadcasted_iota(jnp.int32, (r, r), 1)
    same_chunk = (row // c) == (col // c)
    causal = same_chunk & (col <= row)

    la = la_ref[...]
    b = _dot_exact(causal.astype(f32), la)
    b_mid = jnp.concatenate(
        [jnp.broadcast_to(b[i * c + c // 2:i * c + c // 2 + 1], (c, GLA_DK)) for i in range(n_chunks)], axis=0)
    b_last = jnp.concatenate(
        [jnp.broadcast_to(b[i * c + c - 1:i * c + c], (c, GLA_DK)) for i in range(n_chunks)], axis=0)
    q = q_ref[...].astype(f32) * (GLA_HK ** -0.5)
    k = k_ref[...].astype(f32)
    qs = (q * jnp.exp(b - b_mid)).astype(bf16)
    ks = (k * jnp.exp(b_mid - b)).astype(bf16)
    qd = (q * jnp.exp(b)).astype(bf16)
    kd_t = (k * jnp.exp(b_last - b)).T.astype(bf16)
    la_t = la.T
    lane = lax.broadcasted_iota(jnp.int32, (GLA_HK, r), 1)
    nw = nw_ref[...]

    for h in range(GLA_HEADS):
        hk = slice(h * GLA_HK, (h + 1) * GLA_HK)
        hv = slice(h * GLA_HV, (h + 1) * GLA_HV)
        v_h = v_ref[:, hv]
        scores = _dot_nt(qs[:, hk], ks[:, hk])
        o_intra = _dot(jnp.where(causal, scores, 0.0).astype(bf16), v_h)
        s = s_ref[h]
        for i in range(n_chunks):
            rows = slice(i * c, (i + 1) * c)
            in_chunk = (lane // c) == i
            o = o_intra[rows] + _dot(qd[rows, hk], s.astype(bf16))
            o_ref[rows, hv] = _gla_norm_gate(o, nw, g_ref[rows, hv]).astype(bf16)
            decay = jnp.exp(jnp.sum(jnp.where(in_chunk, la_t[hk], 0.0), axis=1, keepdims=True))
            s = decay * s + _dot(jnp.where(in_chunk, kd_t[hk], jnp.zeros_like(kd_t[hk])), v_h)
        s_ref[h] = s

    @pl.when(step == pl.num_programs(1) - 1)
    def _():
        sout_ref[...] = s_ref[...]


def _gla_scan(proj, la, nw, s0, n_seq, seq_len):
    steps = seq_len // GLA_R
    t = n_seq * seq_len

    def rows(b, s):
        return b * steps + s

    return pl.pallas_call(
        _gla_scan_kernel,
        grid=(n_seq, steps),
        in_specs=[pl.BlockSpec((GLA_R, GLA_DK), lambda b, s: (rows(b, s), OFF_Q // GLA_DK)),
                  pl.BlockSpec((GLA_R, GLA_DK), lambda b, s: (rows(b, s), OFF_K // GLA_DK)),
                  pl.BlockSpec((GLA_R, GLA_DV), lambda b, s: (rows(b, s), OFF_V // GLA_DV)),
                  pl.BlockSpec((GLA_R, GLA_DV), lambda b, s: (rows(b, s), OFF_G // GLA_DV)),
                  pl.BlockSpec((GLA_R, GLA_DK), lambda b, s: (rows(b, s), 0)),
                  pl.BlockSpec((1, GLA_HV), lambda b, s: (0, 0)),
                  pl.BlockSpec((None, GLA_HEADS, GLA_HK, GLA_HV), lambda b, s: (b, 0, 0, 0))],
        out_specs=[pl.BlockSpec((GLA_R, GLA_DV), lambda b, s: (rows(b, s), 0)),
                   pl.BlockSpec((None, GLA_HEADS, GLA_HK, GLA_HV), lambda b, s: (b, 0, 0, 0))],
        out_shape=[jax.ShapeDtypeStruct((t, GLA_DV), bf16),
                   jax.ShapeDtypeStruct((n_seq, GLA_HEADS, GLA_HK, GLA_HV), f32)],
        scratch_shapes=[pltpu.VMEM((GLA_HEADS, GLA_HK, GLA_HV), f32)],
        compiler_params=_cparams(("arbitrary", "arbitrary")),
        name="gla_scan",
    )(proj, proj, proj, proj, la, nw, s0)


GS_SEQ = 4
GS_ROWS = 128
GS_B = GS_ROWS // GS_SEQ


def _gla_step_kernel(q_ref, k_ref, v_ref, g_ref, la_ref, nw_ref, s0_ref, o_ref, sout_ref,
                     oacc_ref, qd_ref, kdt_ref, lat_ref):
    j = pl.program_id(1)
    r = GS_ROWS
    row = lax.broadcasted_iota(jnp.int32, (r, r), 0)
    col = lax.broadcasted_iota(jnp.int32, (r, r), 1)

    @pl.when(j == 0)
    def _():
        same_seq = (row // GS_SEQ) == (col // GS_SEQ)
        causal = same_seq & (col <= row)
        la = la_ref[...]
        b = _dot_exact(causal.astype(f32), la)
        b_mid = _dot_exact((same_seq & (col % GS_SEQ <= GS_SEQ // 2)).astype(f32), la)
        b_last = _dot_exact(same_seq.astype(f32), la)
        q = q_ref[...].astype(f32) * (GLA_HK ** -0.5)
        k = k_ref[...].astype(f32)
        qs = (q * jnp.exp(b - b_mid)).astype(bf16)
        ks = (k * jnp.exp(b_mid - b)).astype(bf16)
        qd_ref[...] = (q * jnp.exp(b)).astype(bf16)
        kdt_ref[...] = (k * jnp.exp(b_last - b)).T.astype(bf16)
        lat_ref[...] = la.T
        for h in range(GLA_HEADS):
            hk = slice(h * GLA_HK, (h + 1) * GLA_HK)
            hv = slice(h * GLA_HV, (h + 1) * GLA_HV)
            scores = _dot_nt(qs[:, hk], ks[:, hk])
            oacc_ref[:, hv] = _dot(jnp.where(causal, scores, 0.0).astype(bf16), v_ref[:, hv])

    in_seq_lane = (col // GS_SEQ) == j
    in_seq_row = (row[:, :GLA_HK] // GS_SEQ) == j
    for h in range(GLA_HEADS):
        hk = slice(h * GLA_HK, (h + 1) * GLA_HK)
        hv = slice(h * GLA_HV, (h + 1) * GLA_HV)
        s0 = s0_ref[h]
        kd_t = kdt_ref[hk, :]
        ds = _dot(jnp.where(in_seq_lane, kd_t, jnp.zeros_like(kd_t)), v_ref[:, hv])
        decay = jnp.exp(jnp.sum(jnp.where(in_seq_lane, lat_ref[hk, :], 0.0), axis=1, keepdims=True))
        sout_ref[h] = decay * s0 + ds
        qd = qd_ref[:, hk]
        oacc_ref[:, hv] += _dot(jnp.where(in_seq_row, qd, jnp.zeros_like(qd)), s0.astype(bf16))

    @pl.when(j == pl.num_programs(1) - 1)
    def _():
        nw = nw_ref[...]
        for h in range(GLA_HEADS):
            hv = slice(h * GLA_HV, (h + 1) * GLA_HV)
            o_ref[:, hv] = _gla_norm_gate(oacc_ref[:, hv], nw, g_ref[:, hv]).astype(bf16)


def _gla_step(proj, la, nw, s0):
    n_seq = s0.shape[0]
    t = n_seq * GS_SEQ
    groups = t // GS_ROWS
    return pl.pallas_call(
        _gla_step_kernel,
        grid=(groups, GS_B),
        in_specs=[pl.BlockSpec((GS_ROWS, GLA_DK), lambda g, j: (g, OFF_Q // GLA_DK)),
                  pl.BlockSpec((GS_ROWS, GLA_DK), lambda g, j: (g, OFF_K // GLA_DK)),
                  pl.BlockSpec((GS_ROWS, GLA_DV), lambda g, j: (g, OFF_V // GLA_DV)),
                  pl.BlockSpec((GS_ROWS, GLA_DV), lambda g, j: (g, OFF_G // GLA_DV)),
                  pl.BlockSpec((GS_ROWS, GLA_DK), lambda g, j: (g, 0)),
                  pl.BlockSpec((1, GLA_HV), lambda g, j: (0, 0)),
                  pl.BlockSpec((None, GLA_HEADS, GLA_HK, GLA_HV), lambda g, j: (g * GS_B + j, 0, 0, 0))],
        out_specs=[pl.BlockSpec((GS_ROWS, GLA_DV), lambda g, j: (g, 0)),
                   pl.BlockSpec((None, GLA_HEADS, GLA_HK, GLA_HV), lambda g, j: (g * GS_B + j, 0, 0, 0))],
        out_shape=[jax.ShapeDtypeStruct((t, GLA_DV), bf16),
                   jax.ShapeDtypeStruct((n_seq, GLA_HEADS, GLA_HK, GLA_HV), f32)],
        scratch_shapes=[pltpu.VMEM((GS_ROWS, GLA_DV), f32),
                        pltpu.VMEM((GS_ROWS, GLA_DK), bf16),
                        pltpu.VMEM((GLA_DK, GS_ROWS), bf16),
                        pltpu.VMEM((GLA_DK, GS_ROWS), f32)],
        compiler_params=_cparams(("arbitrary", "arbitrary")),
        name="gla_step",
    )(proj, proj, proj, proj, la, nw, s0)


POST_TM = 256
HALO = 16


def _post_kernel(long_seq, tiles_per_seq, *refs):
    if long_seq:
        (x_ref, oa_ref, cb_ref, cc_ref, ch_ref, ga_ref, gb_ref, hcc_ref, hch_ref, cbuf_ref,
         wc_ref, g1_ref, sc_ref, sh_ref, nw_ref, wo_ref, wrh_ref, wrl_ref, br_ref,
         h_ref, hn_ref, lg_ref, u_ref) = refs
    else:
        (x_ref, oa_ref, cb_ref, cc_ref, ch_ref, ga_ref, gb_ref, p0_ref, p1_ref,
         wc_ref, g1_ref, sc_ref, sh_ref, nw_ref, wo_ref, wrh_ref, wrl_ref, br_ref,
         h_ref, hn_ref, lg_ref, u_ref) = refs
    tm = x_ref.shape[0]
    u = cc_ref[...].astype(f32) * ch_ref[...].astype(f32)
    row = lax.broadcasted_iota(jnp.int32, (tm, CONV_CH), 0)
    if long_seq:
        pos = row
        first = (pl.program_id(0) % tiles_per_seq) == 0
        halo = hcc_ref[HALO - 2:HALO, :].astype(f32) * hch_ref[HALO - 2:HALO, :].astype(f32)
        cbuf = cbuf_ref[...]
        p0 = jnp.where(first, cbuf[0:1], halo[0:1])
        p1 = jnp.where(first, cbuf[1:2], halo[1:2])
        u_ref[...] = u[tm - 8:tm]
    else:
        pos = row % GS_SEQ
        p0 = p0_ref[...]
        p1 = p1_ref[...]
        u_ref[...] = u
    u1 = jnp.where(pos == 0, p1, pltpu.roll(u, 1, 0))
    u2 = jnp.where(pos == 0, p0, jnp.where(pos == 1, p1, pltpu.roll(u, 2, 0)))
    wc = wc_ref[...]
    conv = wc[0:1] * u2 + wc[1:2] * u1 + wc[2:3] * u
    ob = (cb_ref[...].astype(f32) * conv).astype(bf16)

    ya = _dot(oa_ref[...], wo_ref[0:GLA_DV, :])
    yb = _dot(ob, wo_ref[GLA_DV:GLA_DV + CONV_CH, :])
    y = jax.nn.sigmoid(ga_ref[...].astype(f32)) * ya + jax.nn.sigmoid(gb_ref[...].astype(f32)) * yb
    h = x_ref[...] + g1_ref[...] * y
    h_ref[...] = h
    hn = _rms(h) * nw_ref[...] * (1.0 + sc_ref[...]) + sh_ref[...]
    hn_hi = hn.astype(bf16)
    hn_ref[...] = hn_hi
    hn_lo = (hn - hn_hi.astype(f32)).astype(bf16)
    lg_ref[...] = (_dot(hn_hi, wrh_ref[...]) + _dot(hn_hi, wrl_ref[...]) + _dot(hn_lo, wrh_ref[...])
                   + br_ref[...])


def _post(x2, oa, proj, prev, wc, g1, sc2, sh2, nw, wo, wrh, wrl, br, long_seq, seq_len):
    t = x2.shape[0]
    tm = POST_TM
    n_tiles = t // tm
    tiles_per_seq = seq_len // tm if long_seq else 1

    def colblk(off, width):
        return pl.BlockSpec((tm, width), lambda m: (m, off // width))

    const = lambda shape: pl.BlockSpec(shape, lambda m: tuple(0 for _ in shape))
    if long_seq:
        mod_spec = pl.BlockSpec((None, 1, D_MODEL), lambda m: (m // tiles_per_seq, 0, 0))
        halo_rows = lambda m: jnp.maximum(m * (tm // HALO) - 1, 0)
        prev_specs = [pl.BlockSpec((HALO, CONV_CH), lambda m: (halo_rows(m), OFF_CC // CONV_CH)),
                      pl.BlockSpec((HALO, CONV_CH), lambda m: (halo_rows(m), OFF_CH // CONV_CH)),
                      pl.BlockSpec((None, CONV_K - 1, CONV_CH), lambda m: (m // tiles_per_seq, 0, 0))]
        prev_args = [proj, proj, prev]
        u_spec = pl.BlockSpec((None, 8, CONV_CH), lambda m: (m, 0, 0))
        u_shape = jax.ShapeDtypeStruct((n_tiles, 8, CONV_CH), f32)
    else:
        mod_spec = pl.BlockSpec((tm, D_MODEL), lambda m: (m, 0))
        prev_specs = [pl.BlockSpec((tm, CONV_CH), lambda m: (m, 0))] * 2
        prev_args = list(prev)
        u_spec = pl.BlockSpec((tm, CONV_CH), lambda m: (m, 0))
        u_shape = jax.ShapeDtypeStruct((t, CONV_CH), f32)
    return pl.pallas_call(
        functools.partial(_post_kernel, long_seq, tiles_per_seq),
        grid=(n_tiles,),
        in_specs=[pl.BlockSpec((tm, D_MODEL), lambda m: (m, 0)),
                  pl.BlockSpec((tm, GLA_DV), lambda m: (m, 0)),
                  colblk(OFF_CB, CONV_CH), colblk(OFF_CC, CONV_CH), colblk(OFF_CH, CONV_CH),
                  colblk(OFF_GA, D_MODEL), colblk(OFF_GB, D_MODEL)]
                 + prev_specs
                 + [const((CONV_K, CONV_CH)), mod_spec, mod_spec, mod_spec, const((1, D_MODEL)),
                    const((GLA_DV + CONV_CH, D_MODEL)), const((D_MODEL, LANES)), const((D_MODEL, LANES)),
                    const((1, LANES))],
        out_specs=[pl.BlockSpec((tm, D_MODEL), lambda m: (m, 0)),
                   pl.BlockSpec((tm, D_MODEL), lambda m: (m, 0)),
                   pl.BlockSpec((tm, LANES), lambda m: (m, 0)),
                   u_spec],
        out_shape=[jax.ShapeDtypeStruct((t, D_MODEL), f32),
                   jax.ShapeDtypeStruct((t, D_MODEL), bf16),
                   jax.ShapeDtypeStruct((t, LANES), f32),
                   u_shape],
        compiler_params=_cparams(("arbitrary",)),
        name="post_mix",
    )(x2, oa, proj, proj, proj, proj, proj, *prev_args, wc, g1, sc2, sh2, nw, wo, wrh, wrl, br)


MOE_RB = 256
MOE_TM = 1024
MOE_TF = 256
MOE_KB = MOE_TM // MOE_RB
MOE_NF = D_FF // MOE_TF


def _moe_kernel(ie_ref, ib_ref, in_ref, iv_ref, iz_ref,
                x_hbm, wg_ref, wl_ref, wd_ref, bg_ref, bl_ref, bd_ref, out_hbm,
                xbuf, acc, obuf, sem_in, sem_out):
    i = pl.program_id(0)
    f = pl.program_id(1)
    nb = in_ref[i]
    blk0 = ib_ref[i]
    nz = iz_ref[i]

    def x_copy(jb):
        return pltpu.make_async_copy(x_hbm.at[pl.ds((blk0 + jb) * MOE_RB, MOE_RB)],
                                     xbuf.at[pl.ds(jb * MOE_RB, MOE_RB)], sem_in)

    def out_copy(jb):
        return pltpu.make_async_copy(obuf.at[pl.ds(jb * MOE_RB, MOE_RB)],
                                     out_hbm.at[pl.ds((blk0 + jb) * MOE_RB, MOE_RB)], sem_out)

    @pl.when(f == 0)
    def _():
        def start(jb, carry):
            x_copy(jb).start()
            return carry

        def wait(jb, carry):
            x_copy(jb).wait()
            return carry

        lax.fori_loop(0, nb, start, 0)
        lax.fori_loop(0, nb, wait, 0)

        def fill(jb, carry):
            obuf[pl.ds(pl.multiple_of(jb * MOE_RB, MOE_RB), MOE_RB), :] = jnp.zeros((MOE_RB, D_MODEL), bf16)
            out_copy(jb).start()
            return carry

        def fill_wait(jb, carry):
            out_copy(jb).wait()
            return carry

        lax.fori_loop(0, nz, fill, 0)
        lax.fori_loop(0, nz, fill_wait, 0)

    wg = wg_ref[...].astype(bf16)
    wl = wl_ref[...].astype(bf16)
    wd = wd_ref[...].astype(bf16)
    bg = bg_ref[...]
    bl = bl_ref[...]

    def block(jb, carry):
        rows = pl.ds(pl.multiple_of(jb * MOE_RB, MOE_RB), MOE_RB)
        xb = xbuf[rows, :]
        hg = jnp.minimum(_dot(xb, wg) + bg, SWIGLU_LIMIT)
        hl = jnp.clip(_dot(xb, wl) + bl, -SWIGLU_LIMIT, SWIGLU_LIMIT)
        act = (hl + 1.0) * hg * jax.nn.sigmoid(SWIGLU_ALPHA * hg)
        part = _dot(act.astype(bf16), wd)

        @pl.when(f == 0)
        def _():
            acc[rows, :] = part + bd_ref[...]

        @pl.when(f > 0)
        def _():
            acc[rows, :] += part

        @pl.when(f == MOE_NF - 1)
        def _():
            obuf[rows, :] = acc[rows, :].astype(bf16)
            out_copy(jb).start()

        return carry

    lax.fori_loop(0, nb, block, 0)

    @pl.when(f == MOE_NF - 1)
    def _():
        def wait(jb, carry):
            out_copy(jb).wait()
            return carry

        lax.fori_loop(0, nb, wait, 0)


def _moe(xs, items, w_up, b_up, w_down, b_down):
    rows = xs.shape[0]
    n_items = items[0].shape[0]

    def f_eff(i, f, tabs):
        valid = tabs[3][i]
        return f * valid + (MOE_NF - 1) * (1 - valid)

    def expert(i, tabs):
        return tabs[0][i]

    grid_spec = pltpu.PrefetchScalarGridSpec(
        num_scalar_prefetch=len(items),
        grid=(n_items, MOE_NF),
        in_specs=[pl.BlockSpec(memory_space=pl.ANY),
                  pl.BlockSpec((None, D_MODEL, MOE_TF), lambda i, f, *t: (expert(i, t), 0, f_eff(i, f, t))),
                  pl.BlockSpec((None, D_MODEL, MOE_TF), lambda i, f, *t: (expert(i, t), 0, MOE_NF + f_eff(i, f, t))),
                  pl.BlockSpec((None, MOE_TF, D_MODEL), lambda i, f, *t: (expert(i, t), f_eff(i, f, t), 0)),
                  pl.BlockSpec((None, 1, MOE_TF), lambda i, f, *t: (expert(i, t), 0, f_eff(i, f, t))),
                  pl.BlockSpec((None, 1, MOE_TF), lambda i, f, *t: (expert(i, t), 0, MOE_NF + f_eff(i, f, t))),
                  pl.BlockSpec((None, 1, D_MODEL), lambda i, f, *t: (expert(i, t), 0, 0))],
        out_specs=pl.BlockSpec(memory_space=pl.ANY),
        scratch_shapes=[pltpu.VMEM((MOE_TM, D_MODEL), bf16),
                        pltpu.VMEM((MOE_TM, D_MODEL), f32),
                        pltpu.VMEM((MOE_TM, D_MODEL), bf16),
                        pltpu.SemaphoreType.DMA(()),
                        pltpu.SemaphoreType.DMA(())],
    )
    return pl.pallas_call(
        _moe_kernel,
        grid_spec=grid_spec,
        out_shape=jax.ShapeDtypeStruct((rows, D_MODEL), bf16),
        compiler_params=_cparams(("arbitrary", "arbitrary")),
        name="moe_experts",
    )(*items,
      xs, w_up, w_up, w_down,
      b_up.reshape(N_EXPERTS, 1, 2 * D_FF), b_up.reshape(N_EXPERTS, 1, 2 * D_FF),
      b_down.reshape(N_EXPERTS, 1, D_MODEL))


FIN_TM = 256


def _final_kernel(h_ref, og_ref, gate_ref, g2_ref, fw_ref, y_ref):
    gate = gate_ref[...]
    ff = og_ref[0].astype(f32) * gate[:, 0:1]
    for j in range(1, TOP_K):
        ff = ff + og_ref[j].astype(f32) * gate[:, j:j + 1]
    h = h_ref[...] + g2_ref[...] * ff
    y_ref[...] = _rms(h) * fw_ref[...]


def _final(h, og, gate, g2, fw, row0, per_row_mod, seq_len):
    t = h.shape[0]
    tm = FIN_TM
    blk0 = row0 // tm
    if per_row_mod:
        mod_spec = pl.BlockSpec((tm, D_MODEL), lambda m: (m, 0))
    else:
        tiles_per_seq = seq_len // tm
        mod_spec = pl.BlockSpec((None, 1, D_MODEL), lambda m: (m // tiles_per_seq, 0, 0))
    return pl.pallas_call(
        _final_kernel,
        grid=(t // tm,),
        in_specs=[pl.BlockSpec((tm, D_MODEL), lambda m: (m, 0)),
                  pl.BlockSpec((TOP_K, tm, D_MODEL), lambda m: (0, blk0 + m, 0)),
                  pl.BlockSpec((tm, TOP_K), lambda m: (blk0 + m, 0)),
                  mod_spec,
                  pl.BlockSpec((1, D_MODEL), lambda m: (0, 0))],
        out_specs=pl.BlockSpec((tm, D_MODEL), lambda m: (m, 0)),
        out_shape=jax.ShapeDtypeStruct((t, D_MODEL), f32),
        compiler_params=_cparams(("arbitrary",)),
        name="final_norm",
    )(h, og, gate, g2, fw)


def _routing(logits):
    t = logits.shape[0]
    n_pairs = t * TOP_K
    top_v, top_e = lax.top_k(logits, TOP_K)
    gate = jax.nn.softmax(top_v, axis=-1)
    e_flat = top_e.reshape(-1)
    onehot = (e_flat[:, None] == jnp.arange(N_EXPERTS, dtype=e_flat.dtype)[None, :]).astype(jnp.int32)
    csum = jnp.cumsum(onehot, axis=0)
    rank = jnp.sum(onehot * csum, axis=1) - 1
    counts = csum[-1]
    nblk = (counts + MOE_RB - 1) // MOE_RB
    blk_end = jnp.cumsum(nblk)
    blk_start = blk_end - nblk
    dest = blk_start[e_flat] * MOE_RB + rank
    n_rows = (n_pairs // MOE_RB + N_EXPERTS) * MOE_RB
    src_tok = jnp.zeros((n_rows,), jnp.int32).at[dest].set(jnp.arange(n_pairs, dtype=jnp.int32) // TOP_K)

    n_items = N_EXPERTS + n_rows // MOE_TM
    nit = (nblk + MOE_KB - 1) // MOE_KB
    it_end = jnp.cumsum(nit)
    it_start = it_end - nit
    idx = jnp.arange(n_items, dtype=jnp.int32)
    valid = idx < it_end[-1]
    last_valid = jnp.maximum(it_end[-1] - 1, 0)
    idx_c = jnp.minimum(idx, last_valid)
    e_of = jnp.minimum(jnp.searchsorted(it_end, idx_c, side='right'), N_EXPERTS - 1).astype(jnp.int32)
    k = idx_c - it_start[e_of]
    item_nb = jnp.where(valid, jnp.clip(nblk[e_of] - k * MOE_KB, 0, MOE_KB), 0).astype(jnp.int32)
    spare = idx - it_end[-1]
    tail_blk = blk_end[-1] + spare * MOE_KB
    item_nz = jnp.where(valid, 0, jnp.clip(n_rows // MOE_RB - tail_blk, 0, MOE_KB)).astype(jnp.int32)
    item_blk = jnp.where(valid, blk_start[e_of] + k * MOE_KB, tail_blk).astype(jnp.int32)
    items = (e_of, item_blk, item_nb, valid.astype(jnp.int32), item_nz)
    return gate, dest.reshape(t, TOP_K), src_tok, items


def kernel(x_prompt, x_sample, state_gla, state_conv, c_prompt, c_sample, w_ada, b_ada, norm1_w,
           w_in, w_gk_up, b_gk, gla_norm_w, w_conv, w_out, norm2_w, w_router, b_router, w_up, b_up,
           w_down, b_down, final_norm_w):
    n_p, seq_p, _ = x_prompt.shape
    n_s, seq_s, _ = x_sample.shape
    t_p, t_s = n_p * seq_p, n_s * seq_s
    assert seq_s == GS_SEQ and w_ada.shape[0] == 1

    c_all = jnp.concatenate([c_prompt, c_sample], axis=0)
    pad = (-c_all.shape[0]) % 16
    c_all = jnp.pad(c_all, ((0, pad), (0, 0)))
    mod = _ada(c_all, w_ada[0], b_ada[0])
    mod_p = [mod[:n_p, i * D_MODEL:(i + 1) * D_MODEL].reshape(n_p, 1, D_MODEL) for i in range(N_MOD)]
    mod_s = [jnp.repeat(mod[n_p:n_p + n_s, i * D_MODEL:(i + 1) * D_MODEL], seq_s, axis=0) for i in range(N_MOD)]

    w_in0 = w_in[0]
    a0 = OFF_G
    w_main = jnp.concatenate([w_in0[:, :a0], w_in0[:, a0 + GLA_LOW_RANK:]], axis=1).astype(bf16)
    w_alow = jnp.pad(w_in0[:, a0:a0 + GLA_LOW_RANK], ((0, 0), (0, LANES - GLA_LOW_RANK))).astype(bf16)
    w_gk = jnp.pad(w_gk_up[0], ((0, LANES - GLA_LOW_RANK), (0, 0))).astype(bf16)
    bgk = b_gk[0].reshape(1, GLA_DK)
    wo = w_out[0].astype(bf16)
    wr = jnp.pad(w_router[0], ((0, 0), (0, LANES - N_EXPERTS)))
    wr_hi = wr.astype(bf16)
    wr_lo = (wr - wr_hi.astype(f32)).astype(bf16)
    br = jnp.pad(b_router[0], (0, LANES - N_EXPERTS)).reshape(1, LANES)
    n1w = norm1_w[0].reshape(1, D_MODEL)
    n2w = norm2_w[0].reshape(1, D_MODEL)
    gnw = gla_norm_w[0].reshape(1, GLA_HV)
    fw = final_norm_w.reshape(1, D_MODEL)
    wc = w_conv[0]

    xp = x_prompt.reshape(t_p, D_MODEL)
    xs_ = x_sample.reshape(t_s, D_MODEL)

    proj_p, la_p = _in_proj(xp, mod_p[1], mod_p[0], n1w, w_main, w_alow, w_gk, bgk, 1024, False)
    proj_s, la_s = _in_proj(xs_, mod_s[1], mod_s[0], n1w, w_main, w_alow, w_gk, bgk, t_s, True)

    gla0 = jnp.zeros((n_p, GLA_HEADS, GLA_HK, GLA_HV), f32)
    conv0 = jnp.zeros((n_p, CONV_K - 1, CONV_CH), f32)
    oa_p, gla_p = _gla_scan(proj_p, la_p, gnw, gla0, n_p, seq_p)
    oa_s, gla_s = _gla_step(proj_s, la_s, gnw, state_gla[0])

    h_p, hn_p, lg_p, ut_p = _post(xp, oa_p, proj_p, conv0, wc, mod_p[2], mod_p[4], mod_p[3], n2w, wo,
                                  wr_hi, wr_lo, br, True, seq_p)
    prev_s = (jnp.repeat(state_conv[0][:, 0], seq_s, axis=0), jnp.repeat(state_conv[0][:, 1], seq_s, axis=0))
    h_s, hn_s, lg_s, u_s = _post(xs_, oa_s, proj_s, prev_s, wc, mod_s[2], mod_s[4], mod_s[3], n2w, wo,
                                 wr_hi, wr_lo, br, False, seq_s)
    conv_p = ut_p.reshape(n_p, seq_p // POST_TM, 8, CONV_CH)[:, -1, 8 - (CONV_K - 1):]
    conv_s = u_s.reshape(n_s, seq_s, CONV_CH)[:, seq_s - (CONV_K - 1):]

    hn = jnp.concatenate([hn_p, hn_s], axis=0)
    logits = jnp.concatenate([lg_p, lg_s], axis=0)[:, :N_EXPERTS]
    gate, dest, src_tok, items = _routing(logits)
    x_rows = hn[src_tok]
    out_rows = _moe(x_rows, items, w_up[0], b_up[0], w_down[0], b_down[0])
    og = out_rows[dest.T]

    y_p = _final(h_p, og, gate, mod_p[5], fw, 0, False, seq_p)
    y_s = _final(h_s, og, gate, mod_s[5], fw, t_p, True, seq_s)

    return (y_p.reshape(n_p, seq_p, D_MODEL), y_s.reshape(n_s, seq_s, D_MODEL),
            gla_p[None], conv_p[None], gla_s[None], conv_s[None])
```

```python
import functools

import jax
import jax.numpy as jnp
from jax import lax
from jax.experimental import pallas as pl
from jax.experimental.pallas import tpu as pltpu

f32 = jnp.float32
bf16 = jnp.bfloat16

D_MODEL = 2048
N_MOD = 6
GLA_HEADS = 4
GLA_DK = 512
GLA_DV = 1024
GLA_HK = 128
GLA_HV = 256
GLA_LOW_RANK = 16
GLA_TAU = 16.0
GLA_CHUNK = 64
CONV_CH = 1024
CONV_K = 3
N_EXPERTS = 32
TOP_K = 4
D_FF = 2048
SWIGLU_LIMIT = 7.0
SWIGLU_ALPHA = 1.702
EPS = 1e-6

LANES = 128
D_MAIN = 10240
VMEM_LIMIT = 56 * 1024 * 1024

OFF_Q, OFF_K, OFF_V, OFF_G = 0, 512, 1024, 2048
OFF_CB, OFF_CC, OFF_CH, OFF_GA, OFF_GB = 3072, 4096, 5120, 6144, 8192

HIGHEST = lax.Precision.HIGHEST


def _cparams(sem):
    return pltpu.CompilerParams(dimension_semantics=sem, vmem_limit_bytes=VMEM_LIMIT)


def _dot(a, b):
    return jnp.dot(a, b, preferred_element_type=f32)


def _dot_nt(a, b):
    return lax.dot_general(a, b, (((1,), (1,)), ((), ())), preferred_element_type=f32)


def _dot_exact(a, b):
    return jnp.dot(a, b, precision=HIGHEST, preferred_element_type=f32)


def _rms(x):
    return x * lax.rsqrt(jnp.mean(x * x, axis=-1, keepdims=True) + EPS)


ADA_TN = 1024


def _ada_kernel(c_ref, w_ref, b_ref, o_ref):
    c = c_ref[...]
    s = (c * jax.nn.sigmoid(c)).astype(bf16)
    o_ref[...] = _dot(s, w_ref[...].astype(bf16)) + b_ref[...]


def _ada(c, w, b):
    rows = c.shape[0]
    n = w.shape[1]
    return pl.pallas_call(
        _ada_kernel,
        grid=(n // ADA_TN,),
        in_specs=[pl.BlockSpec((rows, D_MODEL), lambda j: (0, 0)),
                  pl.BlockSpec((D_MODEL, ADA_TN), lambda j: (0, j)),
                  pl.BlockSpec((1, ADA_TN), lambda j: (0, j))],
        out_specs=pl.BlockSpec((rows, ADA_TN), lambda j: (0, j)),
        out_shape=jax.ShapeDtypeStruct((rows, n), f32),
        compiler_params=_cparams(("arbitrary",)),
        name="adaln",
    )(c, w, b.reshape(1, n))


IN_TN = 1024


def _in_kernel(x_ref, sc_ref, sh_ref, nw_ref, w_ref, wa_ref, wgk_ref, bgk_ref,
               proj_ref, la_ref, xn_ref):
    @pl.when(pl.program_id(1) == 0)
    def _():
        xn = _rms(x_ref[...]) * nw_ref[...] * (1.0 + sc_ref[...]) + sh_ref[...]
        xnb = xn.astype(bf16)
        xn_ref[...] = xnb
        a_low = _dot(xnb, wa_ref[...])
        z = _dot(a_low.astype(bf16), wgk_ref[...]) + bgk_ref[...]
        la_ref[...] = (jnp.minimum(z, 0.0) - jnp.log(1.0 + jnp.exp(-jnp.abs(z)))) * (1.0 / GLA_TAU)

    proj_ref[...] = _dot(xn_ref[...], w_ref[...]).astype(bf16)


def _in_proj(x2, sc, sh, nw, w_main, w_alow, w_gk, b_gk, tm, per_row_mod):
    t = x2.shape[0]
    if per_row_mod:
        mod_spec = pl.BlockSpec((tm, D_MODEL), lambda m, n: (m, 0))
    else:
        tiles_per_seq = (t // sc.shape[0]) // tm
        mod_spec = pl.BlockSpec((None, 1, D_MODEL), lambda m, n: (m // tiles_per_seq, 0, 0))
    return pl.pallas_call(
        _in_kernel,
        grid=(t // tm, D_MAIN // IN_TN),
        in_specs=[pl.BlockSpec((tm, D_MODEL), lambda m, n: (m, 0)),
                  mod_spec, mod_spec,
                  pl.BlockSpec((1, D_MODEL), lambda m, n: (0, 0)),
                  pl.BlockSpec((D_MODEL, IN_TN), lambda m, n: (0, n)),
                  pl.BlockSpec((D_MODEL, LANES), lambda m, n: (0, 0)),
                  pl.BlockSpec((LANES, GLA_DK), lambda m, n: (0, 0)),
                  pl.BlockSpec((1, GLA_DK), lambda m, n: (0, 0))],
        out_specs=[pl.BlockSpec((tm, IN_TN), lambda m, n: (m, n)),
                   pl.BlockSpec((tm, GLA_DK), lambda m, n: (m, 0))],
        out_shape=[jax.ShapeDtypeStruct((t, D_MAIN), bf16),
                   jax.ShapeDtypeStruct((t, GLA_DK), f32)],
        scratch_shapes=[pltpu.VMEM((tm, D_MODEL), bf16)],
        compiler_params=_cparams(("arbitrary", "arbitrary")),
        name="in_proj",
    )(x2, sc, sh, nw, w_main, w_alow, w_gk, b_gk)


GLA_R = 256


def _gla_norm_gate(o, nw, g):
    gf = g.astype(f32)
    return _rms(o) * nw * (gf * jax.nn.sigmoid(gf))


def _gla_scan_kernel(q_ref, k_ref, v_ref, g_ref, la_ref, nw_ref, s0_ref, o_ref, sout_ref, s_ref):
    step = pl.program_id(1)

    @pl.when(step == 0)
    def _():
        s_ref[...] = s0_ref[...]

    r, c = GLA_R, GLA_CHUNK
    n_chunks = r // c
    row = lax.broadcasted_iota(jnp.int32, (r, r), 0)
    col = lax.broadcasted_iota(jnp.int32, (r, r), 1)
    same_chunk = (row // c) == (col // c)
    causal = same_chunk & (col <= row)

    la = la_ref[...]
    b = _dot_exact(causal.astype(f32), la)
    b_mid = jnp.concatenate(
        [jnp.broadcast_to(b[i * c + c // 2:i * c + c // 2 + 1], (c, GLA_DK)) for i in range(n_chunks)], axis=0)
    b_last = jnp.concatenate(
        [jnp.broadcast_to(b[i * c + c - 1:i * c + c], (c, GLA_DK)) for i in range(n_chunks)], axis=0)
    q = q_ref[...].astype(f32) * (GLA_HK ** -0.5)
    k = k_ref[...].astype(f32)
    qs = (q * jnp.exp(b - b_mid)).astype(bf16)
    ks = (k * jnp.exp(b_mid - b)).astype(bf16)
    qd = (q * jnp.exp(b)).astype(bf16)
    kd_t = (k * jnp.exp(b_last - b)).T.astype(bf16)
    la_t = la.T
    lane = lax.broadcasted_iota(jnp.int32, (GLA_HK, r), 1)
    nw = nw_ref[...]

    for h in range(GLA_HEADS):
        hk = slice(h * GLA_HK, (h + 1) * GLA_HK)
        hv = slice(h * GLA_HV, (h + 1) * GLA_HV)
        v_h = v_ref[:, hv]
        scores = _dot_nt(qs[:, hk], ks[:, hk])
        o_intra = _dot(jnp.where(causal, scores, 0.0).astype(bf16), v_h)
        s = s_ref[h]
        for i in range(n_chunks):
            rows = slice(i * c, (i + 1) * c)
            in_chunk = (lane // c) == i
            o = o_intra[rows] + _dot(qd[rows, hk], s.astype(bf16))
            o_ref[rows, hv] = _gla_norm_gate(o, nw, g_ref[rows, hv]).astype(bf16)
            decay = jnp.exp(jnp.sum(jnp.where(in_chunk, la_t[hk], 0.0), axis=1, keepdims=True))
            s = decay * s + _dot(jnp.where(in_chunk, kd_t[hk], jnp.zeros_like(kd_t[hk])), v_h)
        s_ref[h] = s

    @pl.when(step == pl.num_programs(1) - 1)
    def _():
        sout_ref[...] = s_ref[...]


def _gla_scan(proj, la, nw, s0, n_seq, seq_len):
    steps = seq_len // GLA_R
    t = n_seq * seq_len

    def rows(b, s):
        return b * steps + s

    return pl.pallas_call(
        _gla_scan_kernel,
        grid=(n_seq, steps),
        in_specs=[pl.BlockSpec((GLA_R, GLA_DK), lambda b, s: (rows(b, s), OFF_Q // GLA_DK)),
                  pl.BlockSpec((GLA_R, GLA_DK), lambda b, s: (rows(b, s), OFF_K // GLA_DK)),
                  pl.BlockSpec((GLA_R, GLA_DV), lambda b, s: (rows(b, s), OFF_V // GLA_DV)),
                  pl.BlockSpec((GLA_R, GLA_DV), lambda b, s: (rows(b, s), OFF_G // GLA_DV)),
                  pl.BlockSpec((GLA_R, GLA_DK), lambda b, s: (rows(b, s), 0)),
                  pl.BlockSpec((1, GLA_HV), lambda b, s: (0, 0)),
                  pl.BlockSpec((None, GLA_HEADS, GLA_HK, GLA_HV), lambda b, s: (b, 0, 0, 0))],
        out_specs=[pl.BlockSpec((GLA_R, GLA_DV), lambda b, s: (rows(b, s), 0)),
                   pl.BlockSpec((None, GLA_HEADS, GLA_HK, GLA_HV), lambda b, s: (b, 0, 0, 0))],
        out_shape=[jax.ShapeDtypeStruct((t, GLA_DV), bf16),
                   jax.ShapeDtypeStruct((n_seq, GLA_HEADS, GLA_HK, GLA_HV), f32)],
        scratch_shapes=[pltpu.VMEM((GLA_HEADS, GLA_HK, GLA_HV), f32)],
        compiler_params=_cparams(("arbitrary", "arbitrary")),
        name="gla_scan",
    )(proj, proj, proj, proj, la, nw, s0)


GS_SEQ = 4
GS_ROWS = 128
GS_B = GS_ROWS // GS_SEQ


def _gla_step_kernel(q_ref, k_ref, v_ref, g_ref, la_ref, nw_ref, s0_ref, o_ref, sout_ref,
                     oacc_ref, qd_ref, kdt_ref, lat_ref):
    j = pl.program_id(1)
    r = GS_ROWS
    row = lax.broadcasted_iota(jnp.int32, (r, r), 0)
    col = lax.broadcasted_iota(jnp.int32, (r, r), 1)

    @pl.when(j == 0)
    def _():
        same_seq = (row // GS_SEQ) == (col // GS_SEQ)
        causal = same_seq & (col <= row)
        la = la_ref[...]
        b = _dot_exact(causal.astype(f32), la)
        b_mid = _dot_exact((same_seq & (col % GS_SEQ <= GS_SEQ // 2)).astype(f32), la)
        b_last = _dot_exact(same_seq.astype(f32), la)
        q = q_ref[...].astype(f32) * (GLA_HK ** -0.5)
        k = k_ref[...].astype(f32)
        qs = (q * jnp.exp(b - b_mid)).astype(bf16)
        ks = (k * jnp.exp(b_mid - b)).astype(bf16)
        qd_ref[...] = (q * jnp.exp(b)).astype(bf16)
        kdt_ref[...] = (k * jnp.exp(b_last - b)).T.astype(bf16)
        lat_ref[...] = la.T
        for h in range(GLA_HEADS):
            hk = slice(h * GLA_HK, (h + 1) * GLA_HK)
            hv = slice(h * GLA_HV, (h + 1) * GLA_HV)
            scores = _dot_nt(qs[:, hk], ks[:, hk])
            oacc_ref[:, hv] = _dot(jnp.where(causal, scores, 0.0).astype(bf16), v_ref[:, hv])

    in_seq_lane = (col // GS_SEQ) == j
    in_seq_row = (row[:, :GLA_HK] // GS_SEQ) == j
    for h in range(GLA_HEADS):
        hk = slice(h * GLA_HK, (h + 1) * GLA_HK)
        hv = slice(h * GLA_HV, (h + 1) * GLA_HV)
        s0 = s0_ref[h]
        kd_t = kdt_ref[hk, :]
        ds = _dot(jnp.where(in_seq_lane, kd_t, jnp.zeros_like(kd_t)), v_ref[:, hv])
        decay = jnp.exp(jnp.sum(jnp.where(in_seq_lane, lat_ref[hk, :], 0.0), axis=1, keepdims=True))
        sout_ref[h] = decay * s0 + ds
        qd = qd_ref[:, hk]
        oacc_ref[:, hv] += _dot(jnp.where(in_seq_row, qd, jnp.zeros_like(qd)), s0.astype(bf16))

    @pl.when(j == pl.num_programs(1) - 1)
    def _():
        nw = nw_ref[...]
        for h in range(GLA_HEADS):
            hv = slice(h * GLA_HV, (h + 1) * GLA_HV)
            o_ref[:, hv] = _gla_norm_gate(oacc_ref[:, hv], nw, g_ref[:, hv]).astype(bf16)


def _gla_step(proj, la, nw, s0):
    n_seq = s0.shape[0]
    t = n_seq * GS_SEQ
    groups = t // GS_ROWS
    return pl.pallas_call(
        _gla_step_kernel,
        grid=(groups, GS_B),
        in_specs=[pl.BlockSpec((GS_ROWS, GLA_DK), lambda g, j: (g, OFF_Q // GLA_DK)),
                  pl.BlockSpec((GS_ROWS, GLA_DK), lambda g, j: (g, OFF_K // GLA_DK)),
                  pl.BlockSpec((GS_ROWS, GLA_DV), lambda g, j: (g, OFF_V // GLA_DV)),
                  pl.BlockSpec((GS_ROWS, GLA_DV), lambda g, j: (g, OFF_G // GLA_DV)),
                  pl.BlockSpec((GS_ROWS, GLA_DK), lambda g, j: (g, 0)),
                  pl.BlockSpec((1, GLA_HV), lambda g, j: (0, 0)),
                  pl.BlockSpec((None, GLA_HEADS, GLA_HK, GLA_HV), lambda g, j: (g * GS_B + j, 0, 0, 0))],
        out_specs=[pl.BlockSpec((GS_ROWS, GLA_DV), lambda g, j: (g, 0)),
                   pl.BlockSpec((None, GLA_HEADS, GLA_HK, GLA_HV), lambda g, j: (g * GS_B + j, 0, 0, 0))],
        out_shape=[jax.ShapeDtypeStruct((t, GLA_DV), bf16),
                   jax.ShapeDtypeStruct((n_seq, GLA_HEADS, GLA_HK, GLA_HV), f32)],
        scratch_shapes=[pltpu.VMEM((GS_ROWS, GLA_DV), f32),
                        pltpu.VMEM((GS_ROWS, GLA_DK), bf16),
                        pltpu.VMEM((GLA_DK, GS_ROWS), bf16),
                        pltpu.VMEM((GLA_DK, GS_ROWS), f32)],
        compiler_params=_cparams(("arbitrary", "arbitrary")),
        name="gla_step",
    )(proj, proj, proj, proj, la, nw, s0)


POST_TM = 256
HALO = 16


def _post_kernel(long_seq, tiles_per_seq, *refs):
    if long_seq:
        (x_ref, oa_ref, cb_ref, cc_ref, ch_ref, ga_ref, gb_ref, hcc_ref, hch_ref, cbuf_ref,
         wc_ref, g1_ref, sc_ref, sh_ref, nw_ref, wo_ref, wrh_ref, wrl_ref, br_ref,
         h_ref, hn_ref, lg_ref, u_ref) = refs
    else:
        (x_ref, oa_ref, cb_ref, cc_ref, ch_ref, ga_ref, gb_ref, p0_ref, p1_ref,
         wc_ref, g1_ref, sc_ref, sh_ref, nw_ref, wo_ref, wrh_ref, wrl_ref, br_ref,
         h_ref, hn_ref, lg_ref, u_ref) = refs
    tm = x_ref.shape[0]
    u = cc_ref[...].astype(f32) * ch_ref[...].astype(f32)
    row = lax.broadcasted_iota(jnp.int32, (tm, CONV_CH), 0)
    if long_seq:
        pos = row
        first = (pl.program_id(0) % tiles_per_seq) == 0
        halo = hcc_ref[HALO - 2:HALO, :].astype(f32) * hch_ref[HALO - 2:HALO, :].astype(f32)
        cbuf = cbuf_ref[...]
        p0 = jnp.where(first, cbuf[0:1], halo[0:1])
        p1 = jnp.where(first, cbuf[1:2], halo[1:2])
        u_ref[...] = u[tm - 8:tm]
    else:
        pos = row % GS_SEQ
        p0 = p0_ref[...]
        p1 = p1_ref[...]
        u_ref[...] = u
    u1 = jnp.where(pos == 0, p1, pltpu.roll(u, 1, 0))
    u2 = jnp.where(pos == 0, p0, jnp.where(pos == 1, p1, pltpu.roll(u, 2, 0)))
    wc = wc_ref[...]
    conv = wc[0:1] * u2 + wc[1:2] * u1 + wc[2:3] * u
    ob = (cb_ref[...].astype(f32) * conv).astype(bf16)

    ya = _dot(oa_ref[...], wo_ref[0:GLA_DV, :])
    yb = _dot(ob, wo_ref[GLA_DV:GLA_DV + CONV_CH, :])
    y = jax.nn.sigmoid(ga_ref[...].astype(f32)) * ya + jax.nn.sigmoid(gb_ref[...].astype(f32)) * yb
    h = x_ref[...] + g1_ref[...] * y
    h_ref[...] = h
    hn = _rms(h) * nw_ref[...] * (1.0 + sc_ref[...]) + sh_ref[...]
    hn_hi = hn.astype(bf16)
    hn_ref[...] = hn_hi
    hn_lo = (hn - hn_hi.astype(f32)).astype(bf16)
    lg_ref[...] = (_dot(hn_hi, wrh_ref[...]) + _dot(hn_hi, wrl_ref[...]) + _dot(hn_lo, wrh_ref[...])
                   + br_ref[...])


def _post(x2, oa, proj, prev, wc, g1, sc2, sh2, nw, wo, wrh, wrl, br, long_seq, seq_len):
    t = x2.shape[0]
    tm = POST_TM
    n_tiles = t // tm
    tiles_per_seq = seq_len // tm if long_seq else 1

    def colblk(off, width):
        return pl.BlockSpec((tm, width), lambda m: (m, off // width))

    const = lambda shape: pl.BlockSpec(shape, lambda m: tuple(0 for _ in shape))
    if long_seq:
        mod_spec = pl.BlockSpec((None, 1, D_MODEL), lambda m: (m // tiles_per_seq, 0, 0))
        halo_rows = lambda m: jnp.maximum(m * (tm // HALO) - 1, 0)
        prev_specs = [pl.BlockSpec((HALO, CONV_CH), lambda m: (halo_rows(m), OFF_CC // CONV_CH)),
                      pl.BlockSpec((HALO, CONV_CH), lambda m: (halo_rows(m), OFF_CH // CONV_CH)),
                      pl.BlockSpec((None, CONV_K - 1, CONV_CH), lambda m: (m // tiles_per_seq, 0, 0))]
        prev_args = [proj, proj, prev]
        u_spec = pl.BlockSpec((None, 8, CONV_CH), lambda m: (m, 0, 0))
        u_shape = jax.ShapeDtypeStruct((n_tiles, 8, CONV_CH), f32)
    else:
        mod_spec = pl.BlockSpec((tm, D_MODEL), lambda m: (m, 0))
        prev_specs = [pl.BlockSpec((tm, CONV_CH), lambda m: (m, 0))] * 2
        prev_args = list(prev)
        u_spec = pl.BlockSpec((tm, CONV_CH), lambda m: (m, 0))
        u_shape = jax.ShapeDtypeStruct((t, CONV_CH), f32)
    return pl.pallas_call(
        functools.partial(_post_kernel, long_seq, tiles_per_seq),
        grid=(n_tiles,),
        in_specs=[pl.BlockSpec((tm, D_MODEL), lambda m: (m, 0)),
                  pl.BlockSpec((tm, GLA_DV), lambda m: (m, 0)),
                  colblk(OFF_CB, CONV_CH), colblk(OFF_CC, CONV_CH), colblk(OFF_CH, CONV_CH),
                  colblk(OFF_GA, D_MODEL), colblk(OFF_GB, D_MODEL)]
                 + prev_specs
                 + [const((CONV_K, CONV_CH)), mod_spec, mod_spec, mod_spec, const((1, D_MODEL)),
                    const((GLA_DV + CONV_CH, D_MODEL)), const((D_MODEL, LANES)), const((D_MODEL, LANES)),
                    const((1, LANES))],
        out_specs=[pl.BlockSpec((tm, D_MODEL), lambda m: (m, 0)),
                   pl.BlockSpec((tm, D_MODEL), lambda m: (m, 0)),
                   pl.BlockSpec((tm, LANES), lambda m: (m, 0)),
                   u_spec],
        out_shape=[jax.ShapeDtypeStruct((t, D_MODEL), f32),
                   jax.ShapeDtypeStruct((t, D_MODEL), bf16),
                   jax.ShapeDtypeStruct((t, LANES), f32),
                   u_shape],
        compiler_params=_cparams(("arbitrary",)),
        name="post_mix",
    )(x2, oa, proj, proj, proj, proj, proj, *prev_args, wc, g1, sc2, sh2, nw, wo, wrh, wrl, br)


MOE_RB = 256
MOE_TM = 1024
MOE_TF = 256
MOE_KB = MOE_TM // MOE_RB
MOE_NF = D_FF // MOE_TF


def _moe_kernel(ie_ref, ib_ref, in_ref, iv_ref, iz_ref,
                x_hbm, wg_ref, wl_ref, wd_ref, bg_ref, bl_ref, bd_ref, out_hbm,
                xbuf, acc, obuf, sem_in, sem_out):
    i = pl.program_id(0)
    f = pl.program_id(1)
    nb = in_ref[i]
    blk0 = ib_ref[i]
    nz = iz_ref[i]

    def x_copy(jb):
        return pltpu.make_async_copy(x_hbm.at[pl.ds((blk0 + jb) * MOE_RB, MOE_RB)],
                                     xbuf.at[pl.ds(jb * MOE_RB, MOE_RB)], sem_in)

    def out_copy(jb):
        return pltpu.make_async_copy(obuf.at[pl.ds(jb * MOE_RB, MOE_RB)],
                                     out_hbm.at[pl.ds((blk0 + jb) * MOE_RB, MOE_RB)], sem_out)

    @pl.when(f == 0)
    def _():
        def start(jb, carry):
            x_copy(jb).start()
            return carry

        def wait(jb, carry):
            x_copy(jb).wait()
            return carry

        lax.fori_loop(0, nb, start, 0)
        lax.fori_loop(0, nb, wait, 0)

        def fill(jb, carry):
            obuf[pl.ds(pl.multiple_of(jb * MOE_RB, MOE_RB), MOE_RB), :] = jnp.zeros((MOE_RB, D_MODEL), bf16)
            out_copy(jb).start()
            return carry

        def fill_wait(jb, carry):
            out_copy(jb).wait()
            return carry

        lax.fori_loop(0, nz, fill, 0)
        lax.fori_loop(0, nz, fill_wait, 0)

    wg = wg_ref[...].astype(bf16)
    wl = wl_ref[...].astype(bf16)
    wd = wd_ref[...].astype(bf16)
    bg = bg_ref[...]
    bl = bl_ref[...]

    def block(jb, carry):
        rows = pl.ds(pl.multiple_of(jb * MOE_RB, MOE_RB), MOE_RB)
        xb = xbuf[rows, :]
        hg = jnp.minimum(_dot(xb, wg) + bg, SWIGLU_LIMIT)
        hl = jnp.clip(_dot(xb, wl) + bl, -SWIGLU_LIMIT, SWIGLU_LIMIT)
        act = (hl + 1.0) * hg * jax.nn.sigmoid(SWIGLU_ALPHA * hg)
        part = _dot(act.astype(bf16), wd)

        @pl.when(f == 0)
        def _():
            acc[rows, :] = part + bd_ref[...]

        @pl.when(f > 0)
        def _():
            acc[rows, :] += part

        @pl.when(f == MOE_NF - 1)
        def _():
            obuf[rows, :] = acc[rows, :].astype(bf16)
            out_copy(jb).start()

        return carry

    lax.fori_loop(0, nb, block, 0)

    @pl.when(f == MOE_NF - 1)
    def _():
        def wait(jb, carry):
            out_copy(jb).wait()
            return carry

        lax.fori_loop(0, nb, wait, 0)


def _moe(xs, items, w_up, b_up, w_down, b_down):
    rows = xs.shape[0]
    n_items = items[0].shape[0]

    def f_eff(i, f, tabs):
        valid = tabs[3][i]
        return f * valid + (MOE_NF - 1) * (1 - valid)

    def expert(i, tabs):
        return tabs[0][i]

    grid_spec = pltpu.PrefetchScalarGridSpec(
        num_scalar_prefetch=len(items),
        grid=(n_items, MOE_NF),
        in_specs=[pl.BlockSpec(memory_space=pl.ANY),
                  pl.BlockSpec((None, D_MODEL, MOE_TF), lambda i, f, *t: (expert(i, t), 0, f_eff(i, f, t))),
                  pl.BlockSpec((None, D_MODEL, MOE_TF), lambda i, f, *t: (expert(i, t), 0, MOE_NF + f_eff(i, f, t))),
                  pl.BlockSpec((None, MOE_TF, D_MODEL), lambda i, f, *t: (expert(i, t), f_eff(i, f, t), 0)),
                  pl.BlockSpec((None, 1, MOE_TF), lambda i, f, *t: (expert(i, t), 0, f_eff(i, f, t))),
                  pl.BlockSpec((None, 1, MOE_TF), lambda i, f, *t: (expert(i, t), 0, MOE_NF + f_eff(i, f, t))),
                  pl.BlockSpec((None, 1, D_MODEL), lambda i, f, *t: (expert(i, t), 0, 0))],
        out_specs=pl.BlockSpec(memory_space=pl.ANY),
        scratch_shapes=[pltpu.VMEM((MOE_TM, D_MODEL), bf16),
                        pltpu.VMEM((MOE_TM, D_MODEL), f32),
                        pltpu.VMEM((MOE_TM, D_MODEL), bf16),
                        pltpu.SemaphoreType.DMA(()),
                        pltpu.SemaphoreType.DMA(())],
    )
    return pl.pallas_call(
        _moe_kernel,
        grid_spec=grid_spec,
        out_shape=jax.ShapeDtypeStruct((rows, D_MODEL), bf16),
        compiler_params=_cparams(("arbitrary", "arbitrary")),
        name="moe_experts",
    )(*items,
      xs, w_up, w_up, w_down,
      b_up.reshape(N_EXPERTS, 1, 2 * D_FF), b_up.reshape(N_EXPERTS, 1, 2 * D_FF),
      b_down.reshape(N_EXPERTS, 1, D_MODEL))


FIN_TM = 256


def _final_kernel(h_ref, og_ref, gate_ref, g2_ref, fw_ref, y_ref):
    gate = gate_ref[...]
    ff = og_ref[0].astype(f32) * gate[:, 0:1]
    for j in range(1, TOP_K):
        ff = ff + og_ref[j].astype(f32) * gate[:, j:j + 1]
    h = h_ref[...] + g2_ref[...] * ff
    y_ref[...] = _rms(h) * fw_ref[...]


def _final(h, og, gate, g2, fw, row0, per_row_mod, seq_len):
    t = h.shape[0]
    tm = FIN_TM
    blk0 = row0 // tm
    if per_row_mod:
        mod_spec = pl.BlockSpec((tm, D_MODEL), lambda m: (m, 0))
    else:
        tiles_per_seq = seq_len // tm
        mod_spec = pl.BlockSpec((None, 1, D_MODEL), lambda m: (m // tiles_per_seq, 0, 0))
    return pl.pallas_call(
        _final_kernel,
        grid=(t // tm,),
        in_specs=[pl.BlockSpec((tm, D_MODEL), lambda m: (m, 0)),
                  pl.BlockSpec((TOP_K, tm, D_MODEL), lambda m: (0, blk0 + m, 0)),
                  pl.BlockSpec((tm, TOP_K), lambda m: (blk0 + m, 0)),
                  mod_spec,
                  pl.BlockSpec((1, D_MODEL), lambda m: (0, 0))],
        out_specs=pl.BlockSpec((tm, D_MODEL), lambda m: (m, 0)),
        out_shape=jax.ShapeDtypeStruct((t, D_MODEL), f32),
        compiler_params=_cparams(("arbitrary",)),
        name="final_norm",
    )(h, og, gate, g2, fw)


def _routing(logits):
    t = logits.shape[0]
    n_pairs = t * TOP_K
    top_v, top_e = lax.top_k(logits, TOP_K)
    gate = jax.nn.softmax(top_v, axis=-1)
    e_flat = top_e.reshape(-1)
    onehot = (e_flat[:, None] == jnp.arange(N_EXPERTS, dtype=e_flat.dtype)[None, :]).astype(jnp.int32)
    csum = jnp.cumsum(onehot, axis=0)
    rank = jnp.sum(onehot * csum, axis=1) - 1
    counts = csum[-1]
    nblk = (counts + MOE_RB - 1) // MOE_RB
    blk_end = jnp.cumsum(nblk)
    blk_start = blk_end - nblk
    dest = blk_start[e_flat] * MOE_RB + rank
    n_rows = (n_pairs // MOE_RB + N_EXPERTS) * MOE_RB
    order = jnp.argsort(e_flat, stable=True).astype(jnp.int32)
    raw_start = jnp.cumsum(counts) - counts
    r = jnp.arange(n_rows, dtype=jnp.int32)
    e_row = jnp.minimum(jnp.searchsorted(blk_end * MOE_RB, r, side='right'), N_EXPERTS - 1)
    off = r - blk_start[e_row] * MOE_RB
    src_pair = order[jnp.clip(raw_start[e_row] + off, 0, n_pairs - 1)]
    src_tok = jnp.where(off < counts[e_row], src_pair // TOP_K, 0).astype(jnp.int32)

    n_items = N_EXPERTS + n_rows // MOE_TM
    nit = (nblk + MOE_KB - 1) // MOE_KB
    it_end = jnp.cumsum(nit)
    it_start = it_end - nit
    idx = jnp.arange(n_items, dtype=jnp.int32)
    valid = idx < it_end[-1]
    last_valid = jnp.maximum(it_end[-1] - 1, 0)
    idx_c = jnp.minimum(idx, last_valid)
    e_of = jnp.minimum(jnp.searchsorted(it_end, idx_c, side='right'), N_EXPERTS - 1).astype(jnp.int32)
    k = idx_c - it_start[e_of]
    item_nb = jnp.where(valid, jnp.clip(nblk[e_of] - k * MOE_KB, 0, MOE_KB), 0).astype(jnp.int32)
    spare = idx - it_end[-1]
    tail_blk = blk_end[-1] + spare * MOE_KB
    item_nz = jnp.where(valid, 0, jnp.clip(n_rows // MOE_RB - tail_blk, 0, MOE_KB)).astype(jnp.int32)
    item_blk = jnp.where(valid, blk_start[e_of] + k * MOE_KB, tail_blk).astype(jnp.int32)
    items = (e_of, item_blk, item_nb, valid.astype(jnp.int32), item_nz)
    return gate, dest.reshape(t, TOP_K), src_tok, items


def kernel(x_prompt, x_sample, state_gla, state_conv, c_prompt, c_sample, w_ada, b_ada, norm1_w,
           w_in, w_gk_up, b_gk, gla_norm_w, w_conv, w_out, norm2_w, w_router, b_router, w_up, b_up,
           w_down, b_down, final_norm_w):
    n_p, seq_p, _ = x_prompt.shape
    n_s, seq_s, _ = x_sample.shape
    t_p, t_s = n_p * seq_p, n_s * seq_s
    assert seq_s == GS_SEQ and w_ada.shape[0] == 1

    c_all = jnp.concatenate([c_prompt, c_sample], axis=0)
    pad = (-c_all.shape[0]) % 16
    c_all = jnp.pad(c_all, ((0, pad), (0, 0)))
    mod = _ada(c_all, w_ada[0], b_ada[0])
    mod_p = [mod[:n_p, i * D_MODEL:(i + 1) * D_MODEL].reshape(n_p, 1, D_MODEL) for i in range(N_MOD)]
    mod_s = [jnp.repeat(mod[n_p:n_p + n_s, i * D_MODEL:(i + 1) * D_MODEL], seq_s, axis=0) for i in range(N_MOD)]

    w_in0 = w_in[0]
    a0 = OFF_G
    w_main = jnp.concatenate([w_in0[:, :a0], w_in0[:, a0 + GLA_LOW_RANK:]], axis=1).astype(bf16)
    w_alow = jnp.pad(w_in0[:, a0:a0 + GLA_LOW_RANK], ((0, 0), (0, LANES - GLA_LOW_RANK))).astype(bf16)
    w_gk = jnp.pad(w_gk_up[0], ((0, LANES - GLA_LOW_RANK), (0, 0))).astype(bf16)
    bgk = b_gk[0].reshape(1, GLA_DK)
    wo = w_out[0].astype(bf16)
    wr = jnp.pad(w_router[0], ((0, 0), (0, LANES - N_EXPERTS)))
    wr_hi = wr.astype(bf16)
    wr_lo = (wr - wr_hi.astype(f32)).astype(bf16)
    br = jnp.pad(b_router[0], (0, LANES - N_EXPERTS)).reshape(1, LANES)
    n1w = norm1_w[0].reshape(1, D_MODEL)
    n2w = norm2_w[0].reshape(1, D_MODEL)
    gnw = gla_norm_w[0].reshape(1, GLA_HV)
    fw = final_norm_w.reshape(1, D_MODEL)
    wc = w_conv[0]

    xp = x_prompt.reshape(t_p, D_MODEL)
    xs_ = x_sample.reshape(t_s, D_MODEL)

    proj_p, la_p = _in_proj(xp, mod_p[1], mod_p[0], n1w, w_main, w_alow, w_gk, bgk, 1024, False)
    proj_s, la_s = _in_proj(xs_, mod_s[1], mod_s[0], n1w, w_main, w_alow, w_gk, bgk, t_s, True)

    gla0 = jnp.zeros((n_p, GLA_HEADS, GLA_HK, GLA_HV), f32)
    conv0 = jnp.zeros((n_p, CONV_K - 1, CONV_CH), f32)
    oa_p, gla_p = _gla_scan(proj_p, la_p, gnw, gla0, n_p, seq_p)
    oa_s, gla_s = _gla_step(proj_s, la_s, gnw, state_gla[0])

    h_p, hn_p, lg_p, ut_p = _post(xp, oa_p, proj_p, conv0, wc, mod_p[2], mod_p[4], mod_p[3], n2w, wo,
                                  wr_hi, wr_lo, br, True, seq_p)
    prev_s = (jnp.repeat(state_conv[0][:, 0], seq_s, axis=0), jnp.repeat(state_conv[0][:, 1], seq_s, axis=0))
    h_s, hn_s, lg_s, u_s = _post(xs_, oa_s, proj_s, prev_s, wc, mod_s[2], mod_s[4], mod_s[3], n2w, wo,
                                 wr_hi, wr_lo, br, False, seq_s)
    conv_p = ut_p.reshape(n_p, seq_p // POST_TM, 8, CONV_CH)[:, -1, 8 - (CONV_K - 1):]
    conv_s = u_s.reshape(n_s, seq_s, CONV_CH)[:, seq_s - (CONV_K - 1):]

    hn = jnp.concatenate([hn_p, hn_s], axis=0)
    logits = jnp.concatenate([lg_p, lg_s], axis=0)[:, :N_EXPERTS]
    gate, dest, src_tok, items = _routing(logits)
    x_rows = hn[src_tok]
    out_rows = _moe(x_rows, items, w_up[0], b_up[0], w_down[0], b_down[0])
    og = out_rows[dest.T.reshape(-1)].reshape(TOP_K, t_p + t_s, D_MODEL)

    y_p = _final(h_p, og, gate, mod_p[5], fw, 0, False, seq_p)
    y_s = _final(h_s, og, gate, mod_s[5], fw, t_p, True, seq_s)

    return (y_p.reshape(n_p, seq_p, D_MODEL), y_s.reshape(n_s, seq_s, D_MODEL),
            gla_p[None], conv_p[None], gla_s[None], conv_s[None])
```

```python
import functools

import jax
import jax.numpy as jnp
from jax import lax
from jax.experimental import pallas as pl
from jax.experimental.pallas import tpu as pltpu

f32 = jnp.float32
bf16 = jnp.bfloat16

D_MODEL = 2048
N_MOD = 6
GLA_HEADS = 4
GLA_DK = 512
GLA_DV = 1024
GLA_HK = 128
GLA_HV = 256
GLA_LOW_RANK = 16
GLA_TAU = 16.0
GLA_CHUNK = 64
CONV_CH = 1024
CONV_K = 3
N_EXPERTS = 32
TOP_K = 4
D_FF = 2048
SWIGLU_LIMIT = 7.0
SWIGLU_ALPHA = 1.702
EPS = 1e-6

LANES = 128
D_MAIN = 10240
VMEM_LIMIT = 56 * 1024 * 1024

OFF_Q, OFF_K, OFF_V, OFF_G = 0, 512, 1024, 2048
OFF_CB, OFF_CC, OFF_CH, OFF_GA, OFF_GB = 3072, 4096, 5120, 6144, 8192

ROUTER_PAD = -1e30

HIGHEST = lax.Precision.HIGHEST


def _cparams(sem):
    return pltpu.CompilerParams(dimension_semantics=sem, vmem_limit_bytes=VMEM_LIMIT)


def _dot(a, b):
    return jnp.dot(a, b, preferred_element_type=f32)


def _dot_nt(a, b):
    return lax.dot_general(a, b, (((1,), (1,)), ((), ())), preferred_element_type=f32)


def _dot_exact(a, b):
    return jnp.dot(a, b, precision=HIGHEST, preferred_element_type=f32)


def _rms(x):
    return x * lax.rsqrt(jnp.mean(x * x, axis=-1, keepdims=True) + EPS)


ADA_TN = 1024


def _ada_kernel(c_ref, w_ref, b_ref, o_ref):
    c = c_ref[...]
    s = (c * jax.nn.sigmoid(c)).astype(bf16)
    o_ref[...] = _dot(s, w_ref[...].astype(bf16)) + b_ref[...]


def _ada(c, w, b):
    rows = c.shape[0]
    n = w.shape[1]
    return pl.pallas_call(
        _ada_kernel,
        grid=(n // ADA_TN,),
        in_specs=[pl.BlockSpec((rows, D_MODEL), lambda j: (0, 0)),
                  pl.BlockSpec((D_MODEL, ADA_TN), lambda j: (0, j)),
                  pl.BlockSpec((1, ADA_TN), lambda j: (0, j))],
        out_specs=pl.BlockSpec((rows, ADA_TN), lambda j: (0, j)),
        out_shape=jax.ShapeDtypeStruct((rows, n), f32),
        compiler_params=_cparams(("arbitrary",)),
        name="adaln",
    )(c, w, b.reshape(1, n))


IN_TN = 1024


def _in_kernel(x_ref, sc_ref, sh_ref, nw_ref, w_ref, wa_ref, wgk_ref, bgk_ref,
               proj_ref, la_ref, xn_ref):
    @pl.when(pl.program_id(1) == 0)
    def _():
        xn = _rms(x_ref[...]) * nw_ref[...] * (1.0 + sc_ref[...]) + sh_ref[...]
        xnb = xn.astype(bf16)
        xn_ref[...] = xnb
        a_low = _dot(xnb, wa_ref[...])
        z = _dot(a_low.astype(bf16), wgk_ref[...]) + bgk_ref[...]
        la_ref[...] = (jnp.minimum(z, 0.0) - jnp.log(1.0 + jnp.exp(-jnp.abs(z)))) * (1.0 / GLA_TAU)

    proj_ref[...] = _dot(xn_ref[...], w_ref[...]).astype(bf16)


def _in_proj(x2, sc, sh, nw, w_main, w_alow, w_gk, b_gk, tm, per_row_mod):
    t = x2.shape[0]
    if per_row_mod:
        mod_spec = pl.BlockSpec((tm, D_MODEL), lambda m, n: (m, 0))
    else:
        tiles_per_seq = (t // sc.shape[0]) // tm
        mod_spec = pl.BlockSpec((None, 1, D_MODEL), lambda m, n: (m // tiles_per_seq, 0, 0))
    return pl.pallas_call(
        _in_kernel,
        grid=(t // tm, D_MAIN // IN_TN),
        in_specs=[pl.BlockSpec((tm, D_MODEL), lambda m, n: (m, 0)),
                  mod_spec, mod_spec,
                  pl.BlockSpec((1, D_MODEL), lambda m, n: (0, 0)),
                  pl.BlockSpec((D_MODEL, IN_TN), lambda m, n: (0, n)),
                  pl.BlockSpec((D_MODEL, LANES), lambda m, n: (0, 0)),
                  pl.BlockSpec((LANES, GLA_DK), lambda m, n: (0, 0)),
                  pl.BlockSpec((1, GLA_DK), lambda m, n: (0, 0))],
        out_specs=[pl.BlockSpec((tm, IN_TN), lambda m, n: (m, n)),
                   pl.BlockSpec((tm, GLA_DK), lambda m, n: (m, 0))],
        out_shape=[jax.ShapeDtypeStruct((t, D_MAIN), bf16),
                   jax.ShapeDtypeStruct((t, GLA_DK), f32)],
        scratch_shapes=[pltpu.VMEM((tm, D_MODEL), bf16)],
        compiler_params=_cparams(("arbitrary", "arbitrary")),
        name="in_proj",
    )(x2, sc, sh, nw, w_main, w_alow, w_gk, b_gk)


GLA_R = 256


def _gla_norm_gate(o, nw, g):
    gf = g.astype(f32)
    return _rms(o) * nw * (gf * jax.nn.sigmoid(gf))


def _gla_scan_kernel(q_ref, k_ref, v_ref, g_ref, la_ref, nw_ref, s0_ref, o_ref, sout_ref, s_ref):
    step = pl.program_id(1)

    @pl.when(step == 0)
    def _():
        s_ref[...] = s0_ref[...]

    r, c = GLA_R, GLA_CHUNK
    n_chunks = r // c
    row = lax.broadcasted_iota(jnp.int32, (r, r), 0)
    col = lax.broadcasted_iota(jnp.int32, (r, r), 1)
    same_chunk = (row // c) == (col // c)
    causal = same_chunk & (col <= row)

    la = la_ref[...]
    b = _dot_exact(causal.astype(f32), la)
    b_mid = jnp.concatenate(
        [jnp.broadcast_to(b[i * c + c // 2:i * c + c // 2 + 1], (c, GLA_DK)) for i in range(n_chunks)], axis=0)
    b_last = jnp.concatenate(
        [jnp.broadcast_to(b[i * c + c - 1:i * c + c], (c, GLA_DK)) for i in range(n_chunks)], axis=0)
    q = q_ref[...].astype(f32) * (GLA_HK ** -0.5)
    k = k_ref[...].astype(f32)
    qs = (q * jnp.exp(b - b_mid)).astype(bf16)
    ks = (k * jnp.exp(b_mid - b)).astype(bf16)
    qd = (q * jnp.exp(b)).astype(bf16)
    kd_t = (k * jnp.exp(b_last - b)).T.astype(bf16)
    la_t = la.T
    lane = lax.broadcasted_iota(jnp.int32, (GLA_HK, r), 1)
    nw = nw_ref[...]

    for h in range(GLA_HEADS):
        hk = slice(h * GLA_HK, (h + 1) * GLA_HK)
        hv = slice(h * GLA_HV, (h + 1) * GLA_HV)
        v_h = v_ref[:, hv]
        scores = _dot_nt(qs[:, hk], ks[:, hk])
        o_intra = _dot(jnp.where(causal, scores, 0.0).astype(bf16), v_h)
        s = s_ref[h]
        for i in range(n_chunks):
            rows = slice(i * c, (i + 1) * c)
            in_chunk = (lane // c) == i
            o = o_intra[rows] + _dot(qd[rows, hk], s.astype(bf16))
            o_ref[rows, hv] = _gla_norm_gate(o, nw, g_ref[rows, hv]).astype(bf16)
            decay = jnp.exp(jnp.sum(jnp.where(in_chunk, la_t[hk], 0.0), axis=1, keepdims=True))
            s = decay * s + _dot(jnp.where(in_chunk, kd_t[hk], jnp.zeros_like(kd_t[hk])), v_h)
        s_ref[h] = s

    @pl.when(step == pl.num_programs(1) - 1)
    def _():
        sout_ref[...] = s_ref[...]


def _gla_scan(proj, la, nw, s0, n_seq, seq_len):
    steps = seq_len // GLA_R
    t = n_seq * seq_len

    def rows(b, s):
        return b * steps + s

    return pl.pallas_call(
        _gla_scan_kernel,
        grid=(n_seq, steps),
        in_specs=[pl.BlockSpec((GLA_R, GLA_DK), lambda b, s: (rows(b, s), OFF_Q // GLA_DK)),
                  pl.BlockSpec((GLA_R, GLA_DK), lambda b, s: (rows(b, s), OFF_K // GLA_DK)),
                  pl.BlockSpec((GLA_R, GLA_DV), lambda b, s: (rows(b, s), OFF_V // GLA_DV)),
                  pl.BlockSpec((GLA_R, GLA_DV), lambda b, s: (rows(b, s), OFF_G // GLA_DV)),
                  pl.BlockSpec((GLA_R, GLA_DK), lambda b, s: (rows(b, s), 0)),
                  pl.BlockSpec((1, GLA_HV), lambda b, s: (0, 0)),
                  pl.BlockSpec((None, GLA_HEADS, GLA_HK, GLA_HV), lambda b, s: (b, 0, 0, 0))],
        out_specs=[pl.BlockSpec((GLA_R, GLA_DV), lambda b, s: (rows(b, s), 0)),
                   pl.BlockSpec((None, GLA_HEADS, GLA_HK, GLA_HV), lambda b, s: (b, 0, 0, 0))],
        out_shape=[jax.ShapeDtypeStruct((t, GLA_DV), bf16),
                   jax.ShapeDtypeStruct((n_seq, GLA_HEADS, GLA_HK, GLA_HV), f32)],
        scratch_shapes=[pltpu.VMEM((GLA_HEADS, GLA_HK, GLA_HV), f32)],
        compiler_params=_cparams(("arbitrary", "arbitrary")),
        name="gla_scan",
    )(proj, proj, proj, proj, la, nw, s0)


GS_SEQ = 4
GS_ROWS = 128
GS_B = GS_ROWS // GS_SEQ


def _gla_step_kernel(q_ref, k_ref, v_ref, g_ref, la_ref, nw_ref, s0_ref, o_ref, sout_ref,
                     oacc_ref, qd_ref, kdt_ref, lat_ref):
    j = pl.program_id(1)
    r = GS_ROWS
    row = lax.broadcasted_iota(jnp.int32, (r, r), 0)
    col = lax.broadcasted_iota(jnp.int32, (r, r), 1)

    @pl.when(j == 0)
    def _():
        same_seq = (row // GS_SEQ) == (col // GS_SEQ)
        causal = same_seq & (col <= row)
        la = la_ref[...]
        b = _dot_exact(causal.astype(f32), la)
        b_mid = _dot_exact((same_seq & (col % GS_SEQ <= GS_SEQ // 2)).astype(f32), la)
        b_last = _dot_exact(same_seq.astype(f32), la)
        q = q_ref[...].astype(f32) * (GLA_HK ** -0.5)
        k = k_ref[...].astype(f32)
        qs = (q * jnp.exp(b - b_mid)).astype(bf16)
        ks = (k * jnp.exp(b_mid - b)).astype(bf16)
        qd_ref[...] = (q * jnp.exp(b)).astype(bf16)
        kdt_ref[...] = (k * jnp.exp(b_last - b)).T.astype(bf16)
        lat_ref[...] = la.T
        for h in range(GLA_HEADS):
            hk = slice(h * GLA_HK, (h + 1) * GLA_HK)
            hv = slice(h * GLA_HV, (h + 1) * GLA_HV)
            scores = _dot_nt(qs[:, hk], ks[:, hk])
            oacc_ref[:, hv] = _dot(jnp.where(causal, scores, 0.0).astype(bf16), v_ref[:, hv])

    in_seq_lane = (col // GS_SEQ) == j
    in_seq_row = (row[:, :GLA_HK] // GS_SEQ) == j
    for h in range(GLA_HEADS):
        hk = slice(h * GLA_HK, (h + 1) * GLA_HK)
        hv = slice(h * GLA_HV, (h + 1) * GLA_HV)
        s0 = s0_ref[h]
        kd_t = kdt_ref[hk, :]
        ds = _dot(jnp.where(in_seq_lane, kd_t, jnp.zeros_like(kd_t)), v_ref[:, hv])
        decay = jnp.exp(jnp.sum(jnp.where(in_seq_lane, lat_ref[hk, :], 0.0), axis=1, keepdims=True))
        sout_ref[h] = decay * s0 + ds
        qd = qd_ref[:, hk]
        oacc_ref[:, hv] += _dot(jnp.where(in_seq_row, qd, jnp.zeros_like(qd)), s0.astype(bf16))

    @pl.when(j == pl.num_programs(1) - 1)
    def _():
        nw = nw_ref[...]
        for h in range(GLA_HEADS):
            hv = slice(h * GLA_HV, (h + 1) * GLA_HV)
            o_ref[:, hv] = _gla_norm_gate(oacc_ref[:, hv], nw, g_ref[:, hv]).astype(bf16)


def _gla_step(proj, la, nw, s0):
    n_seq = s0.shape[0]
    t = n_seq * GS_SEQ
    groups = t // GS_ROWS
    return pl.pallas_call(
        _gla_step_kernel,
        grid=(groups, GS_B),
        in_specs=[pl.BlockSpec((GS_ROWS, GLA_DK), lambda g, j: (g, OFF_Q // GLA_DK)),
                  pl.BlockSpec((GS_ROWS, GLA_DK), lambda g, j: (g, OFF_K // GLA_DK)),
                  pl.BlockSpec((GS_ROWS, GLA_DV), lambda g, j: (g, OFF_V // GLA_DV)),
                  pl.BlockSpec((GS_ROWS, GLA_DV), lambda g, j: (g, OFF_G // GLA_DV)),
                  pl.BlockSpec((GS_ROWS, GLA_DK), lambda g, j: (g, 0)),
                  pl.BlockSpec((1, GLA_HV), lambda g, j: (0, 0)),
                  pl.BlockSpec((None, GLA_HEADS, GLA_HK, GLA_HV), lambda g, j: (g * GS_B + j, 0, 0, 0))],
        out_specs=[pl.BlockSpec((GS_ROWS, GLA_DV), lambda g, j: (g, 0)),
                   pl.BlockSpec((None, GLA_HEADS, GLA_HK, GLA_HV), lambda g, j: (g * GS_B + j, 0, 0, 0))],
        out_shape=[jax.ShapeDtypeStruct((t, GLA_DV), bf16),
                   jax.ShapeDtypeStruct((n_seq, GLA_HEADS, GLA_HK, GLA_HV), f32)],
        scratch_shapes=[pltpu.VMEM((GS_ROWS, GLA_DV), f32),
                        pltpu.VMEM((GS_ROWS, GLA_DK), bf16),
                        pltpu.VMEM((GLA_DK, GS_ROWS), bf16),
                        pltpu.VMEM((GLA_DK, GS_ROWS), f32)],
        compiler_params=_cparams(("arbitrary", "arbitrary")),
        name="gla_step",
    )(proj, proj, proj, proj, la, nw, s0)


POST_TM = 256
HALO = 16


def _post_kernel(long_seq, tiles_per_seq, *refs):
    if long_seq:
        (x_ref, oa_ref, cb_ref, cc_ref, ch_ref, ga_ref, gb_ref, hcc_ref, hch_ref, cbuf_ref,
         wc_ref, g1_ref, sc_ref, sh_ref, nw_ref, wo_ref, wrh_ref, wrl_ref, br_ref, _hn_alias,
         h_ref, hn_ref, gate_ref, eid_ref, u_ref) = refs
    else:
        (x_ref, oa_ref, cb_ref, cc_ref, ch_ref, ga_ref, gb_ref, p0_ref, p1_ref,
         wc_ref, g1_ref, sc_ref, sh_ref, nw_ref, wo_ref, wrh_ref, wrl_ref, br_ref, _hn_alias,
         h_ref, hn_ref, gate_ref, eid_ref, u_ref) = refs
    tm = x_ref.shape[0]
    u = cc_ref[...].astype(f32) * ch_ref[...].astype(f32)
    row = lax.broadcasted_iota(jnp.int32, (tm, CONV_CH), 0)
    if long_seq:
        pos = row
        first = (pl.program_id(0) % tiles_per_seq) == 0
        halo = hcc_ref[HALO - 2:HALO, :].astype(f32) * hch_ref[HALO - 2:HALO, :].astype(f32)
        cbuf = cbuf_ref[...]
        p0 = jnp.where(first, cbuf[0:1], halo[0:1])
        p1 = jnp.where(first, cbuf[1:2], halo[1:2])
        u_ref[...] = u[tm - 8:tm]
    else:
        pos = row % GS_SEQ
        p0 = p0_ref[...]
        p1 = p1_ref[...]
        u_ref[...] = u
    u1 = jnp.where(pos == 0, p1, pltpu.roll(u, 1, 0))
    u2 = jnp.where(pos == 0, p0, jnp.where(pos == 1, p1, pltpu.roll(u, 2, 0)))
    wc = wc_ref[...]
    conv = wc[0:1] * u2 + wc[1:2] * u1 + wc[2:3] * u
    ob = (cb_ref[...].astype(f32) * conv).astype(bf16)

    ya = _dot(oa_ref[...], wo_ref[0:GLA_DV, :])
    yb = _dot(ob, wo_ref[GLA_DV:GLA_DV + CONV_CH, :])
    y = jax.nn.sigmoid(ga_ref[...].astype(f32)) * ya + jax.nn.sigmoid(gb_ref[...].astype(f32)) * yb
    h = x_ref[...] + g1_ref[...] * y
    h_ref[...] = h
    hn = _rms(h) * nw_ref[...] * (1.0 + sc_ref[...]) + sh_ref[...]
    hn_hi = hn.astype(bf16)
    hn_ref[...] = hn_hi
    hn_lo = (hn - hn_hi.astype(f32)).astype(bf16)
    lg = (_dot(hn_hi, wrh_ref[...]) + _dot(hn_hi, wrl_ref[...]) + _dot(hn_lo, wrh_ref[...])
          + br_ref[...])

    lane = lax.broadcasted_iota(jnp.int32, (tm, LANES), 1)
    lane_f = lane.astype(f32)
    vals, idxs = [], []
    for _ in range(TOP_K):
        m = jnp.max(lg, axis=-1, keepdims=True)
        idx = jnp.min(jnp.where(lg == m, lane_f, float(LANES)), axis=-1, keepdims=True)
        vals.append(m)
        idxs.append(idx)
        lg = jnp.where(lane_f == idx, -jnp.inf, lg)
    exps = [jnp.exp(v - vals[0]) for v in vals]
    den = exps[0] + exps[1] + exps[2] + exps[3]
    gate = jnp.zeros((tm, LANES), f32)
    eid = jnp.zeros((tm, LANES), f32)
    for j in range(TOP_K):
        gate = jnp.where(lane == j, exps[j] / den, gate)
        eid = jnp.where(lane == j, idxs[j], eid)
    gate_ref[...] = gate
    eid_ref[...] = eid.astype(jnp.int32)


def _post(x2, oa, proj, prev, wc, g1, sc2, sh2, nw, wo, wrh, wrl, br, hn_buf, row0, long_seq, seq_len):
    t = x2.shape[0]
    tm = POST_TM
    n_tiles = t // tm
    blk0 = row0 // tm
    tiles_per_seq = seq_len // tm if long_seq else 1

    def colblk(off, width):
        return pl.BlockSpec((tm, width), lambda m: (m, off // width))

    const = lambda shape: pl.BlockSpec(shape, lambda m: tuple(0 for _ in shape))
    if long_seq:
        mod_spec = pl.BlockSpec((None, 1, D_MODEL), lambda m: (m // tiles_per_seq, 0, 0))
        halo_rows = lambda m: jnp.maximum(m * (tm // HALO) - 1, 0)
        prev_specs = [pl.BlockSpec((HALO, CONV_CH), lambda m: (halo_rows(m), OFF_CC // CONV_CH)),
                      pl.BlockSpec((HALO, CONV_CH), lambda m: (halo_rows(m), OFF_CH // CONV_CH)),
                      pl.BlockSpec((None, CONV_K - 1, CONV_CH), lambda m: (m // tiles_per_seq, 0, 0))]
        prev_args = [proj, proj, prev]
        u_spec = pl.BlockSpec((None, 8, CONV_CH), lambda m: (m, 0, 0))
        u_shape = jax.ShapeDtypeStruct((n_tiles, 8, CONV_CH), f32)
    else:
        mod_spec = pl.BlockSpec((tm, D_MODEL), lambda m: (m, 0))
        prev_specs = [pl.BlockSpec((tm, CONV_CH), lambda m: (m, 0))] * 2
        prev_args = list(prev)
        u_spec = pl.BlockSpec((tm, CONV_CH), lambda m: (m, 0))
        u_shape = jax.ShapeDtypeStruct((t, CONV_CH), f32)
    n_in = 7 + len(prev_specs) + 10
    return pl.pallas_call(
        functools.partial(_post_kernel, long_seq, tiles_per_seq),
        grid=(n_tiles,),
        in_specs=[pl.BlockSpec((tm, D_MODEL), lambda m: (m, 0)),
                  pl.BlockSpec((tm, GLA_DV), lambda m: (m, 0)),
                  colblk(OFF_CB, CONV_CH), colblk(OFF_CC, CONV_CH), colblk(OFF_CH, CONV_CH),
                  colblk(OFF_GA, D_MODEL), colblk(OFF_GB, D_MODEL)]
                 + prev_specs
                 + [const((CONV_K, CONV_CH)), mod_spec, mod_spec, mod_spec, const((1, D_MODEL)),
                    const((GLA_DV + CONV_CH, D_MODEL)), const((D_MODEL, LANES)), const((D_MODEL, LANES)),
                    const((1, LANES)), pl.BlockSpec(memory_space=pl.ANY)],
        out_specs=[pl.BlockSpec((tm, D_MODEL), lambda m: (m, 0)),
                   pl.BlockSpec((tm, D_MODEL), lambda m: (blk0 + m, 0)),
                   pl.BlockSpec((tm, LANES), lambda m: (m, 0)),
                   pl.BlockSpec((tm, LANES), lambda m: (m, 0)),
                   u_spec],
        out_shape=[jax.ShapeDtypeStruct((t, D_MODEL), f32),
                   jax.ShapeDtypeStruct(hn_buf.shape, bf16),
                   jax.ShapeDtypeStruct((t, LANES), f32),
                   jax.ShapeDtypeStruct((t, LANES), jnp.int32),
                   u_shape],
        input_output_aliases={n_in - 1: 1},
        compiler_params=_cparams(("arbitrary",)),
        name="post_mix",
    )(x2, oa, proj, proj, proj, proj, proj, *prev_args, wc, g1, sc2, sh2, nw, wo, wrh, wrl, br, hn_buf)


MOE_RB = 256
MOE_TM = 1024
MOE_TF = 256
MOE_KB = MOE_TM // MOE_RB
MOE_NF = D_FF // MOE_TF


def _moe_kernel(ie_ref, ib_ref, in_ref, iv_ref, iz_ref,
                x_hbm, wg_ref, wl_ref, wd_ref, bg_ref, bl_ref, bd_ref, out_hbm,
                xbuf, acc, obuf, sem_in, sem_out):
    i = pl.program_id(0)
    f = pl.program_id(1)
    nb = in_ref[i]
    blk0 = ib_ref[i]
    nz = iz_ref[i]

    def x_copy(jb):
        return pltpu.make_async_copy(x_hbm.at[pl.ds((blk0 + jb) * MOE_RB, MOE_RB)],
                                     xbuf.at[pl.ds(jb * MOE_RB, MOE_RB)], sem_in)

    def out_copy(jb):
        return pltpu.make_async_copy(obuf.at[pl.ds(jb * MOE_RB, MOE_RB)],
                                     out_hbm.at[pl.ds((blk0 + jb) * MOE_RB, MOE_RB)], sem_out)

    @pl.when(f == 0)
    def _():
        def start(jb, carry):
            x_copy(jb).start()
            return carry

        def wait(jb, carry):
            x_copy(jb).wait()
            return carry

        lax.fori_loop(0, nb, start, 0)
        lax.fori_loop(0, nb, wait, 0)

        def fill(jb, carry):
            obuf[pl.ds(pl.multiple_of(jb * MOE_RB, MOE_RB), MOE_RB), :] = jnp.zeros((MOE_RB, D_MODEL), bf16)
            out_copy(jb).start()
            return carry

        def fill_wait(jb, carry):
            out_copy(jb).wait()
            return carry

        lax.fori_loop(0, nz, fill, 0)
        lax.fori_loop(0, nz, fill_wait, 0)

    wg = wg_ref[...].astype(bf16)
    wl = wl_ref[...].astype(bf16)
    wd = wd_ref[...].astype(bf16)
    bg = bg_ref[...]
    bl = bl_ref[...]

    def block(jb, carry):
        rows = pl.ds(pl.multiple_of(jb * MOE_RB, MOE_RB), MOE_RB)
        xb = xbuf[rows, :]
        hg = jnp.minimum(_dot(xb, wg) + bg, SWIGLU_LIMIT)
        hl = jnp.clip(_dot(xb, wl) + bl, -SWIGLU_LIMIT, SWIGLU_LIMIT)
        act = (hl + 1.0) * hg * jax.nn.sigmoid(SWIGLU_ALPHA * hg)
        part = _dot(act.astype(bf16), wd)

        @pl.when(f == 0)
        def _():
            acc[rows, :] = part + bd_ref[...]

        @pl.when(f > 0)
        def _():
            acc[rows, :] += part

        @pl.when(f == MOE_NF - 1)
        def _():
            obuf[rows, :] = acc[rows, :].astype(bf16)
            out_copy(jb).start()

        return carry

    lax.fori_loop(0, nb, block, 0)

    @pl.when(f == MOE_NF - 1)
    def _():
        def wait(jb, carry):
            out_copy(jb).wait()
            return carry

        lax.fori_loop(0, nb, wait, 0)


def _moe(xs, items, w_up, b_up, w_down, b_down):
    rows = xs.shape[0]
    n_items = items[0].shape[0]

    def f_eff(i, f, tabs):
        valid = tabs[3][i]
        return f * valid + (MOE_NF - 1) * (1 - valid)

    def expert(i, tabs):
        return tabs[0][i]

    grid_spec = pltpu.PrefetchScalarGridSpec(
        num_scalar_prefetch=len(items),
        grid=(n_items, MOE_NF),
        in_specs=[pl.BlockSpec(memory_space=pl.ANY),
                  pl.BlockSpec((None, D_MODEL, MOE_TF), lambda i, f, *t: (expert(i, t), 0, f_eff(i, f, t))),
                  pl.BlockSpec((None, D_MODEL, MOE_TF), lambda i, f, *t: (expert(i, t), 0, MOE_NF + f_eff(i, f, t))),
                  pl.BlockSpec((None, MOE_TF, D_MODEL), lambda i, f, *t: (expert(i, t), f_eff(i, f, t), 0)),
                  pl.BlockSpec((None, 1, MOE_TF), lambda i, f, *t: (expert(i, t), 0, f_eff(i, f, t))),
                  pl.BlockSpec((None, 1, MOE_TF), lambda i, f, *t: (expert(i, t), 0, MOE_NF + f_eff(i, f, t))),
                  pl.BlockSpec((None, 1, D_MODEL), lambda i, f, *t: (expert(i, t), 0, 0))],
        out_specs=pl.BlockSpec(memory_space=pl.ANY),
        scratch_shapes=[pltpu.VMEM((MOE_TM, D_MODEL), bf16),
                        pltpu.VMEM((MOE_TM, D_MODEL), f32),
                        pltpu.VMEM((MOE_TM, D_MODEL), bf16),
                        pltpu.SemaphoreType.DMA(()),
                        pltpu.SemaphoreType.DMA(())],
    )
    return pl.pallas_call(
        _moe_kernel,
        grid_spec=grid_spec,
        out_shape=jax.ShapeDtypeStruct((rows, D_MODEL), bf16),
        compiler_params=_cparams(("arbitrary", "arbitrary")),
        name="moe_experts",
    )(*items,
      xs, w_up, w_up, w_down,
      b_up.reshape(N_EXPERTS, 1, 2 * D_FF), b_up.reshape(N_EXPERTS, 1, 2 * D_FF),
      b_down.reshape(N_EXPERTS, 1, D_MODEL))


FIN_TM = 256


def _final_kernel(h_ref, og_ref, gate_ref, g2_ref, fw_ref, y_ref):
    gate = gate_ref[...]
    ff = og_ref[0].astype(f32) * gate[:, 0:1]
    for j in range(1, TOP_K):
        ff = ff + og_ref[j].astype(f32) * gate[:, j:j + 1]
    h = h_ref[...] + g2_ref[...] * ff
    y_ref[...] = _rms(h) * fw_ref[...]


def _final(h, og, gate, g2, fw, row0, per_row_mod, seq_len):
    t = h.shape[0]
    tm = FIN_TM
    blk0 = row0 // tm
    if per_row_mod:
        mod_spec = pl.BlockSpec((tm, D_MODEL), lambda m: (m, 0))
    else:
        tiles_per_seq = seq_len // tm
        mod_spec = pl.BlockSpec((None, 1, D_MODEL), lambda m: (m // tiles_per_seq, 0, 0))
    return pl.pallas_call(
        _final_kernel,
        grid=(t // tm,),
        in_specs=[pl.BlockSpec((tm, D_MODEL), lambda m: (m, 0)),
                  pl.BlockSpec((TOP_K, tm, D_MODEL), lambda m: (0, blk0 + m, 0)),
                  pl.BlockSpec((tm, LANES), lambda m: (blk0 + m, 0)),
                  mod_spec,
                  pl.BlockSpec((1, D_MODEL), lambda m: (0, 0))],
        out_specs=pl.BlockSpec((tm, D_MODEL), lambda m: (m, 0)),
        out_shape=jax.ShapeDtypeStruct((t, D_MODEL), f32),
        compiler_params=_cparams(("arbitrary",)),
        name="final_norm",
    )(h, og, gate, g2, fw)


def _count_le(bounds, x):
    return jnp.sum((bounds[None, :] <= x[:, None]).astype(jnp.int32), axis=1)


def _routing(top_e):
    t = top_e.shape[0]
    n_pairs = t * TOP_K
    e_flat = top_e.reshape(-1)
    onehot = (e_flat[:, None] == jnp.arange(N_EXPERTS, dtype=e_flat.dtype)[None, :]).astype(jnp.int32)
    csum = jnp.cumsum(onehot, axis=0)
    rank = jnp.sum(onehot * csum, axis=1) - 1
    counts = csum[-1]
    nblk = (counts + MOE_RB - 1) // MOE_RB
    blk_end = jnp.cumsum(nblk)
    blk_start = blk_end - nblk
    dest = blk_start[e_flat] * MOE_RB + rank
    n_rows = (n_pairs // MOE_RB + N_EXPERTS) * MOE_RB
    order = jnp.argsort(e_flat, stable=True).astype(jnp.int32)
    raw_start = jnp.cumsum(counts) - counts
    r = jnp.arange(n_rows, dtype=jnp.int32)
    e_row = jnp.minimum(_count_le(blk_end * MOE_RB, r), N_EXPERTS - 1)
    off = r - blk_start[e_row] * MOE_RB
    src_pair = order[jnp.clip(raw_start[e_row] + off, 0, n_pairs - 1)]
    src_tok = jnp.where(off < counts[e_row], src_pair // TOP_K, 0).astype(jnp.int32)

    n_items = N_EXPERTS + n_rows // MOE_TM
    nit = (nblk + MOE_KB - 1) // MOE_KB
    it_end = jnp.cumsum(nit)
    it_start = it_end - nit
    idx = jnp.arange(n_items, dtype=jnp.int32)
    valid = idx < it_end[-1]
    last_valid = jnp.maximum(it_end[-1] - 1, 0)
    idx_c = jnp.minimum(idx, last_valid)
    e_of = jnp.minimum(_count_le(it_end, idx_c), N_EXPERTS - 1).astype(jnp.int32)
    k = idx_c - it_start[e_of]
    item_nb = jnp.where(valid, jnp.clip(nblk[e_of] - k * MOE_KB, 0, MOE_KB), 0).astype(jnp.int32)
    spare = idx - it_end[-1]
    tail_blk = blk_end[-1] + spare * MOE_KB
    item_nz = jnp.where(valid, 0, jnp.clip(n_rows // MOE_RB - tail_blk, 0, MOE_KB)).astype(jnp.int32)
    item_blk = jnp.where(valid, blk_start[e_of] + k * MOE_KB, tail_blk).astype(jnp.int32)
    items = (e_of, item_blk, item_nb, valid.astype(jnp.int32), item_nz)
    return dest.reshape(t, TOP_K), src_tok, items


def kernel(x_prompt, x_sample, state_gla, state_conv, c_prompt, c_sample, w_ada, b_ada, norm1_w,
           w_in, w_gk_up, b_gk, gla_norm_w, w_conv, w_out, norm2_w, w_router, b_router, w_up, b_up,
           w_down, b_down, final_norm_w):
    n_p, seq_p, _ = x_prompt.shape
    n_s, seq_s, _ = x_sample.shape
    t_p, t_s = n_p * seq_p, n_s * seq_s
    assert seq_s == GS_SEQ and w_ada.shape[0] == 1

    c_all = jnp.concatenate([c_prompt, c_sample], axis=0)
    pad = (-c_all.shape[0]) % 16
    c_all = jnp.pad(c_all, ((0, pad), (0, 0)))
    mod = _ada(c_all, w_ada[0], b_ada[0])
    mod_p = [mod[:n_p, i * D_MODEL:(i + 1) * D_MODEL].reshape(n_p, 1, D_MODEL) for i in range(N_MOD)]
    mod_s = [jnp.repeat(mod[n_p:n_p + n_s, i * D_MODEL:(i + 1) * D_MODEL], seq_s, axis=0) for i in range(N_MOD)]

    w_in0 = w_in[0]
    a0 = OFF_G
    w_main = jnp.concatenate([w_in0[:, :a0], w_in0[:, a0 + GLA_LOW_RANK:]], axis=1).astype(bf16)
    w_alow = jnp.pad(w_in0[:, a0:a0 + GLA_LOW_RANK], ((0, 0), (0, LANES - GLA_LOW_RANK))).astype(bf16)
    w_gk = jnp.pad(w_gk_up[0], ((0, LANES - GLA_LOW_RANK), (0, 0))).astype(bf16)
    bgk = b_gk[0].reshape(1, GLA_DK)
    wo = w_out[0].astype(bf16)
    wr = jnp.pad(w_router[0], ((0, 0), (0, LANES - N_EXPERTS)))
    wr_hi = wr.astype(bf16)
    wr_lo = (wr - wr_hi.astype(f32)).astype(bf16)
    br = jnp.pad(b_router[0], (0, LANES - N_EXPERTS), constant_values=ROUTER_PAD).reshape(1, LANES)
    n1w = norm1_w[0].reshape(1, D_MODEL)
    n2w = norm2_w[0].reshape(1, D_MODEL)
    gnw = gla_norm_w[0].reshape(1, GLA_HV)
    fw = final_norm_w.reshape(1, D_MODEL)
    wc = w_conv[0]

    xp = x_prompt.reshape(t_p, D_MODEL)
    xs_ = x_sample.reshape(t_s, D_MODEL)

    proj_p, la_p = _in_proj(xp, mod_p[1], mod_p[0], n1w, w_main, w_alow, w_gk, bgk, 1024, False)
    proj_s, la_s = _in_proj(xs_, mod_s[1], mod_s[0], n1w, w_main, w_alow, w_gk, bgk, t_s, True)

    gla0 = jnp.zeros((n_p, GLA_HEADS, GLA_HK, GLA_HV), f32)
    conv0 = jnp.zeros((n_p, CONV_K - 1, CONV_CH), f32)
    oa_p, gla_p = _gla_scan(proj_p, la_p, gnw, gla0, n_p, seq_p)
    oa_s, gla_s = _gla_step(proj_s, la_s, gnw, state_gla[0])

    hn = jnp.zeros((2 * (t_p + t_s), D_MODEL), bf16)
    h_p, hn, gate_p, eid_p, ut_p = _post(xp, oa_p, proj_p, conv0, wc, mod_p[2], mod_p[4], mod_p[3], n2w, wo,
                                         wr_hi, wr_lo, br, hn, 0, True, seq_p)
    prev_s = (jnp.repeat(state_conv[0][:, 0], seq_s, axis=0), jnp.repeat(state_conv[0][:, 1], seq_s, axis=0))
    h_s, hn, gate_s, eid_s, u_s = _post(xs_, oa_s, proj_s, prev_s, wc, mod_s[2], mod_s[4], mod_s[3], n2w, wo,
                                        wr_hi, wr_lo, br, hn, t_p, False, seq_s)
    conv_p = ut_p.reshape(n_p, seq_p // POST_TM, 8, CONV_CH)[:, -1, 8 - (CONV_K - 1):]
    conv_s = u_s.reshape(n_s, seq_s, CONV_CH)[:, seq_s - (CONV_K - 1):]

    gate = jnp.concatenate([gate_p, gate_s], axis=0)
    top_e = jnp.concatenate([eid_p, eid_s], axis=0)[:, :TOP_K]
    dest, src_tok, items = _routing(top_e)
    x_rows = hn[src_tok]
    out_rows = _moe(x_rows, items, w_up[0], b_up[0], w_down[0], b_down[0])
    og = out_rows[dest.T.reshape(-1)].reshape(TOP_K, t_p + t_s, D_MODEL)

    y_p = _final(h_p, og, gate, mod_p[5], fw, 0, False, seq_p)
    y_s = _final(h_s, og, gate, mod_s[5], fw, t_p, True, seq_s)

    return (y_p.reshape(n_p, seq_p, D_MODEL), y_s.reshape(n_s, seq_s, D_MODEL),
            gla_p[None], conv_p[None], gla_s[None], conv_s[None])
```

```python
import functools

import jax
import jax.numpy as jnp
from jax import lax
from jax.experimental import pallas as pl
from jax.experimental.pallas import tpu as pltpu

f32 = jnp.float32
bf16 = jnp.bfloat16

D_MODEL = 2048
N_MOD = 6
GLA_HEADS = 4
GLA_DK = 512
GLA_DV = 1024
GLA_HK = 128
GLA_HV = 256
GLA_LOW_RANK = 16
GLA_TAU = 16.0
GLA_CHUNK = 64
CONV_CH = 1024
CONV_K = 3
N_EXPERTS = 32
TOP_K = 4
D_FF = 2048
SWIGLU_LIMIT = 7.0
SWIGLU_ALPHA = 1.702
EPS = 1e-6

LANES = 128
D_MAIN = 10240
VMEM_LIMIT = 56 * 1024 * 1024

OFF_Q, OFF_K, OFF_V, OFF_G = 0, 512, 1024, 2048
OFF_CB, OFF_CC, OFF_CH, OFF_GA, OFF_GB = 3072, 4096, 5120, 6144, 8192

ROUTER_PAD = -1e30

HIGHEST = lax.Precision.HIGHEST


def _cparams(sem):
    return pltpu.CompilerParams(dimension_semantics=sem, vmem_limit_bytes=VMEM_LIMIT)


def _dot(a, b):
    return jnp.dot(a, b, preferred_element_type=f32)


def _dot_nt(a, b):
    return lax.dot_general(a, b, (((1,), (1,)), ((), ())), preferred_element_type=f32)


def _dot_exact(a, b):
    return jnp.dot(a, b, precision=HIGHEST, preferred_element_type=f32)


def _rms(x):
    return x * lax.rsqrt(jnp.mean(x * x, axis=-1, keepdims=True) + EPS)


ADA_TN = 1024


def _ada_kernel(c_ref, w_ref, b_ref, o_ref):
    c = c_ref[...]
    s = (c * jax.nn.sigmoid(c)).astype(bf16)
    o_ref[...] = _dot(s, w_ref[...].astype(bf16)) + b_ref[...]


def _ada(c, w, b):
    rows = c.shape[0]
    n = w.shape[1]
    return pl.pallas_call(
        _ada_kernel,
        grid=(n // ADA_TN,),
        in_specs=[pl.BlockSpec((rows, D_MODEL), lambda j: (0, 0)),
                  pl.BlockSpec((D_MODEL, ADA_TN), lambda j: (0, j)),
                  pl.BlockSpec((1, ADA_TN), lambda j: (0, j))],
        out_specs=pl.BlockSpec((rows, ADA_TN), lambda j: (0, j)),
        out_shape=jax.ShapeDtypeStruct((rows, n), f32),
        compiler_params=_cparams(("arbitrary",)),
        name="adaln",
    )(c, w, b.reshape(1, n))


IN_TN = 1024


def _in_kernel(x_ref, sc_ref, sh_ref, nw_ref, w_ref, wa_ref, wgk_ref, bgk_ref,
               proj_ref, la_ref, xn_ref):
    @pl.when(pl.program_id(1) == 0)
    def _():
        xn = _rms(x_ref[...]) * nw_ref[...] * (1.0 + sc_ref[...]) + sh_ref[...]
        xnb = xn.astype(bf16)
        xn_ref[...] = xnb
        a_low = _dot(xnb, wa_ref[...])
        z = _dot(a_low.astype(bf16), wgk_ref[...]) + bgk_ref[...]
        la_ref[...] = (jnp.minimum(z, 0.0) - jnp.log(1.0 + jnp.exp(-jnp.abs(z)))) * (1.0 / GLA_TAU)

    proj_ref[...] = _dot(xn_ref[...], w_ref[...]).astype(bf16)


def _in_proj(x2, sc, sh, nw, w_main, w_alow, w_gk, b_gk, tm, per_row_mod):
    t = x2.shape[0]
    if per_row_mod:
        mod_spec = pl.BlockSpec((tm, D_MODEL), lambda m, n: (m, 0))
    else:
        tiles_per_seq = (t // sc.shape[0]) // tm
        mod_spec = pl.BlockSpec((None, 1, D_MODEL), lambda m, n: (m // tiles_per_seq, 0, 0))
    return pl.pallas_call(
        _in_kernel,
        grid=(t // tm, D_MAIN // IN_TN),
        in_specs=[pl.BlockSpec((tm, D_MODEL), lambda m, n: (m, 0)),
                  mod_spec, mod_spec,
                  pl.BlockSpec((1, D_MODEL), lambda m, n: (0, 0)),
                  pl.BlockSpec((D_MODEL, IN_TN), lambda m, n: (0, n)),
                  pl.BlockSpec((D_MODEL, LANES), lambda m, n: (0, 0)),
                  pl.BlockSpec((LANES, GLA_DK), lambda m, n: (0, 0)),
                  pl.BlockSpec((1, GLA_DK), lambda m, n: (0, 0))],
        out_specs=[pl.BlockSpec((tm, IN_TN), lambda m, n: (m, n)),
                   pl.BlockSpec((tm, GLA_DK), lambda m, n: (m, 0))],
        out_shape=[jax.ShapeDtypeStruct((t, D_MAIN), bf16),
                   jax.ShapeDtypeStruct((t, GLA_DK), f32)],
        scratch_shapes=[pltpu.VMEM((tm, D_MODEL), bf16)],
        compiler_params=_cparams(("arbitrary", "arbitrary")),
        name="in_proj",
    )(x2, sc, sh, nw, w_main, w_alow, w_gk, b_gk)


GLA_R = 256


def _gla_norm_gate(o, nw, g):
    gf = g.astype(f32)
    return _rms(o) * nw * (gf * jax.nn.sigmoid(gf))


def _gla_scan_kernel(q_ref, k_ref, v_ref, g_ref, la_ref, nw_ref, s0_ref, o_ref, sout_ref, s_ref):
    step = pl.program_id(1)

    @pl.when(step == 0)
    def _():
        s_ref[...] = s0_ref[...]

    r, c = GLA_R, GLA_CHUNK
    n_chunks = r // c
    row = lax.broadcasted_iota(jnp.int32, (r, r), 0)
    col = lax.broadcasted_iota(jnp.int32, (r, r), 1)
    same_chunk = (row // c) == (col // c)
    causal = same_chunk & (col <= row)

    la = la_ref[...]
    b = _dot_exact(causal.astype(f32), la)
    b_mid = jnp.concatenate(
        [jnp.broadcast_to(b[i * c + c // 2:i * c + c // 2 + 1], (c, GLA_DK)) for i in range(n_chunks)], axis=0)
    b_last = jnp.concatenate(
        [jnp.broadcast_to(b[i * c + c - 1:i * c + c], (c, GLA_DK)) for i in range(n_chunks)], axis=0)
    q = q_ref[...].astype(f32) * (GLA_HK ** -0.5)
    k = k_ref[...].astype(f32)
    qs = (q * jnp.exp(b - b_mid)).astype(bf16)
    ks = (k * jnp.exp(b_mid - b)).astype(bf16)
    qd = (q * jnp.exp(b)).astype(bf16)
    kd_t = (k * jnp.exp(b_last - b)).T.astype(bf16)
    la_t = la.T
    lane = lax.broadcasted_iota(jnp.int32, (GLA_HK, r), 1)
    nw = nw_ref[...]

    for h in range(GLA_HEADS):
        hk = slice(h * GLA_HK, (h + 1) * GLA_HK)
        hv = slice(h * GLA_HV, (h + 1) * GLA_HV)
        v_h = v_ref[:, hv]
        scores = _dot_nt(qs[:, hk], ks[:, hk])
        o_intra = _dot(jnp.where(causal, scores, 0.0).astype(bf16), v_h)
        s = s_ref[h]
        for i in range(n_chunks):
            rows = slice(i * c, (i + 1) * c)
            in_chunk = (lane // c) == i
            o = o_intra[rows] + _dot(qd[rows, hk], s.astype(bf16))
            o_ref[rows, hv] = _gla_norm_gate(o, nw, g_ref[rows, hv]).astype(bf16)
            decay = jnp.exp(jnp.sum(jnp.where(in_chunk, la_t[hk], 0.0), axis=1, keepdims=True))
            s = decay * s + _dot(jnp.where(in_chunk, kd_t[hk], jnp.zeros_like(kd_t[hk])), v_h)
        s_ref[h] = s

    @pl.when(step == pl.num_programs(1) - 1)
    def _():
        sout_ref[...] = s_ref[...]


def _gla_scan(proj, la, nw, s0, n_seq, seq_len):
    steps = seq_len // GLA_R
    t = n_seq * seq_len

    def rows(b, s):
        return b * steps + s

    return pl.pallas_call(
        _gla_scan_kernel,
        grid=(n_seq, steps),
        in_specs=[pl.BlockSpec((GLA_R, GLA_DK), lambda b, s: (rows(b, s), OFF_Q // GLA_DK)),
                  pl.BlockSpec((GLA_R, GLA_DK), lambda b, s: (rows(b, s), OFF_K // GLA_DK)),
                  pl.BlockSpec((GLA_R, GLA_DV), lambda b, s: (rows(b, s), OFF_V // GLA_DV)),
                  pl.BlockSpec((GLA_R, GLA_DV), lambda b, s: (rows(b, s), OFF_G // GLA_DV)),
                  pl.BlockSpec((GLA_R, GLA_DK), lambda b, s: (rows(b, s), 0)),
                  pl.BlockSpec((1, GLA_HV), lambda b, s: (0, 0)),
                  pl.BlockSpec((None, GLA_HEADS, GLA_HK, GLA_HV), lambda b, s: (b, 0, 0, 0))],
        out_specs=[pl.BlockSpec((GLA_R, GLA_DV), lambda b, s: (rows(b, s), 0)),
                   pl.BlockSpec((None, GLA_HEADS, GLA_HK, GLA_HV), lambda b, s: (b, 0, 0, 0))],
        out_shape=[jax.ShapeDtypeStruct((t, GLA_DV), bf16),
                   jax.ShapeDtypeStruct((n_seq, GLA_HEADS, GLA_HK, GLA_HV), f32)],
        scratch_shapes=[pltpu.VMEM((GLA_HEADS, GLA_HK, GLA_HV), f32)],
        compiler_params=_cparams(("arbitrary", "arbitrary")),
        name="gla_scan",
    )(proj, proj, proj, proj, la, nw, s0)


GS_SEQ = 4
GS_ROWS = 128
GS_B = GS_ROWS // GS_SEQ


def _gla_step_kernel(q_ref, k_ref, v_ref, g_ref, la_ref, nw_ref, s0_ref, o_ref, sout_ref,
                     oacc_ref, qd_ref, kdt_ref, lat_ref):
    j = pl.program_id(1)
    r = GS_ROWS
    row = lax.broadcasted_iota(jnp.int32, (r, r), 0)
    col = lax.broadcasted_iota(jnp.int32, (r, r), 1)

    @pl.when(j == 0)
    def _():
        same_seq = (row // GS_SEQ) == (col // GS_SEQ)
        causal = same_seq & (col <= row)
        la = la_ref[...]
        b = _dot_exact(causal.astype(f32), la)
        b_mid = _dot_exact((same_seq & (col % GS_SEQ <= GS_SEQ // 2)).astype(f32), la)
        b_last = _dot_exact(same_seq.astype(f32), la)
        q = q_ref[...].astype(f32) * (GLA_HK ** -0.5)
        k = k_ref[...].astype(f32)
        qs = (q * jnp.exp(b - b_mid)).astype(bf16)
        ks = (k * jnp.exp(b_mid - b)).astype(bf16)
        qd_ref[...] = (q * jnp.exp(b)).astype(bf16)
        kdt_ref[...] = (k * jnp.exp(b_last - b)).T.astype(bf16)
        lat_ref[...] = la.T
        for h in range(GLA_HEADS):
            hk = slice(h * GLA_HK, (h + 1) * GLA_HK)
            hv = slice(h * GLA_HV, (h + 1) * GLA_HV)
            scores = _dot_nt(qs[:, hk], ks[:, hk])
            oacc_ref[:, hv] = _dot(jnp.where(causal, scores, 0.0).astype(bf16), v_ref[:, hv])

    in_seq_lane = (col // GS_SEQ) == j
    in_seq_row = (row[:, :GLA_HK] // GS_SEQ) == j
    for h in range(GLA_HEADS):
        hk = slice(h * GLA_HK, (h + 1) * GLA_HK)
        hv = slice(h * GLA_HV, (h + 1) * GLA_HV)
        s0 = s0_ref[h]
        kd_t = kdt_ref[hk, :]
        ds = _dot(jnp.where(in_seq_lane, kd_t, jnp.zeros_like(kd_t)), v_ref[:, hv])
        decay = jnp.exp(jnp.sum(jnp.where(in_seq_lane, lat_ref[hk, :], 0.0), axis=1, keepdims=True))
        sout_ref[h] = decay * s0 + ds
        qd = qd_ref[:, hk]
        oacc_ref[:, hv] += _dot(jnp.where(in_seq_row, qd, jnp.zeros_like(qd)), s0.astype(bf16))

    @pl.when(j == pl.num_programs(1) - 1)
    def _():
        nw = nw_ref[...]
        for h in range(GLA_HEADS):
            hv = slice(h * GLA_HV, (h + 1) * GLA_HV)
            o_ref[:, hv] = _gla_norm_gate(oacc_ref[:, hv], nw, g_ref[:, hv]).astype(bf16)


def _gla_step(proj, la, nw, s0):
    n_seq = s0.shape[0]
    t = n_seq * GS_SEQ
    groups = t // GS_ROWS
    return pl.pallas_call(
        _gla_step_kernel,
        grid=(groups, GS_B),
        in_specs=[pl.BlockSpec((GS_ROWS, GLA_DK), lambda g, j: (g, OFF_Q // GLA_DK)),
                  pl.BlockSpec((GS_ROWS, GLA_DK), lambda g, j: (g, OFF_K // GLA_DK)),
                  pl.BlockSpec((GS_ROWS, GLA_DV), lambda g, j: (g, OFF_V // GLA_DV)),
                  pl.BlockSpec((GS_ROWS, GLA_DV), lambda g, j: (g, OFF_G // GLA_DV)),
                  pl.BlockSpec((GS_ROWS, GLA_DK), lambda g, j: (g, 0)),
                  pl.BlockSpec((1, GLA_HV), lambda g, j: (0, 0)),
                  pl.BlockSpec((None, GLA_HEADS, GLA_HK, GLA_HV), lambda g, j: (g * GS_B + j, 0, 0, 0))],
        out_specs=[pl.BlockSpec((GS_ROWS, GLA_DV), lambda g, j: (g, 0)),
                   pl.BlockSpec((None, GLA_HEADS, GLA_HK, GLA_HV), lambda g, j: (g * GS_B + j, 0, 0, 0))],
        out_shape=[jax.ShapeDtypeStruct((t, GLA_DV), bf16),
                   jax.ShapeDtypeStruct((n_seq, GLA_HEADS, GLA_HK, GLA_HV), f32)],
        scratch_shapes=[pltpu.VMEM((GS_ROWS, GLA_DV), f32),
                        pltpu.VMEM((GS_ROWS, GLA_DK), bf16),
                        pltpu.VMEM((GLA_DK, GS_ROWS), bf16),
                        pltpu.VMEM((GLA_DK, GS_ROWS), f32)],
        compiler_params=_cparams(("arbitrary", "arbitrary")),
        name="gla_step",
    )(proj, proj, proj, proj, la, nw, s0)


POST_TM = 256
HALO = 16


def _post_kernel(long_seq, tiles_per_seq, *refs):
    if long_seq:
        (x_ref, oa_ref, cb_ref, cc_ref, ch_ref, ga_ref, gb_ref, hcc_ref, hch_ref, cbuf_ref,
         wc_ref, g1_ref, sc_ref, sh_ref, nw_ref, wo_ref, wrh_ref, wrl_ref, br_ref, _hn_alias,
         h_ref, hn_ref, gate_ref, eid_ref, u_ref) = refs
    else:
        (x_ref, oa_ref, cb_ref, cc_ref, ch_ref, ga_ref, gb_ref, p0_ref, p1_ref,
         wc_ref, g1_ref, sc_ref, sh_ref, nw_ref, wo_ref, wrh_ref, wrl_ref, br_ref, _hn_alias,
         h_ref, hn_ref, gate_ref, eid_ref, u_ref) = refs
    tm = x_ref.shape[0]
    u = cc_ref[...].astype(f32) * ch_ref[...].astype(f32)
    row = lax.broadcasted_iota(jnp.int32, (tm, CONV_CH), 0)
    if long_seq:
        pos = row
        first = (pl.program_id(0) % tiles_per_seq) == 0
        halo = hcc_ref[HALO - 2:HALO, :].astype(f32) * hch_ref[HALO - 2:HALO, :].astype(f32)
        cbuf = cbuf_ref[...]
        p0 = jnp.where(first, cbuf[0:1], halo[0:1])
        p1 = jnp.where(first, cbuf[1:2], halo[1:2])
        u_ref[...] = u[tm - 8:tm]
    else:
        pos = row % GS_SEQ
        p0 = p0_ref[...]
        p1 = p1_ref[...]
        u_ref[...] = u
    u1 = jnp.where(pos == 0, p1, pltpu.roll(u, 1, 0))
    u2 = jnp.where(pos == 0, p0, jnp.where(pos == 1, p1, pltpu.roll(u, 2, 0)))
    wc = wc_ref[...]
    conv = wc[0:1] * u2 + wc[1:2] * u1 + wc[2:3] * u
    ob = (cb_ref[...].astype(f32) * conv).astype(bf16)

    ya = _dot(oa_ref[...], wo_ref[0:GLA_DV, :])
    yb = _dot(ob, wo_ref[GLA_DV:GLA_DV + CONV_CH, :])
    y = jax.nn.sigmoid(ga_ref[...].astype(f32)) * ya + jax.nn.sigmoid(gb_ref[...].astype(f32)) * yb
    h = x_ref[...] + g1_ref[...] * y
    h_ref[...] = h
    hn = _rms(h) * nw_ref[...] * (1.0 + sc_ref[...]) + sh_ref[...]
    hn_hi = hn.astype(bf16)
    hn_ref[...] = hn_hi
    hn_lo = (hn - hn_hi.astype(f32)).astype(bf16)
    lg = (_dot(hn_hi, wrh_ref[...]) + _dot(hn_hi, wrl_ref[...]) + _dot(hn_lo, wrh_ref[...])
          + br_ref[...])

    lane = lax.broadcasted_iota(jnp.int32, (tm, LANES), 1)
    lane_f = lane.astype(f32)
    vals, idxs = [], []
    for _ in range(TOP_K):
        m = jnp.max(lg, axis=-1, keepdims=True)
        idx = jnp.min(jnp.where(lg == m, lane_f, float(LANES)), axis=-1, keepdims=True)
        vals.append(m)
        idxs.append(idx)
        lg = jnp.where(lane_f == idx, -jnp.inf, lg)
    exps = [jnp.exp(v - vals[0]) for v in vals]
    den = exps[0] + exps[1] + exps[2] + exps[3]
    gate = jnp.zeros((tm, LANES), f32)
    eid = jnp.zeros((tm, LANES), f32)
    for j in range(TOP_K):
        gate = jnp.where(lane == j, exps[j] / den, gate)
        eid = jnp.where(lane == j, idxs[j], eid)
    gate_ref[...] = gate
    eid_ref[...] = eid.astype(jnp.int32)


def _post(x2, oa, proj, prev, wc, g1, sc2, sh2, nw, wo, wrh, wrl, br, hn_buf, row0, long_seq, seq_len):
    t = x2.shape[0]
    tm = POST_TM
    n_tiles = t // tm
    blk0 = row0 // tm
    tiles_per_seq = seq_len // tm if long_seq else 1

    def colblk(off, width):
        return pl.BlockSpec((tm, width), lambda m: (m, off // width))

    const = lambda shape: pl.BlockSpec(shape, lambda m: tuple(0 for _ in shape))
    if long_seq:
        mod_spec = pl.BlockSpec((None, 1, D_MODEL), lambda m: (m // tiles_per_seq, 0, 0))
        halo_rows = lambda m: jnp.maximum(m * (tm // HALO) - 1, 0)
        prev_specs = [pl.BlockSpec((HALO, CONV_CH), lambda m: (halo_rows(m), OFF_CC // CONV_CH)),
                      pl.BlockSpec((HALO, CONV_CH), lambda m: (halo_rows(m), OFF_CH // CONV_CH)),
                      pl.BlockSpec((None, CONV_K - 1, CONV_CH), lambda m: (m // tiles_per_seq, 0, 0))]
        prev_args = [proj, proj, prev]
        u_spec = pl.BlockSpec((None, 8, CONV_CH), lambda m: (m, 0, 0))
        u_shape = jax.ShapeDtypeStruct((n_tiles, 8, CONV_CH), f32)
    else:
        mod_spec = pl.BlockSpec((tm, D_MODEL), lambda m: (m, 0))
        prev_specs = [pl.BlockSpec((tm, CONV_CH), lambda m: (m, 0))] * 2
        prev_args = list(prev)
        u_spec = pl.BlockSpec((tm, CONV_CH), lambda m: (m, 0))
        u_shape = jax.ShapeDtypeStruct((t, CONV_CH), f32)
    n_in = 7 + len(prev_specs) + 10
    return pl.pallas_call(
        functools.partial(_post_kernel, long_seq, tiles_per_seq),
        grid=(n_tiles,),
        in_specs=[pl.BlockSpec((tm, D_MODEL), lambda m: (m, 0)),
                  pl.BlockSpec((tm, GLA_DV), lambda m: (m, 0)),
                  colblk(OFF_CB, CONV_CH), colblk(OFF_CC, CONV_CH), colblk(OFF_CH, CONV_CH),
                  colblk(OFF_GA, D_MODEL), colblk(OFF_GB, D_MODEL)]
                 + prev_specs
                 + [const((CONV_K, CONV_CH)), mod_spec, mod_spec, mod_spec, const((1, D_MODEL)),
                    const((GLA_DV + CONV_CH, D_MODEL)), const((D_MODEL, LANES)), const((D_MODEL, LANES)),
                    const((1, LANES)), pl.BlockSpec(memory_space=pl.ANY)],
        out_specs=[pl.BlockSpec((tm, D_MODEL), lambda m: (m, 0)),
                   pl.BlockSpec((tm, D_MODEL), lambda m: (blk0 + m, 0)),
                   pl.BlockSpec((tm, LANES), lambda m: (m, 0)),
                   pl.BlockSpec((tm, LANES), lambda m: (m, 0)),
                   u_spec],
        out_shape=[jax.ShapeDtypeStruct((t, D_MODEL), f32),
                   jax.ShapeDtypeStruct(hn_buf.shape, bf16),
                   jax.ShapeDtypeStruct((t, LANES), f32),
                   jax.ShapeDtypeStruct((t, LANES), jnp.int32),
                   u_shape],
        input_output_aliases={n_in - 1: 1},
        compiler_params=_cparams(("arbitrary",)),
        name="post_mix",
    )(x2, oa, proj, proj, proj, proj, proj, *prev_args, wc, g1, sc2, sh2, nw, wo, wrh, wrl, br, hn_buf)


MOE_RB = 256
MOE_TM = 1024
MOE_TF = 256
MOE_KB = MOE_TM // MOE_RB
MOE_NF = D_FF // MOE_TF


def _moe_kernel(ie_ref, ib_ref, in_ref, iv_ref, iz_ref,
                x_hbm, wg_ref, wl_ref, wd_ref, bg_ref, bl_ref, bd_ref, out_hbm,
                xbuf, acc, obuf, sem_in, sem_out):
    i = pl.program_id(0)
    f = pl.program_id(1)
    nb = in_ref[i]
    blk0 = ib_ref[i]
    nz = iz_ref[i]

    def x_copy(jb):
        return pltpu.make_async_copy(x_hbm.at[pl.ds((blk0 + jb) * MOE_RB, MOE_RB)],
                                     xbuf.at[pl.ds(jb * MOE_RB, MOE_RB)], sem_in)

    def out_copy(jb):
        return pltpu.make_async_copy(obuf.at[pl.ds(jb * MOE_RB, MOE_RB)],
                                     out_hbm.at[pl.ds((blk0 + jb) * MOE_RB, MOE_RB)], sem_out)

    @pl.when(f == 0)
    def _():
        def start(jb, carry):
            x_copy(jb).start()
            return carry

        def wait(jb, carry):
            x_copy(jb).wait()
            return carry

        lax.fori_loop(0, nb, start, 0)
        lax.fori_loop(0, nb, wait, 0)

        def fill(jb, carry):
            obuf[pl.ds(pl.multiple_of(jb * MOE_RB, MOE_RB), MOE_RB), :] = jnp.zeros((MOE_RB, D_MODEL), bf16)
            out_copy(jb).start()
            return carry

        def fill_wait(jb, carry):
            out_copy(jb).wait()
            return carry

        lax.fori_loop(0, nz, fill, 0)
        lax.fori_loop(0, nz, fill_wait, 0)

    wg = wg_ref[...].astype(bf16)
    wl = wl_ref[...].astype(bf16)
    wd = wd_ref[...].astype(bf16)
    bg = bg_ref[...]
    bl = bl_ref[...]

    def block(jb, carry):
        rows = pl.ds(pl.multiple_of(jb * MOE_RB, MOE_RB), MOE_RB)
        xb = xbuf[rows, :]
        hg = jnp.minimum(_dot(xb, wg) + bg, SWIGLU_LIMIT)
        hl = jnp.clip(_dot(xb, wl) + bl, -SWIGLU_LIMIT, SWIGLU_LIMIT)
        act = (hl + 1.0) * hg * jax.nn.sigmoid(SWIGLU_ALPHA * hg)
        part = _dot(act.astype(bf16), wd)

        @pl.when(f == 0)
        def _():
            acc[rows, :] = part + bd_ref[...]

        @pl.when(f > 0)
        def _():
            acc[rows, :] += part

        @pl.when(f == MOE_NF - 1)
        def _():
            obuf[rows, :] = acc[rows, :].astype(bf16)
            out_copy(jb).start()

        return carry

    lax.fori_loop(0, nb, block, 0)

    @pl.when(f == MOE_NF - 1)
    def _():
        def wait(jb, carry):
            out_copy(jb).wait()
            return carry

        lax.fori_loop(0, nb, wait, 0)


def _moe(xs, items, w_up, b_up, w_down, b_down):
    rows = xs.shape[0]
    n_items = items[0].shape[0]

    def f_eff(i, f, tabs):
        valid = tabs[3][i]
        return f * valid + (MOE_NF - 1) * (1 - valid)

    def expert(i, tabs):
        return tabs[0][i]

    grid_spec = pltpu.PrefetchScalarGridSpec(
        num_scalar_prefetch=len(items),
        grid=(n_items, MOE_NF),
        in_specs=[pl.BlockSpec(memory_space=pl.ANY),
                  pl.BlockSpec((None, D_MODEL, MOE_TF), lambda i, f, *t: (expert(i, t), 0, f_eff(i, f, t))),
                  pl.BlockSpec((None, D_MODEL, MOE_TF), lambda i, f, *t: (expert(i, t), 0, MOE_NF + f_eff(i, f, t))),
                  pl.BlockSpec((None, MOE_TF, D_MODEL), lambda i, f, *t: (expert(i, t), f_eff(i, f, t), 0)),
                  pl.BlockSpec((None, 1, MOE_TF), lambda i, f, *t: (expert(i, t), 0, f_eff(i, f, t))),
                  pl.BlockSpec((None, 1, MOE_TF), lambda i, f, *t: (expert(i, t), 0, MOE_NF + f_eff(i, f, t))),
                  pl.BlockSpec((None, 1, D_MODEL), lambda i, f, *t: (expert(i, t), 0, 0))],
        out_specs=pl.BlockSpec(memory_space=pl.ANY),
        scratch_shapes=[pltpu.VMEM((MOE_TM, D_MODEL), bf16),
                        pltpu.VMEM((MOE_TM, D_MODEL), f32),
                        pltpu.VMEM((MOE_TM, D_MODEL), bf16),
                        pltpu.SemaphoreType.DMA(()),
                        pltpu.SemaphoreType.DMA(())],
    )
    return pl.pallas_call(
        _moe_kernel,
        grid_spec=grid_spec,
        out_shape=jax.ShapeDtypeStruct((rows, D_MODEL), bf16),
        compiler_params=_cparams(("arbitrary", "arbitrary")),
        name="moe_experts",
    )(*items,
      xs, w_up, w_up, w_down,
      b_up.reshape(N_EXPERTS, 1, 2 * D_FF), b_up.reshape(N_EXPERTS, 1, 2 * D_FF),
      b_down.reshape(N_EXPERTS, 1, D_MODEL))


FIN_TM = 256


def _final_kernel(h_ref, og_ref, gate_ref, g2_ref, fw_ref, y_ref):
    gate = gate_ref[...]
    ff = og_ref[0].astype(f32) * gate[:, 0:1]
    for j in range(1, TOP_K):
        ff = ff + og_ref[j].astype(f32) * gate[:, j:j + 1]
    h = h_ref[...] + g2_ref[...] * ff
    y_ref[...] = _rms(h) * fw_ref[...]


def _final(h, og, gate, g2, fw, row0, per_row_mod, seq_len):
    t = h.shape[0]
    tm = FIN_TM
    blk0 = row0 // tm
    if per_row_mod:
        mod_spec = pl.BlockSpec((tm, D_MODEL), lambda m: (m, 0))
    else:
        tiles_per_seq = seq_len // tm
        mod_spec = pl.BlockSpec((None, 1, D_MODEL), lambda m: (m // tiles_per_seq, 0, 0))
    return pl.pallas_call(
        _final_kernel,
        grid=(t // tm,),
        in_specs=[pl.BlockSpec((tm, D_MODEL), lambda m: (m, 0)),
                  pl.BlockSpec((TOP_K, tm, D_MODEL), lambda m: (0, blk0 + m, 0)),
                  pl.BlockSpec((tm, LANES), lambda m: (blk0 + m, 0)),
                  mod_spec,
                  pl.BlockSpec((1, D_MODEL), lambda m: (0, 0))],
        out_specs=pl.BlockSpec((tm, D_MODEL), lambda m: (m, 0)),
        out_shape=jax.ShapeDtypeStruct((t, D_MODEL), f32),
        compiler_params=_cparams(("arbitrary",)),
        name="final_norm",
    )(h, og, gate, g2, fw)


def _count_le(bounds, x):
    return jnp.sum((bounds[None, :] <= x[:, None]).astype(jnp.int32), axis=1)


def _routing(top_e):
    t = top_e.shape[0]
    n_pairs = t * TOP_K
    e_flat = top_e.reshape(-1)
    onehot = (e_flat[:, None] == jnp.arange(N_EXPERTS, dtype=e_flat.dtype)[None, :]).astype(jnp.int32)
    csum = jnp.cumsum(onehot, axis=0)
    rank = jnp.sum(onehot * csum, axis=1) - 1
    counts = csum[-1]
    nblk = (counts + MOE_RB - 1) // MOE_RB
    blk_end = jnp.cumsum(nblk)
    blk_start = blk_end - nblk
    dest = blk_start[e_flat] * MOE_RB + rank
    n_rows = (n_pairs // MOE_RB + N_EXPERTS) * MOE_RB
    order = jnp.argsort(e_flat, stable=True).astype(jnp.int32)
    raw_start = jnp.cumsum(counts) - counts
    r = jnp.arange(n_rows, dtype=jnp.int32)
    e_row = jnp.minimum(_count_le(blk_end * MOE_RB, r), N_EXPERTS - 1)
    off = r - blk_start[e_row] * MOE_RB
    src_pair = order[jnp.clip(raw_start[e_row] + off, 0, n_pairs - 1)]
    src_tok = jnp.where(off < counts[e_row], src_pair // TOP_K, r % t).astype(jnp.int32)

    n_items = N_EXPERTS + n_rows // MOE_TM
    nit = (nblk + MOE_KB - 1) // MOE_KB
    it_end = jnp.cumsum(nit)
    it_start = it_end - nit
    idx = jnp.arange(n_items, dtype=jnp.int32)
    valid = idx < it_end[-1]
    last_valid = jnp.maximum(it_end[-1] - 1, 0)
    idx_c = jnp.minimum(idx, last_valid)
    e_of = jnp.minimum(_count_le(it_end, idx_c), N_EXPERTS - 1).astype(jnp.int32)
    k = idx_c - it_start[e_of]
    item_nb = jnp.where(valid, jnp.clip(nblk[e_of] - k * MOE_KB, 0, MOE_KB), 0).astype(jnp.int32)
    spare = idx - it_end[-1]
    tail_blk = blk_end[-1] + spare * MOE_KB
    item_nz = jnp.where(valid, 0, jnp.clip(n_rows // MOE_RB - tail_blk, 0, MOE_KB)).astype(jnp.int32)
    item_blk = jnp.where(valid, blk_start[e_of] + k * MOE_KB, tail_blk).astype(jnp.int32)
    items = (e_of, item_blk, item_nb, valid.astype(jnp.int32), item_nz)
    return dest.reshape(t, TOP_K), src_tok, items


def kernel(x_prompt, x_sample, state_gla, state_conv, c_prompt, c_sample, w_ada, b_ada, norm1_w,
           w_in, w_gk_up, b_gk, gla_norm_w, w_conv, w_out, norm2_w, w_router, b_router, w_up, b_up,
           w_down, b_down, final_norm_w):
    n_p, seq_p, _ = x_prompt.shape
    n_s, seq_s, _ = x_sample.shape
    t_p, t_s = n_p * seq_p, n_s * seq_s
    assert seq_s == GS_SEQ and w_ada.shape[0] == 1

    c_all = jnp.concatenate([c_prompt, c_sample], axis=0)
    pad = (-c_all.shape[0]) % 16
    c_all = jnp.pad(c_all, ((0, pad), (0, 0)))
    mod = _ada(c_all, w_ada[0], b_ada[0])
    mod_p = [mod[:n_p, i * D_MODEL:(i + 1) * D_MODEL].reshape(n_p, 1, D_MODEL) for i in range(N_MOD)]
    mod_s = [jnp.repeat(mod[n_p:n_p + n_s, i * D_MODEL:(i + 1) * D_MODEL], seq_s, axis=0) for i in range(N_MOD)]

    w_in0 = w_in[0]
    a0 = OFF_G
    w_main = jnp.concatenate([w_in0[:, :a0], w_in0[:, a0 + GLA_LOW_RANK:]], axis=1).astype(bf16)
    w_alow = jnp.pad(w_in0[:, a0:a0 + GLA_LOW_RANK], ((0, 0), (0, LANES - GLA_LOW_RANK))).astype(bf16)
    w_gk = jnp.pad(w_gk_up[0], ((0, LANES - GLA_LOW_RANK), (0, 0))).astype(bf16)
    bgk = b_gk[0].reshape(1, GLA_DK)
    wo = w_out[0].astype(bf16)
    wr = jnp.pad(w_router[0], ((0, 0), (0, LANES - N_EXPERTS)))
    wr_hi = wr.astype(bf16)
    wr_lo = (wr - wr_hi.astype(f32)).astype(bf16)
    br = jnp.pad(b_router[0], (0, LANES - N_EXPERTS), constant_values=ROUTER_PAD).reshape(1, LANES)
    n1w = norm1_w[0].reshape(1, D_MODEL)
    n2w = norm2_w[0].reshape(1, D_MODEL)
    gnw = gla_norm_w[0].reshape(1, GLA_HV)
    fw = final_norm_w.reshape(1, D_MODEL)
    wc = w_conv[0]

    xp = x_prompt.reshape(t_p, D_MODEL)
    xs_ = x_sample.reshape(t_s, D_MODEL)

    proj_p, la_p = _in_proj(xp, mod_p[1], mod_p[0], n1w, w_main, w_alow, w_gk, bgk, 1024, False)
    proj_s, la_s = _in_proj(xs_, mod_s[1], mod_s[0], n1w, w_main, w_alow, w_gk, bgk, t_s, True)

    gla0 = jnp.zeros((n_p, GLA_HEADS, GLA_HK, GLA_HV), f32)
    conv0 = jnp.zeros((n_p, CONV_K - 1, CONV_CH), f32)
    oa_p, gla_p = _gla_scan(proj_p, la_p, gnw, gla0, n_p, seq_p)
    oa_s, gla_s = _gla_step(proj_s, la_s, gnw, state_gla[0])

    hn = jnp.zeros((2 * (t_p + t_s), D_MODEL), bf16)
    h_p, hn, gate_p, eid_p, ut_p = _post(xp, oa_p, proj_p, conv0, wc, mod_p[2], mod_p[4], mod_p[3], n2w, wo,
                                         wr_hi, wr_lo, br, hn, 0, True, seq_p)
    prev_s = (jnp.repeat(state_conv[0][:, 0], seq_s, axis=0), jnp.repeat(state_conv[0][:, 1], seq_s, axis=0))
    h_s, hn, gate_s, eid_s, u_s = _post(xs_, oa_s, proj_s, prev_s, wc, mod_s[2], mod_s[4], mod_s[3], n2w, wo,
                                        wr_hi, wr_lo, br, hn, t_p, False, seq_s)
    conv_p = ut_p.reshape(n_p, seq_p // POST_TM, 8, CONV_CH)[:, -1, 8 - (CONV_K - 1):]
    conv_s = u_s.reshape(n_s, seq_s, CONV_CH)[:, seq_s - (CONV_K - 1):]

    gate = jnp.concatenate([gate_p, gate_s], axis=0)
    top_e = jnp.concatenate([eid_p, eid_s], axis=0)[:, :TOP_K]
    dest, src_tok, items = _routing(top_e)
    x_rows = hn[src_tok]
    out_rows = _moe(x_rows, items, w_up[0], b_up[0], w_down[0], b_down[0])
    og = out_rows[dest.T.reshape(-1)].reshape(TOP_K, t_p + t_s, D_MODEL)

    y_p = _final(h_p, og, gate, mod_p[5], fw, 0, False, seq_p)
    y_s = _final(h_s, og, gate, mod_s[5], fw, t_p, True, seq_s)

    return (y_p.reshape(n_p, seq_p, D_MODEL), y_s.reshape(n_s, seq_s, D_MODEL),
            gla_p[None], conv_p[None], gla_s[None], conv_s[None])
```

```python
import functools

import jax
import jax.numpy as jnp
from jax import lax
from jax.experimental import pallas as pl
from jax.experimental.pallas import tpu as pltpu

f32 = jnp.float32
bf16 = jnp.bfloat16

D_MODEL = 2048
N_MOD = 6
GLA_HEADS = 4
GLA_DK = 512
GLA_DV = 1024
GLA_HK = 128
GLA_HV = 256
GLA_LOW_RANK = 16
GLA_TAU = 16.0
GLA_CHUNK = 64
CONV_CH = 1024
CONV_K = 3
N_EXPERTS = 32
TOP_K = 4
D_FF = 2048
SWIGLU_LIMIT = 7.0
SWIGLU_ALPHA = 1.702
EPS = 1e-6

LANES = 128
D_MAIN = 10240
VMEM_LIMIT = 56 * 1024 * 1024

OFF_Q, OFF_K, OFF_V, OFF_G = 0, 512, 1024, 2048
OFF_CB, OFF_CC, OFF_CH, OFF_GA, OFF_GB = 3072, 4096, 5120, 6144, 8192

ROUTER_PAD = -1e30

HIGHEST = lax.Precision.HIGHEST


def _cparams(sem):
    return pltpu.CompilerParams(dimension_semantics=sem, vmem_limit_bytes=VMEM_LIMIT)


def _dot(a, b):
    return jnp.dot(a, b, preferred_element_type=f32)


def _dot_nt(a, b):
    return lax.dot_general(a, b, (((1,), (1,)), ((), ())), preferred_element_type=f32)


def _dot_exact(a, b):
    return jnp.dot(a, b, precision=HIGHEST, preferred_element_type=f32)


def _rms(x):
    return x * lax.rsqrt(jnp.mean(x * x, axis=-1, keepdims=True) + EPS)


ADA_TN = 1024


def _ada_kernel(c_ref, w_ref, b_ref, o_ref):
    c = c_ref[...]
    s = (c * jax.nn.sigmoid(c)).astype(bf16)
    o_ref[...] = _dot(s, w_ref[...].astype(bf16)) + b_ref[...]


def _ada(c, w, b):
    rows = c.shape[0]
    n = w.shape[1]
    return pl.pallas_call(
        _ada_kernel,
        grid=(n // ADA_TN,),
        in_specs=[pl.BlockSpec((rows, D_MODEL), lambda j: (0, 0)),
                  pl.BlockSpec((D_MODEL, ADA_TN), lambda j: (0, j)),
                  pl.BlockSpec((1, ADA_TN), lambda j: (0, j))],
        out_specs=pl.BlockSpec((rows, ADA_TN), lambda j: (0, j)),
        out_shape=jax.ShapeDtypeStruct((rows, n), f32),
        compiler_params=_cparams(("arbitrary",)),
        name="adaln",
    )(c, w, b.reshape(1, n))


IN_TN = 1024


def _in_kernel(x_ref, sc_ref, sh_ref, nw_ref, w_ref, wa_ref, wgk_ref, bgk_ref,
               proj_ref, la_ref, xn_ref):
    @pl.when(pl.program_id(1) == 0)
    def _():
        xn = _rms(x_ref[...]) * nw_ref[...] * (1.0 + sc_ref[...]) + sh_ref[...]
        xnb = xn.astype(bf16)
        xn_ref[...] = xnb
        a_low = _dot(xnb, wa_ref[...])
        z = _dot(a_low.astype(bf16), wgk_ref[...]) + bgk_ref[...]
        la_ref[...] = (jnp.minimum(z, 0.0) - jnp.log(1.0 + jnp.exp(-jnp.abs(z)))) * (1.0 / GLA_TAU)

    proj_ref[...] = _dot(xn_ref[...], w_ref[...]).astype(bf16)


def _in_proj(x2, sc, sh, nw, w_main, w_alow, w_gk, b_gk, tm, per_row_mod):
    t = x2.shape[0]
    if per_row_mod:
        mod_spec = pl.BlockSpec((tm, D_MODEL), lambda m, n: (m, 0))
    else:
        tiles_per_seq = (t // sc.shape[0]) // tm
        mod_spec = pl.BlockSpec((None, 1, D_MODEL), lambda m, n: (m // tiles_per_seq, 0, 0))
    return pl.pallas_call(
        _in_kernel,
        grid=(t // tm, D_MAIN // IN_TN),
        in_specs=[pl.BlockSpec((tm, D_MODEL), lambda m, n: (m, 0)),
                  mod_spec, mod_spec,
                  pl.BlockSpec((1, D_MODEL), lambda m, n: (0, 0)),
                  pl.BlockSpec((D_MODEL, IN_TN), lambda m, n: (0, n)),
                  pl.BlockSpec((D_MODEL, LANES), lambda m, n: (0, 0)),
                  pl.BlockSpec((LANES, GLA_DK), lambda m, n: (0, 0)),
                  pl.BlockSpec((1, GLA_DK), lambda m, n: (0, 0))],
        out_specs=[pl.BlockSpec((tm, IN_TN), lambda m, n: (m, n)),
                   pl.BlockSpec((tm, GLA_DK), lambda m, n: (m, 0))],
        out_shape=[jax.ShapeDtypeStruct((t, D_MAIN), bf16),
                   jax.ShapeDtypeStruct((t, GLA_DK), f32)],
        scratch_shapes=[pltpu.VMEM((tm, D_MODEL), bf16)],
        compiler_params=_cparams(("arbitrary", "arbitrary")),
        name="in_proj",
    )(x2, sc, sh, nw, w_main, w_alow, w_gk, b_gk)


GLA_R = 256


def _gla_norm_gate(o, nw, g):
    gf = g.astype(f32)
    return _rms(o) * nw * (gf * jax.nn.sigmoid(gf))


def _gla_scan_kernel(q_ref, k_ref, v_ref, g_ref, la_ref, nw_ref, s0_ref, o_ref, sout_ref, s_ref):
    step = pl.program_id(1)

    @pl.when(step == 0)
    def _():
        s_ref[...] = s0_ref[...]

    r, c = GLA_R, GLA_CHUNK
    n_chunks = r // c
    row = lax.broadcasted_iota(jnp.int32, (r, r), 0)
    col = lax.broadcasted_iota(jnp.int32, (r, r), 1)
    same_chunk = (row // c) == (col // c)
    causal = same_chunk & (col <= row)

    la = la_ref[...]
    b = _dot_exact(causal.astype(f32), la)
    b_mid = jnp.concatenate(
        [jnp.broadcast_to(b[i * c + c // 2:i * c + c // 2 + 1], (c, GLA_DK)) for i in range(n_chunks)], axis=0)
    b_last = jnp.concatenate(
        [jnp.broadcast_to(b[i * c + c - 1:i * c + c], (c, GLA_DK)) for i in range(n_chunks)], axis=0)
    q = q_ref[...].astype(f32) * (GLA_HK ** -0.5)
    k = k_ref[...].astype(f32)
    qs = (q * jnp.exp(b - b_mid)).astype(bf16)
    ks = (k * jnp.exp(b_mid - b)).astype(bf16)
    qd = (q * jnp.exp(b)).astype(bf16)
    kd_t = (k * jnp.exp(b_last - b)).T.astype(bf16)
    la_t = la.T
    lane = lax.broadcasted_iota(jnp.int32, (GLA_HK, r), 1)
    nw = nw_ref[...]

    for h in range(GLA_HEADS):
        hk = slice(h * GLA_HK, (h + 1) * GLA_HK)
        hv = slice(h * GLA_HV, (h + 1) * GLA_HV)
        v_h = v_ref[:, hv]
        scores = _dot_nt(qs[:, hk], ks[:, hk])
        o_intra = _dot(jnp.where(causal, scores, 0.0).astype(bf16), v_h)
        s = s_ref[h]
        for i in range(n_chunks):
            rows = slice(i * c, (i + 1) * c)
            in_chunk = (lane // c) == i
            o = o_intra[rows] + _dot(qd[rows, hk], s.astype(bf16))
            o_ref[rows, hv] = _gla_norm_gate(o, nw, g_ref[rows, hv]).astype(bf16)
            decay = jnp.exp(jnp.sum(jnp.where(in_chunk, la_t[hk], 0.0), axis=1, keepdims=True))
            s = decay * s + _dot(jnp.where(in_chunk, kd_t[hk], jnp.zeros_like(kd_t[hk])), v_h)
        s_ref[h] = s

    @pl.when(step == pl.num_programs(1) - 1)
    def _():
        sout_ref[...] = s_ref[...]


def _gla_scan(proj, la, nw, s0, n_seq, seq_len):
    steps = seq_len // GLA_R
    t = n_seq * seq_len

    def rows(b, s):
        return b * steps + s

    return pl.pallas_call(
        _gla_scan_kernel,
        grid=(n_seq, steps),
        in_specs=[pl.BlockSpec((GLA_R, GLA_DK), lambda b, s: (rows(b, s), OFF_Q // GLA_DK)),
                  pl.BlockSpec((GLA_R, GLA_DK), lambda b, s: (rows(b, s), OFF_K // GLA_DK)),
                  pl.BlockSpec((GLA_R, GLA_DV), lambda b, s: (rows(b, s), OFF_V // GLA_DV)),
                  pl.BlockSpec((GLA_R, GLA_DV), lambda b, s: (rows(b, s), OFF_G // GLA_DV)),
                  pl.BlockSpec((GLA_R, GLA_DK), lambda b, s: (rows(b, s), 0)),
                  pl.BlockSpec((1, GLA_HV), lambda b, s: (0, 0)),
                  pl.BlockSpec((None, GLA_HEADS, GLA_HK, GLA_HV), lambda b, s: (b, 0, 0, 0))],
        out_specs=[pl.BlockSpec((GLA_R, GLA_DV), lambda b, s: (rows(b, s), 0)),
                   pl.BlockSpec((None, GLA_HEADS, GLA_HK, GLA_HV), lambda b, s: (b, 0, 0, 0))],
        out_shape=[jax.ShapeDtypeStruct((t, GLA_DV), bf16),
                   jax.ShapeDtypeStruct((n_seq, GLA_HEADS, GLA_HK, GLA_HV), f32)],
        scratch_shapes=[pltpu.VMEM((GLA_HEADS, GLA_HK, GLA_HV), f32)],
        compiler_params=_cparams(("arbitrary", "arbitrary")),
        name="gla_scan",
    )(proj, proj, proj, proj, la, nw, s0)


GS_SEQ = 4
GS_ROWS = 128
GS_B = GS_ROWS // GS_SEQ


def _gla_step_kernel(q_ref, k_ref, v_ref, g_ref, la_ref, nw_ref, s0_ref, o_ref, sout_ref,
                     oacc_ref, qd_ref, kdt_ref, lat_ref):
    j = pl.program_id(1)
    r = GS_ROWS
    row = lax.broadcasted_iota(jnp.int32, (r, r), 0)
    col = lax.broadcasted_iota(jnp.int32, (r, r), 1)

    @pl.when(j == 0)
    def _():
        same_seq = (row // GS_SEQ) == (col // GS_SEQ)
        causal = same_seq & (col <= row)
        la = la_ref[...]
        b = _dot_exact(causal.astype(f32), la)
        b_mid = _dot_exact((same_seq & (col % GS_SEQ <= GS_SEQ // 2)).astype(f32), la)
        b_last = _dot_exact(same_seq.astype(f32), la)
        q = q_ref[...].astype(f32) * (GLA_HK ** -0.5)
        k = k_ref[...].astype(f32)
        qs = (q * jnp.exp(b - b_mid)).astype(bf16)
        ks = (k * jnp.exp(b_mid - b)).astype(bf16)
        qd_ref[...] = (q * jnp.exp(b)).astype(bf16)
        kdt_ref[...] = (k * jnp.exp(b_last - b)).T.astype(bf16)
        lat_ref[...] = la.T
        for h in range(GLA_HEADS):
            hk = slice(h * GLA_HK, (h + 1) * GLA_HK)
            hv = slice(h * GLA_HV, (h + 1) * GLA_HV)
            scores = _dot_nt(qs[:, hk], ks[:, hk])
            oacc_ref[:, hv] = _dot(jnp.where(causal, scores, 0.0).astype(bf16), v_ref[:, hv])

    in_seq_lane = (col // GS_SEQ) == j
    in_seq_row = (row[:, :GLA_HK] // GS_SEQ) == j
    for h in range(GLA_HEADS):
        hk = slice(h * GLA_HK, (h + 1) * GLA_HK)
        hv = slice(h * GLA_HV, (h + 1) * GLA_HV)
        s0 = s0_ref[h]
        kd_t = kdt_ref[hk, :]
        ds = _dot(jnp.where(in_seq_lane, kd_t, jnp.zeros_like(kd_t)), v_ref[:, hv])
        decay = jnp.exp(jnp.sum(jnp.where(in_seq_lane, lat_ref[hk, :], 0.0), axis=1, keepdims=True))
        sout_ref[h] = decay * s0 + ds
        qd = qd_ref[:, hk]
        oacc_ref[:, hv] += _dot(jnp.where(in_seq_row, qd, jnp.zeros_like(qd)), s0.astype(bf16))

    @pl.when(j == pl.num_programs(1) - 1)
    def _():
        nw = nw_ref[...]
        for h in range(GLA_HEADS):
            hv = slice(h * GLA_HV, (h + 1) * GLA_HV)
            o_ref[:, hv] = _gla_norm_gate(oacc_ref[:, hv], nw, g_ref[:, hv]).astype(bf16)


def _gla_step(proj, la, nw, s0):
    n_seq = s0.shape[0]
    t = n_seq * GS_SEQ
    groups = t // GS_ROWS
    return pl.pallas_call(
        _gla_step_kernel,
        grid=(groups, GS_B),
        in_specs=[pl.BlockSpec((GS_ROWS, GLA_DK), lambda g, j: (g, OFF_Q // GLA_DK)),
                  pl.BlockSpec((GS_ROWS, GLA_DK), lambda g, j: (g, OFF_K // GLA_DK)),
                  pl.BlockSpec((GS_ROWS, GLA_DV), lambda g, j: (g, OFF_V // GLA_DV)),
                  pl.BlockSpec((GS_ROWS, GLA_DV), lambda g, j: (g, OFF_G // GLA_DV)),
                  pl.BlockSpec((GS_ROWS, GLA_DK), lambda g, j: (g, 0)),
                  pl.BlockSpec((1, GLA_HV), lambda g, j: (0, 0)),
                  pl.BlockSpec((None, GLA_HEADS, GLA_HK, GLA_HV), lambda g, j: (g * GS_B + j, 0, 0, 0))],
        out_specs=[pl.BlockSpec((GS_ROWS, GLA_DV), lambda g, j: (g, 0)),
                   pl.BlockSpec((None, GLA_HEADS, GLA_HK, GLA_HV), lambda g, j: (g * GS_B + j, 0, 0, 0))],
        out_shape=[jax.ShapeDtypeStruct((t, GLA_DV), bf16),
                   jax.ShapeDtypeStruct((n_seq, GLA_HEADS, GLA_HK, GLA_HV), f32)],
        scratch_shapes=[pltpu.VMEM((GS_ROWS, GLA_DV), f32),
                        pltpu.VMEM((GS_ROWS, GLA_DK), bf16),
                        pltpu.VMEM((GLA_DK, GS_ROWS), bf16),
                        pltpu.VMEM((GLA_DK, GS_ROWS), f32)],
        compiler_params=_cparams(("arbitrary", "arbitrary")),
        name="gla_step",
    )(proj, proj, proj, proj, la, nw, s0)


POST_TM = 256
HALO = 16


def _post_kernel(long_seq, tiles_per_seq, *refs):
    if long_seq:
        (x_ref, oa_ref, cb_ref, cc_ref, ch_ref, ga_ref, gb_ref, hcc_ref, hch_ref, cbuf_ref,
         wc_ref, g1_ref, sc_ref, sh_ref, nw_ref, wo_ref, wrh_ref, wrl_ref, br_ref, _hn_alias,
         h_ref, hn_ref, gate_ref, eid_ref, u_ref) = refs
    else:
        (x_ref, oa_ref, cb_ref, cc_ref, ch_ref, ga_ref, gb_ref, p0_ref, p1_ref,
         wc_ref, g1_ref, sc_ref, sh_ref, nw_ref, wo_ref, wrh_ref, wrl_ref, br_ref, _hn_alias,
         h_ref, hn_ref, gate_ref, eid_ref, u_ref) = refs
    tm = x_ref.shape[0]
    u = cc_ref[...].astype(f32) * ch_ref[...].astype(f32)
    row = lax.broadcasted_iota(jnp.int32, (tm, CONV_CH), 0)
    if long_seq:
        pos = row
        first = (pl.program_id(0) % tiles_per_seq) == 0
        halo = hcc_ref[HALO - 2:HALO, :].astype(f32) * hch_ref[HALO - 2:HALO, :].astype(f32)
        cbuf = cbuf_ref[...]
        p0 = jnp.where(first, cbuf[0:1], halo[0:1])
        p1 = jnp.where(first, cbuf[1:2], halo[1:2])
        u_ref[...] = u[tm - 8:tm]
    else:
        pos = row % GS_SEQ
        p0 = p0_ref[...]
        p1 = p1_ref[...]
        u_ref[...] = u
    u1 = jnp.where(pos == 0, p1, pltpu.roll(u, 1, 0))
    u2 = jnp.where(pos == 0, p0, jnp.where(pos == 1, p1, pltpu.roll(u, 2, 0)))
    wc = wc_ref[...]
    conv = wc[0:1] * u2 + wc[1:2] * u1 + wc[2:3] * u
    ob = (cb_ref[...].astype(f32) * conv).astype(bf16)

    ya = _dot(oa_ref[...], wo_ref[0:GLA_DV, :])
    yb = _dot(ob, wo_ref[GLA_DV:GLA_DV + CONV_CH, :])
    y = jax.nn.sigmoid(ga_ref[...].astype(f32)) * ya + jax.nn.sigmoid(gb_ref[...].astype(f32)) * yb
    h = x_ref[...] + g1_ref[...] * y
    h_ref[...] = h
    hn = _rms(h) * nw_ref[...] * (1.0 + sc_ref[...]) + sh_ref[...]
    hn_hi = hn.astype(bf16)
    hn_ref[...] = hn_hi
    hn_lo = (hn - hn_hi.astype(f32)).astype(bf16)
    lg = (_dot(hn_hi, wrh_ref[...]) + _dot(hn_hi, wrl_ref[...]) + _dot(hn_lo, wrh_ref[...])
          + br_ref[...])

    lane = lax.broadcasted_iota(jnp.int32, (tm, LANES), 1)
    lane_f = lane.astype(f32)
    vals, idxs = [], []
    for _ in range(TOP_K):
        m = jnp.max(lg, axis=-1, keepdims=True)
        idx = jnp.min(jnp.where(lg == m, lane_f, float(LANES)), axis=-1, keepdims=True)
        vals.append(m)
        idxs.append(idx)
        lg = jnp.where(lane_f == idx, -jnp.inf, lg)
    exps = [jnp.exp(v - vals[0]) for v in vals]
    den = exps[0] + exps[1] + exps[2] + exps[3]
    gate = jnp.zeros((tm, LANES), f32)
    eid = jnp.zeros((tm, LANES), f32)
    for j in range(TOP_K):
        gate = jnp.where(lane == j, exps[j] / den, gate)
        eid = jnp.where(lane == j, idxs[j], eid)
    gate_ref[...] = gate
    eid_ref[...] = eid.astype(jnp.int32)


def _post(x2, oa, proj, prev, wc, g1, sc2, sh2, nw, wo, wrh, wrl, br, hn_buf, row0, long_seq, seq_len):
    t = x2.shape[0]
    tm = POST_TM
    n_tiles = t // tm
    blk0 = row0 // tm
    tiles_per_seq = seq_len // tm if long_seq else 1

    def colblk(off, width):
        return pl.BlockSpec((tm, width), lambda m: (m, off // width))

    const = lambda shape: pl.BlockSpec(shape, lambda m: tuple(0 for _ in shape))
    if long_seq:
        mod_spec = pl.BlockSpec((None, 1, D_MODEL), lambda m: (m // tiles_per_seq, 0, 0))
        halo_rows = lambda m: jnp.maximum(m * (tm // HALO) - 1, 0)
        prev_specs = [pl.BlockSpec((HALO, CONV_CH), lambda m: (halo_rows(m), OFF_CC // CONV_CH)),
                      pl.BlockSpec((HALO, CONV_CH), lambda m: (halo_rows(m), OFF_CH // CONV_CH)),
                      pl.BlockSpec((None, CONV_K - 1, CONV_CH), lambda m: (m // tiles_per_seq, 0, 0))]
        prev_args = [proj, proj, prev]
        u_spec = pl.BlockSpec((None, 8, CONV_CH), lambda m: (m, 0, 0))
        u_shape = jax.ShapeDtypeStruct((n_tiles, 8, CONV_CH), f32)
    else:
        mod_spec = pl.BlockSpec((tm, D_MODEL), lambda m: (m, 0))
        prev_specs = [pl.BlockSpec((tm, CONV_CH), lambda m: (m, 0))] * 2
        prev_args = list(prev)
        u_spec = pl.BlockSpec((tm, CONV_CH), lambda m: (m, 0))
        u_shape = jax.ShapeDtypeStruct((t, CONV_CH), f32)
    n_in = 7 + len(prev_specs) + 10
    return pl.pallas_call(
        functools.partial(_post_kernel, long_seq, tiles_per_seq),
        grid=(n_tiles,),
        in_specs=[pl.BlockSpec((tm, D_MODEL), lambda m: (m, 0)),
                  pl.BlockSpec((tm, GLA_DV), lambda m: (m, 0)),
                  colblk(OFF_CB, CONV_CH), colblk(OFF_CC, CONV_CH), colblk(OFF_CH, CONV_CH),
                  colblk(OFF_GA, D_MODEL), colblk(OFF_GB, D_MODEL)]
                 + prev_specs
                 + [const((CONV_K, CONV_CH)), mod_spec, mod_spec, mod_spec, const((1, D_MODEL)),
                    const((GLA_DV + CONV_CH, D_MODEL)), const((D_MODEL, LANES)), const((D_MODEL, LANES)),
                    const((1, LANES)), pl.BlockSpec(memory_space=pl.ANY)],
        out_specs=[pl.BlockSpec((tm, D_MODEL), lambda m: (m, 0)),
                   pl.BlockSpec((tm, D_MODEL), lambda m: (blk0 + m, 0)),
                   pl.BlockSpec((tm, LANES), lambda m: (m, 0)),
                   pl.BlockSpec((tm, LANES), lambda m: (m, 0)),
                   u_spec],
        out_shape=[jax.ShapeDtypeStruct((t, D_MODEL), f32),
                   jax.ShapeDtypeStruct(hn_buf.shape, bf16),
                   jax.ShapeDtypeStruct((t, LANES), f32),
                   jax.ShapeDtypeStruct((t, LANES), jnp.int32),
                   u_shape],
        input_output_aliases={n_in - 1: 1},
        compiler_params=_cparams(("arbitrary",)),
        name="post_mix",
    )(x2, oa, proj, proj, proj, proj, proj, *prev_args, wc, g1, sc2, sh2, nw, wo, wrh, wrl, br, hn_buf)


MOE_RB = 256
MOE_TM = 2048
MOE_TN = 512
MOE_KB = MOE_TM // MOE_RB
MOE_NT = D_FF // MOE_TN
MOE_BIG = 2 * MOE_RB


def _moe_kernel(ie_ref, ib_ref, in_ref, iv_ref, iz_ref,
                x_hbm, wg_ref, wl_ref, wd_ref, bg_ref, bl_ref, bd_ref, out_hbm,
                xbuf, hbuf, wup, wdn, sem_in, sem_out):
    i = pl.program_id(0)
    s = pl.program_id(1)
    nb = in_ref[i]
    blk0 = ib_ref[i]
    nz = iz_ref[i]

    def x_copy(jb, k):
        return pltpu.make_async_copy(
            x_hbm.at[pl.ds((blk0 + jb) * MOE_RB, MOE_RB), pl.ds(k * MOE_TN, MOE_TN)],
            xbuf.at[k, pl.ds(jb * MOE_RB, MOE_RB)], sem_in)

    def out_copy(jb, k):
        return pltpu.make_async_copy(
            xbuf.at[k, pl.ds(jb * MOE_RB, MOE_RB)],
            out_hbm.at[pl.ds((blk0 + jb) * MOE_RB, MOE_RB), pl.ds(k * MOE_TN, MOE_TN)], sem_out)

    def for_blocks(n, fn):
        def body(jb, carry):
            for k in range(MOE_NT):
                fn(jb, k)
            return carry

        lax.fori_loop(0, n, body, 0)

    def for_rows(fn):
        def body(j, carry):
            fn(pl.multiple_of(j * MOE_BIG, MOE_BIG), MOE_BIG)
            return carry

        lax.fori_loop(0, nb // 2, body, 0)

        @pl.when(nb % 2 == 1)
        def _():
            fn(pl.multiple_of((nb - 1) * MOE_RB, MOE_RB), MOE_RB)

    @pl.when(s == 0)
    def _():
        for_blocks(nb, lambda jb, k: x_copy(jb, k).start())
        for_blocks(nb, lambda jb, k: x_copy(jb, k).wait())

        def fill(jb, k):
            xbuf[k, pl.ds(pl.multiple_of(jb * MOE_RB, MOE_RB), MOE_RB), :] = jnp.zeros((MOE_RB, MOE_TN), bf16)
            out_copy(jb, k).start()

        for_blocks(nz, fill)
        for_blocks(nz, lambda jb, k: out_copy(jb, k).wait())

    @pl.when(s < MOE_NT)
    def _():
        wup[:, 0:MOE_TN] = wg_ref[...].astype(bf16)
        wup[:, MOE_TN:2 * MOE_TN] = wl_ref[...].astype(bf16)
        bg = bg_ref[...]
        bl = bl_ref[...]

        def up(r0, size):
            rows = pl.ds(r0, size)
            xb = jnp.concatenate([xbuf[k, rows, :] for k in range(MOE_NT)], axis=1)
            h = _dot(xb, wup[...])
            hg = jnp.minimum(h[:, 0:MOE_TN] + bg, SWIGLU_LIMIT)
            hl = jnp.clip(h[:, MOE_TN:2 * MOE_TN] + bl, -SWIGLU_LIMIT, SWIGLU_LIMIT)
            hbuf[s, rows, :] = ((hl + 1.0) * hg * jax.nn.sigmoid(SWIGLU_ALPHA * hg)).astype(bf16)

        for_rows(up)

    @pl.when(s >= MOE_NT)
    def _():
        wdn[...] = wd_ref[...].astype(bf16)
        bd = bd_ref[...]

        def down(r0, size):
            rows = pl.ds(r0, size)
            hb = jnp.concatenate([hbuf[k, rows, :] for k in range(MOE_NT)], axis=1)
            xbuf[s - MOE_NT, rows, :] = (_dot(hb, wdn[...]) + bd).astype(bf16)

        for_rows(down)

    @pl.when(s == 2 * MOE_NT - 1)
    def _():
        for_blocks(nb, lambda jb, k: out_copy(jb, k).start())
        for_blocks(nb, lambda jb, k: out_copy(jb, k).wait())


def _moe(xs, items, w_up, b_up, w_down, b_down):
    rows = xs.shape[0]
    n_items = items[0].shape[0]

    last = MOE_NT - 1

    def up_tile(i, s, tabs):
        valid = tabs[3][i]
        return jnp.minimum(s, last) * valid + last * (1 - valid)

    def down_tile(i, s, tabs):
        valid = tabs[3][i]
        return jnp.maximum(s - MOE_NT, 0) * valid + last * (1 - valid)

    def expert(i, tabs):
        return tabs[0][i]

    grid_spec = pltpu.PrefetchScalarGridSpec(
        num_scalar_prefetch=len(items),
        grid=(n_items, 2 * MOE_NT),
        in_specs=[pl.BlockSpec(memory_space=pl.ANY),
                  pl.BlockSpec((None, D_MODEL, MOE_TN), lambda i, s, *t: (expert(i, t), 0, up_tile(i, s, t))),
                  pl.BlockSpec((None, D_MODEL, MOE_TN), lambda i, s, *t: (expert(i, t), 0, MOE_NT + up_tile(i, s, t))),
                  pl.BlockSpec((None, D_FF, MOE_TN), lambda i, s, *t: (expert(i, t), 0, down_tile(i, s, t))),
                  pl.BlockSpec((None, 1, MOE_TN), lambda i, s, *t: (expert(i, t), 0, up_tile(i, s, t))),
                  pl.BlockSpec((None, 1, MOE_TN), lambda i, s, *t: (expert(i, t), 0, MOE_NT + up_tile(i, s, t))),
                  pl.BlockSpec((None, 1, MOE_TN), lambda i, s, *t: (expert(i, t), 0, down_tile(i, s, t)))],
        out_specs=pl.BlockSpec(memory_space=pl.ANY),
        scratch_shapes=[pltpu.VMEM((MOE_NT, MOE_TM, MOE_TN), bf16),
                        pltpu.VMEM((MOE_NT, MOE_TM, MOE_TN), bf16),
                        pltpu.VMEM((D_MODEL, 2 * MOE_TN), bf16),
                        pltpu.VMEM((D_FF, MOE_TN), bf16),
                        pltpu.SemaphoreType.DMA(()),
                        pltpu.SemaphoreType.DMA(())],
    )
    return pl.pallas_call(
        _moe_kernel,
        grid_spec=grid_spec,
        out_shape=jax.ShapeDtypeStruct((rows, D_MODEL), bf16),
        compiler_params=_cparams(("arbitrary", "arbitrary")),
        name="moe_experts",
    )(*items,
      xs, w_up, w_up, w_down,
      b_up.reshape(N_EXPERTS, 1, 2 * D_FF), b_up.reshape(N_EXPERTS, 1, 2 * D_FF),
      b_down.reshape(N_EXPERTS, 1, D_MODEL))


FIN_TM = 256


def _final_kernel(h_ref, og_ref, gate_ref, g2_ref, fw_ref, y_ref):
    gate = gate_ref[...]
    ff = og_ref[0].astype(f32) * gate[:, 0:1]
    for j in range(1, TOP_K):
        ff = ff + og_ref[j].astype(f32) * gate[:, j:j + 1]
    h = h_ref[...] + g2_ref[...] * ff
    y_ref[...] = _rms(h) * fw_ref[...]


def _final(h, og, gate, g2, fw, row0, per_row_mod, seq_len):
    t = h.shape[0]
    tm = FIN_TM
    blk0 = row0 // tm
    if per_row_mod:
        mod_spec = pl.BlockSpec((tm, D_MODEL), lambda m: (m, 0))
    else:
        tiles_per_seq = seq_len // tm
        mod_spec = pl.BlockSpec((None, 1, D_MODEL), lambda m: (m // tiles_per_seq, 0, 0))
    return pl.pallas_call(
        _final_kernel,
        grid=(t // tm,),
        in_specs=[pl.BlockSpec((tm, D_MODEL), lambda m: (m, 0)),
                  pl.BlockSpec((TOP_K, tm, D_MODEL), lambda m: (0, blk0 + m, 0)),
                  pl.BlockSpec((tm, LANES), lambda m: (blk0 + m, 0)),
                  mod_spec,
                  pl.BlockSpec((1, D_MODEL), lambda m: (0, 0))],
        out_specs=pl.BlockSpec((tm, D_MODEL), lambda m: (m, 0)),
        out_shape=jax.ShapeDtypeStruct((t, D_MODEL), f32),
        compiler_params=_cparams(("arbitrary",)),
        name="final_norm",
    )(h, og, gate, g2, fw)


def _count_le(bounds, x):
    return jnp.sum((bounds[None, :] <= x[:, None]).astype(jnp.int32), axis=1)


def _routing(top_e):
    t = top_e.shape[0]
    n_pairs = t * TOP_K
    e_flat = top_e.reshape(-1)
    onehot = (e_flat[:, None] == jnp.arange(N_EXPERTS, dtype=e_flat.dtype)[None, :]).astype(jnp.int32)
    csum = jnp.cumsum(onehot, axis=0)
    rank = jnp.sum(onehot * csum, axis=1) - 1
    counts = csum[-1]
    nblk = (counts + MOE_RB - 1) // MOE_RB
    blk_end = jnp.cumsum(nblk)
    blk_start = blk_end - nblk
    dest = blk_start[e_flat] * MOE_RB + rank
    n_rows = (n_pairs // MOE_RB + N_EXPERTS) * MOE_RB
    order = jnp.argsort(e_flat, stable=True).astype(jnp.int32)
    raw_start = jnp.cumsum(counts) - counts
    r = jnp.arange(n_rows, dtype=jnp.int32)
    e_row = jnp.minimum(_count_le(blk_end * MOE_RB, r), N_EXPERTS - 1)
    off = r - blk_start[e_row] * MOE_RB
    src_pair = order[jnp.clip(raw_start[e_row] + off, 0, n_pairs - 1)]
    src_tok = jnp.where(off < counts[e_row], src_pair // TOP_K, r % t).astype(jnp.int32)

    n_items = N_EXPERTS + n_rows // MOE_TM
    nit = (nblk + MOE_KB - 1) // MOE_KB
    it_end = jnp.cumsum(nit)
    it_start = it_end - nit
    idx = jnp.arange(n_items, dtype=jnp.int32)
    valid = idx < it_end[-1]
    last_valid = jnp.maximum(it_end[-1] - 1, 0)
    idx_c = jnp.minimum(idx, last_valid)
    e_of = jnp.minimum(_count_le(it_end, idx_c), N_EXPERTS - 1).astype(jnp.int32)
    k = idx_c - it_start[e_of]
    item_nb = jnp.where(valid, jnp.clip(nblk[e_of] - k * MOE_KB, 0, MOE_KB), 0).astype(jnp.int32)
    spare = idx - it_end[-1]
    tail_blk = blk_end[-1] + spare * MOE_KB
    item_nz = jnp.where(valid, 0, jnp.clip(n_rows // MOE_RB - tail_blk, 0, MOE_KB)).astype(jnp.int32)
    item_blk = jnp.where(valid, blk_start[e_of] + k * MOE_KB, tail_blk).astype(jnp.int32)
    items = (e_of, item_blk, item_nb, valid.astype(jnp.int32), item_nz)
    return dest.reshape(t, TOP_K), src_tok, items


def kernel(x_prompt, x_sample, state_gla, state_conv, c_prompt, c_sample, w_ada, b_ada, norm1_w,
           w_in, w_gk_up, b_gk, gla_norm_w, w_conv, w_out, norm2_w, w_router, b_router, w_up, b_up,
           w_down, b_down, final_norm_w):
    n_p, seq_p, _ = x_prompt.shape
    n_s, seq_s, _ = x_sample.shape
    t_p, t_s = n_p * seq_p, n_s * seq_s
    assert seq_s == GS_SEQ and w_ada.shape[0] == 1

    c_all = jnp.concatenate([c_prompt, c_sample], axis=0)
    pad = (-c_all.shape[0]) % 16
    c_all = jnp.pad(c_all, ((0, pad), (0, 0)))
    mod = _ada(c_all, w_ada[0], b_ada[0])
    mod_p = [mod[:n_p, i * D_MODEL:(i + 1) * D_MODEL].reshape(n_p, 1, D_MODEL) for i in range(N_MOD)]
    mod_s = [jnp.repeat(mod[n_p:n_p + n_s, i * D_MODEL:(i + 1) * D_MODEL], seq_s, axis=0) for i in range(N_MOD)]

    w_in0 = w_in[0]
    a0 = OFF_G
    w_main = jnp.concatenate([w_in0[:, :a0], w_in0[:, a0 + GLA_LOW_RANK:]], axis=1).astype(bf16)
    w_alow = jnp.pad(w_in0[:, a0:a0 + GLA_LOW_RANK], ((0, 0), (0, LANES - GLA_LOW_RANK))).astype(bf16)
    w_gk = jnp.pad(w_gk_up[0], ((0, LANES - GLA_LOW_RANK), (0, 0))).astype(bf16)
    bgk = b_gk[0].reshape(1, GLA_DK)
    wo = w_out[0].astype(bf16)
    wr = jnp.pad(w_router[0], ((0, 0), (0, LANES - N_EXPERTS)))
    wr_hi = wr.astype(bf16)
    wr_lo = (wr - wr_hi.astype(f32)).astype(bf16)
    br = jnp.pad(b_router[0], (0, LANES - N_EXPERTS), constant_values=ROUTER_PAD).reshape(1, LANES)
    n1w = norm1_w[0].reshape(1, D_MODEL)
    n2w = norm2_w[0].reshape(1, D_MODEL)
    gnw = gla_norm_w[0].reshape(1, GLA_HV)
    fw = final_norm_w.reshape(1, D_MODEL)
    wc = w_conv[0]

    xp = x_prompt.reshape(t_p, D_MODEL)
    xs_ = x_sample.reshape(t_s, D_MODEL)

    proj_p, la_p = _in_proj(xp, mod_p[1], mod_p[0], n1w, w_main, w_alow, w_gk, bgk, 1024, False)
    proj_s, la_s = _in_proj(xs_, mod_s[1], mod_s[0], n1w, w_main, w_alow, w_gk, bgk, t_s, True)

    gla0 = jnp.zeros((n_p, GLA_HEADS, GLA_HK, GLA_HV), f32)
    conv0 = jnp.zeros((n_p, CONV_K - 1, CONV_CH), f32)
    oa_p, gla_p = _gla_scan(proj_p, la_p, gnw, gla0, n_p, seq_p)
    oa_s, gla_s = _gla_step(proj_s, la_s, gnw, state_gla[0])

    hn = jnp.zeros((2 * (t_p + t_s), D_MODEL), bf16)
    h_p, hn, gate_p, eid_p, ut_p = _post(xp, oa_p, proj_p, conv0, wc, mod_p[2], mod_p[4], mod_p[3], n2w, wo,
                                         wr_hi, wr_lo, br, hn, 0, True, seq_p)
    prev_s = (jnp.repeat(state_conv[0][:, 0], seq_s, axis=0), jnp.repeat(state_conv[0][:, 1], seq_s, axis=0))
    h_s, hn, gate_s, eid_s, u_s = _post(xs_, oa_s, proj_s, prev_s, wc, mod_s[2], mod_s[4], mod_s[3], n2w, wo,
                                        wr_hi, wr_lo, br, hn, t_p, False, seq_s)
    conv_p = ut_p.reshape(n_p, seq_p // POST_TM, 8, CONV_CH)[:, -1, 8 - (CONV_K - 1):]
    conv_s = u_s.reshape(n_s, seq_s, CONV_CH)[:, seq_s - (CONV_K - 1):]

    gate = jnp.concatenate([gate_p, gate_s], axis=0)
    top_e = jnp.concatenate([eid_p, eid_s], axis=0)[:, :TOP_K]
    dest, src_tok, items = _routing(top_e)
    x_rows = hn[src_tok]
    out_rows = _moe(x_rows, items, w_up[0], b_up[0], w_down[0], b_down[0])
    og = out_rows[dest.T.reshape(-1)].reshape(TOP_K, t_p + t_s, D_MODEL)

    y_p = _final(h_p, og, gate, mod_p[5], fw, 0, False, seq_p)
    y_s = _final(h_s, og, gate, mod_s[5], fw, t_p, True, seq_s)

    return (y_p.reshape(n_p, seq_p, D_MODEL), y_s.reshape(n_s, seq_s, D_MODEL),
            gla_p[None], conv_p[None], gla_s[None], conv_s[None])
```

```python
import functools

import jax
import jax.numpy as jnp
from jax import lax
from jax.experimental import pallas as pl
from jax.experimental.pallas import tpu as pltpu

f32 = jnp.float32
bf16 = jnp.bfloat16

D_MODEL = 2048
N_MOD = 6
GLA_HEADS = 4
GLA_DK = 512
GLA_DV = 1024
GLA_HK = 128
GLA_HV = 256
GLA_LOW_RANK = 16
GLA_TAU = 16.0
GLA_CHUNK = 64
CONV_CH = 1024
CONV_K = 3
N_EXPERTS = 32
TOP_K = 4
D_FF = 2048
SWIGLU_LIMIT = 7.0
SWIGLU_ALPHA = 1.702
EPS = 1e-6

LANES = 128
D_MAIN = 10240
VMEM_LIMIT = 56 * 1024 * 1024

OFF_Q, OFF_K, OFF_V, OFF_G = 0, 512, 1024, 2048
OFF_CB, OFF_CC, OFF_CH, OFF_GA, OFF_GB = 3072, 4096, 5120, 6144, 8192

ROUTER_PAD = -1e30

HIGHEST = lax.Precision.HIGHEST


def _cparams(sem):
    return pltpu.CompilerParams(dimension_semantics=sem, vmem_limit_bytes=VMEM_LIMIT)


def _dot(a, b):
    return jnp.dot(a, b, preferred_element_type=f32)


def _dot_nt(a, b):
    return lax.dot_general(a, b, (((1,), (1,)), ((), ())), preferred_element_type=f32)


def _dot_exact(a, b):
    return jnp.dot(a, b, precision=HIGHEST, preferred_element_type=f32)


def _rms(x):
    return x * lax.rsqrt(jnp.mean(x * x, axis=-1, keepdims=True) + EPS)


ADA_TN = 1024


def _ada_kernel(c_ref, w_ref, b_ref, o_ref):
    c = c_ref[...]
    s = (c * jax.nn.sigmoid(c)).astype(bf16)
    o_ref[...] = _dot(s, w_ref[...].astype(bf16)) + b_ref[...]


def _ada(c, w, b):
    rows = c.shape[0]
    n = w.shape[1]
    return pl.pallas_call(
        _ada_kernel,
        grid=(n // ADA_TN,),
        in_specs=[pl.BlockSpec((rows, D_MODEL), lambda j: (0, 0)),
                  pl.BlockSpec((D_MODEL, ADA_TN), lambda j: (0, j)),
                  pl.BlockSpec((1, ADA_TN), lambda j: (0, j))],
        out_specs=pl.BlockSpec((rows, ADA_TN), lambda j: (0, j)),
        out_shape=jax.ShapeDtypeStruct((rows, n), f32),
        compiler_params=_cparams(("arbitrary",)),
        name="adaln",
    )(c, w, b.reshape(1, n))


IN_TN = 1024


def _in_kernel(x_ref, sc_ref, sh_ref, nw_ref, w_ref, wa_ref, wgk_ref, bgk_ref,
               proj_ref, la_ref, xn_ref):
    @pl.when(pl.program_id(1) == 0)
    def _():
        xn = _rms(x_ref[...]) * nw_ref[...] * (1.0 + sc_ref[...]) + sh_ref[...]
        xnb = xn.astype(bf16)
        xn_ref[...] = xnb
        a_low = _dot(xnb, wa_ref[...])
        z = _dot(a_low.astype(bf16), wgk_ref[...]) + bgk_ref[...]
        la_ref[...] = (jnp.minimum(z, 0.0) - jnp.log(1.0 + jnp.exp(-jnp.abs(z)))) * (1.0 / GLA_TAU)

    proj_ref[...] = _dot(xn_ref[...], w_ref[...]).astype(bf16)


def _in_proj(x2, sc, sh, nw, w_main, w_alow, w_gk, b_gk, tm, per_row_mod):
    t = x2.shape[0]
    if per_row_mod:
        mod_spec = pl.BlockSpec((tm, D_MODEL), lambda m, n: (m, 0))
    else:
        tiles_per_seq = (t // sc.shape[0]) // tm
        mod_spec = pl.BlockSpec((None, 1, D_MODEL), lambda m, n: (m // tiles_per_seq, 0, 0))
    return pl.pallas_call(
        _in_kernel,
        grid=(t // tm, D_MAIN // IN_TN),
        in_specs=[pl.BlockSpec((tm, D_MODEL), lambda m, n: (m, 0)),
                  mod_spec, mod_spec,
                  pl.BlockSpec((1, D_MODEL), lambda m, n: (0, 0)),
                  pl.BlockSpec((D_MODEL, IN_TN), lambda m, n: (0, n)),
                  pl.BlockSpec((D_MODEL, LANES), lambda m, n: (0, 0)),
                  pl.BlockSpec((LANES, GLA_DK), lambda m, n: (0, 0)),
                  pl.BlockSpec((1, GLA_DK), lambda m, n: (0, 0))],
        out_specs=[pl.BlockSpec((tm, IN_TN), lambda m, n: (m, n)),
                   pl.BlockSpec((tm, GLA_DK), lambda m, n: (m, 0))],
        out_shape=[jax.ShapeDtypeStruct((t, D_MAIN), bf16),
                   jax.ShapeDtypeStruct((t, GLA_DK), f32)],
        scratch_shapes=[pltpu.VMEM((tm, D_MODEL), bf16)],
        compiler_params=_cparams(("arbitrary", "arbitrary")),
        name="in_proj",
    )(x2, sc, sh, nw, w_main, w_alow, w_gk, b_gk)


GLA_R = 256


def _gla_norm_gate(o, nw, g):
    gf = g.astype(f32)
    return _rms(o) * nw * (gf * jax.nn.sigmoid(gf))


def _gla_scan_kernel(q_ref, k_ref, v_ref, g_ref, la_ref, nw_ref, s0_ref, o_ref, sout_ref, s_ref):
    step = pl.program_id(1)

    @pl.when(step == 0)
    def _():
        s_ref[...] = s0_ref[...]

    r, c = GLA_R, GLA_CHUNK
    n_chunks = r // c
    row = lax.broadcasted_iota(jnp.int32, (r, r), 0)
    col = lax.broadcasted_iota(jnp.int32, (r, r), 1)
    same_chunk = (row // c) == (col // c)
    causal = same_chunk & (col <= row)

    la = la_ref[...]
    b = _dot_exact(causal.astype(f32), la)
    b_mid = jnp.concatenate(
        [jnp.broadcast_to(b[i * c + c // 2:i * c + c // 2 + 1], (c, GLA_DK)) for i in range(n_chunks)], axis=0)
    b_last = jnp.concatenate(
        [jnp.broadcast_to(b[i * c + c - 1:i * c + c], (c, GLA_DK)) for i in range(n_chunks)], axis=0)
    q = q_ref[...].astype(f32) * (GLA_HK ** -0.5)
    k = k_ref[...].astype(f32)
    qs = (q * jnp.exp(b - b_mid)).astype(bf16)
    ks = (k * jnp.exp(b_mid - b)).astype(bf16)
    qd = (q * jnp.exp(b)).astype(bf16)
    kd_t = (k * jnp.exp(b_last - b)).T.astype(bf16)
    la_t = la.T
    lane = lax.broadcasted_iota(jnp.int32, (GLA_HK, r), 1)
    nw = nw_ref[...]

    for h in range(GLA_HEADS):
        hk = slice(h * GLA_HK, (h + 1) * GLA_HK)
        hv = slice(h * GLA_HV, (h + 1) * GLA_HV)
        v_h = v_ref[:, hv]
        scores = _dot_nt(qs[:, hk], ks[:, hk])
        o_intra = _dot(jnp.where(causal, scores, 0.0).astype(bf16), v_h)
        s = s_ref[h]
        for i in range(n_chunks):
            rows = slice(i * c, (i + 1) * c)
            in_chunk = (lane // c) == i
            o = o_intra[rows] + _dot(qd[rows, hk], s.astype(bf16))
            o_ref[rows, hv] = _gla_norm_gate(o, nw, g_ref[rows, hv]).astype(bf16)
            decay = jnp.exp(jnp.sum(jnp.where(in_chunk, la_t[hk], 0.0), axis=1, keepdims=True))
            s = decay * s + _dot(jnp.where(in_chunk, kd_t[hk], jnp.zeros_like(kd_t[hk])), v_h)
        s_ref[h] = s

    @pl.when(step == pl.num_programs(1) - 1)
    def _():
        sout_ref[...] = s_ref[...]


def _gla_scan(proj, la, nw, s0, n_seq, seq_len):
    steps = seq_len // GLA_R
    t = n_seq * seq_len

    def rows(b, s):
        return b * steps + s

    return pl.pallas_call(
        _gla_scan_kernel,
        grid=(n_seq, steps),
        in_specs=[pl.BlockSpec((GLA_R, GLA_DK), lambda b, s: (rows(b, s), OFF_Q // GLA_DK)),
                  pl.BlockSpec((GLA_R, GLA_DK), lambda b, s: (rows(b, s), OFF_K // GLA_DK)),
                  pl.BlockSpec((GLA_R, GLA_DV), lambda b, s: (rows(b, s), OFF_V // GLA_DV)),
                  pl.BlockSpec((GLA_R, GLA_DV), lambda b, s: (rows(b, s), OFF_G // GLA_DV)),
                  pl.BlockSpec((GLA_R, GLA_DK), lambda b, s: (rows(b, s), 0)),
                  pl.BlockSpec((1, GLA_HV), lambda b, s: (0, 0)),
                  pl.BlockSpec((None, GLA_HEADS, GLA_HK, GLA_HV), lambda b, s: (b, 0, 0, 0))],
        out_specs=[pl.BlockSpec((GLA_R, GLA_DV), lambda b, s: (rows(b, s), 0)),
                   pl.BlockSpec((None, GLA_HEADS, GLA_HK, GLA_HV), lambda b, s: (b, 0, 0, 0))],
        out_shape=[jax.ShapeDtypeStruct((t, GLA_DV), bf16),
                   jax.ShapeDtypeStruct((n_seq, GLA_HEADS, GLA_HK, GLA_HV), f32)],
        scratch_shapes=[pltpu.VMEM((GLA_HEADS, GLA_HK, GLA_HV), f32)],
        compiler_params=_cparams(("arbitrary", "arbitrary")),
        name="gla_scan",
    )(proj, proj, proj, proj, la, nw, s0)


GS_SEQ = 4
GS_ROWS = 128
GS_B = GS_ROWS // GS_SEQ


def _gla_step_kernel(q_ref, k_ref, v_ref, g_ref, la_ref, nw_ref, s0_ref, o_ref, sout_ref,
                     oacc_ref, qd_ref, kdt_ref, lat_ref):
    j = pl.program_id(1)
    r = GS_ROWS
    row = lax.broadcasted_iota(jnp.int32, (r, r), 0)
    col = lax.broadcasted_iota(jnp.int32, (r, r), 1)

    @pl.when(j == 0)
    def _():
        same_seq = (row // GS_SEQ) == (col // GS_SEQ)
        causal = same_seq & (col <= row)
        la = la_ref[...]
        b = _dot_exact(causal.astype(f32), la)
        b_mid = _dot_exact((same_seq & (col % GS_SEQ <= GS_SEQ // 2)).astype(f32), la)
        b_last = _dot_exact(same_seq.astype(f32), la)
        q = q_ref[...].astype(f32) * (GLA_HK ** -0.5)
        k = k_ref[...].astype(f32)
        qs = (q * jnp.exp(b - b_mid)).astype(bf16)
        ks = (k * jnp.exp(b_mid - b)).astype(bf16)
        qd_ref[...] = (q * jnp.exp(b)).astype(bf16)
        kdt_ref[...] = (k * jnp.exp(b_last - b)).T.astype(bf16)
        lat_ref[...] = la.T
        for h in range(GLA_HEADS):
            hk = slice(h * GLA_HK, (h + 1) * GLA_HK)
            hv = slice(h * GLA_HV, (h + 1) * GLA_HV)
            scores = _dot_nt(qs[:, hk], ks[:, hk])
            oacc_ref[:, hv] = _dot(jnp.where(causal, scores, 0.0).astype(bf16), v_ref[:, hv])

    in_seq_lane = (col // GS_SEQ) == j
    in_seq_row = (row[:, :GLA_HK] // GS_SEQ) == j
    for h in range(GLA_HEADS):
        hk = slice(h * GLA_HK, (h + 1) * GLA_HK)
        hv = slice(h * GLA_HV, (h + 1) * GLA_HV)
        s0 = s0_ref[h]
        kd_t = kdt_ref[hk, :]
        ds = _dot(jnp.where(in_seq_lane, kd_t, jnp.zeros_like(kd_t)), v_ref[:, hv])
        decay = jnp.exp(jnp.sum(jnp.where(in_seq_lane, lat_ref[hk, :], 0.0), axis=1, keepdims=True))
        sout_ref[h] = decay * s0 + ds
        qd = qd_ref[:, hk]
        oacc_ref[:, hv] += _dot(jnp.where(in_seq_row, qd, jnp.zeros_like(qd)), s0.astype(bf16))

    @pl.when(j == pl.num_programs(1) - 1)
    def _():
        nw = nw_ref[...]
        for h in range(GLA_HEADS):
            hv = slice(h * GLA_HV, (h + 1) * GLA_HV)
            o_ref[:, hv] = _gla_norm_gate(oacc_ref[:, hv], nw, g_ref[:, hv]).astype(bf16)


def _gla_step(proj, la, nw, s0):
    n_seq = s0.shape[0]
    t = n_seq * GS_SEQ
    groups = t // GS_ROWS
    return pl.pallas_call(
        _gla_step_kernel,
        grid=(groups, GS_B),
        in_specs=[pl.BlockSpec((GS_ROWS, GLA_DK), lambda g, j: (g, OFF_Q // GLA_DK)),
                  pl.BlockSpec((GS_ROWS, GLA_DK), lambda g, j: (g, OFF_K // GLA_DK)),
                  pl.BlockSpec((GS_ROWS, GLA_DV), lambda g, j: (g, OFF_V // GLA_DV)),
                  pl.BlockSpec((GS_ROWS, GLA_DV), lambda g, j: (g, OFF_G // GLA_DV)),
                  pl.BlockSpec((GS_ROWS, GLA_DK), lambda g, j: (g, 0)),
                  pl.BlockSpec((1, GLA_HV), lambda g, j: (0, 0)),
                  pl.BlockSpec((None, GLA_HEADS, GLA_HK, GLA_HV), lambda g, j: (g * GS_B + j, 0, 0, 0))],
        out_specs=[pl.BlockSpec((GS_ROWS, GLA_DV), lambda g, j: (g, 0)),
                   pl.BlockSpec((None, GLA_HEADS, GLA_HK, GLA_HV), lambda g, j: (g * GS_B + j, 0, 0, 0))],
        out_shape=[jax.ShapeDtypeStruct((t, GLA_DV), bf16),
                   jax.ShapeDtypeStruct((n_seq, GLA_HEADS, GLA_HK, GLA_HV), f32)],
        scratch_shapes=[pltpu.VMEM((GS_ROWS, GLA_DV), f32),
                        pltpu.VMEM((GS_ROWS, GLA_DK), bf16),
                        pltpu.VMEM((GLA_DK, GS_ROWS), bf16),
                        pltpu.VMEM((GLA_DK, GS_ROWS), f32)],
        compiler_params=_cparams(("arbitrary", "arbitrary")),
        name="gla_step",
    )(proj, proj, proj, proj, la, nw, s0)


POST_TM = 256
HALO = 16


def _post_kernel(long_seq, tiles_per_seq, *refs):
    if long_seq:
        (x_ref, oa_ref, cb_ref, cc_ref, ch_ref, ga_ref, gb_ref, hcc_ref, hch_ref, cbuf_ref,
         wc_ref, g1_ref, sc_ref, sh_ref, nw_ref, wo_ref, wrh_ref, wrl_ref, br_ref, _hn_alias,
         h_ref, hn_ref, gate_ref, eid_ref, u_ref) = refs
    else:
        (x_ref, oa_ref, cb_ref, cc_ref, ch_ref, ga_ref, gb_ref, p0_ref, p1_ref,
         wc_ref, g1_ref, sc_ref, sh_ref, nw_ref, wo_ref, wrh_ref, wrl_ref, br_ref, _hn_alias,
         h_ref, hn_ref, gate_ref, eid_ref, u_ref) = refs
    tm = x_ref.shape[0]
    u = cc_ref[...].astype(f32) * ch_ref[...].astype(f32)
    row = lax.broadcasted_iota(jnp.int32, (tm, CONV_CH), 0)
    if long_seq:
        pos = row
        first = (pl.program_id(0) % tiles_per_seq) == 0
        halo = hcc_ref[HALO - 2:HALO, :].astype(f32) * hch_ref[HALO - 2:HALO, :].astype(f32)
        cbuf = cbuf_ref[...]
        p0 = jnp.where(first, cbuf[0:1], halo[0:1])
        p1 = jnp.where(first, cbuf[1:2], halo[1:2])
        u_ref[...] = u[tm - 8:tm]
    else:
        pos = row % GS_SEQ
        p0 = p0_ref[...]
        p1 = p1_ref[...]
        u_ref[...] = u
    u1 = jnp.where(pos == 0, p1, pltpu.roll(u, 1, 0))
    u2 = jnp.where(pos == 0, p0, jnp.where(pos == 1, p1, pltpu.roll(u, 2, 0)))
    wc = wc_ref[...]
    conv = wc[0:1] * u2 + wc[1:2] * u1 + wc[2:3] * u
    ob = (cb_ref[...].astype(f32) * conv).astype(bf16)

    ya = _dot(oa_ref[...], wo_ref[0:GLA_DV, :])
    yb = _dot(ob, wo_ref[GLA_DV:GLA_DV + CONV_CH, :])
    y = jax.nn.sigmoid(ga_ref[...].astype(f32)) * ya + jax.nn.sigmoid(gb_ref[...].astype(f32)) * yb
    h = x_ref[...] + g1_ref[...] * y
    h_ref[...] = h
    hn = _rms(h) * nw_ref[...] * (1.0 + sc_ref[...]) + sh_ref[...]
    hn_hi = hn.astype(bf16)
    hn_ref[...] = hn_hi
    hn_lo = (hn - hn_hi.astype(f32)).astype(bf16)
    lg = (_dot(hn_hi, wrh_ref[...]) + _dot(hn_hi, wrl_ref[...]) + _dot(hn_lo, wrh_ref[...])
          + br_ref[...])

    lane = lax.broadcasted_iota(jnp.int32, (tm, LANES), 1)
    lane_f = lane.astype(f32)
    vals, idxs = [], []
    for _ in range(TOP_K):
        m = jnp.max(lg, axis=-1, keepdims=True)
        idx = jnp.min(jnp.where(lg == m, lane_f, float(LANES)), axis=-1, keepdims=True)
        vals.append(m)
        idxs.append(idx)
        lg = jnp.where(lane_f == idx, -jnp.inf, lg)
    exps = [jnp.exp(v - vals[0]) for v in vals]
    den = exps[0] + exps[1] + exps[2] + exps[3]
    gate = jnp.zeros((tm, LANES), f32)
    eid = jnp.zeros((tm, LANES), f32)
    for j in range(TOP_K):
        gate = jnp.where(lane == j, exps[j] / den, gate)
        eid = jnp.where(lane == j, idxs[j], eid)
    gate_ref[...] = gate
    eid_ref[...] = eid.astype(jnp.int32)


def _post(x2, oa, proj, prev, wc, g1, sc2, sh2, nw, wo, wrh, wrl, br, hn_buf, row0, long_seq, seq_len):
    t = x2.shape[0]
    tm = POST_TM
    n_tiles = t // tm
    blk0 = row0 // tm
    tiles_per_seq = seq_len // tm if long_seq else 1

    def colblk(off, width):
        return pl.BlockSpec((tm, width), lambda m: (m, off // width))

    const = lambda shape: pl.BlockSpec(shape, lambda m: tuple(0 for _ in shape))
    if long_seq:
        mod_spec = pl.BlockSpec((None, 1, D_MODEL), lambda m: (m // tiles_per_seq, 0, 0))
        halo_rows = lambda m: jnp.maximum(m * (tm // HALO) - 1, 0)
        prev_specs = [pl.BlockSpec((HALO, CONV_CH), lambda m: (halo_rows(m), OFF_CC // CONV_CH)),
                      pl.BlockSpec((HALO, CONV_CH), lambda m: (halo_rows(m), OFF_CH // CONV_CH)),
                      pl.BlockSpec((None, CONV_K - 1, CONV_CH), lambda m: (m // tiles_per_seq, 0, 0))]
        prev_args = [proj, proj, prev]
        u_spec = pl.BlockSpec((None, 8, CONV_CH), lambda m: (m, 0, 0))
        u_shape = jax.ShapeDtypeStruct((n_tiles, 8, CONV_CH), f32)
    else:
        mod_spec = pl.BlockSpec((tm, D_MODEL), lambda m: (m, 0))
        prev_specs = [pl.BlockSpec((tm, CONV_CH), lambda m: (m, 0))] * 2
        prev_args = list(prev)
        u_spec = pl.BlockSpec((tm, CONV_CH), lambda m: (m, 0))
        u_shape = jax.ShapeDtypeStruct((t, CONV_CH), f32)
    n_in = 7 + len(prev_specs) + 10
    return pl.pallas_call(
        functools.partial(_post_kernel, long_seq, tiles_per_seq),
        grid=(n_tiles,),
        in_specs=[pl.BlockSpec((tm, D_MODEL), lambda m: (m, 0)),
                  pl.BlockSpec((tm, GLA_DV), lambda m: (m, 0)),
                  colblk(OFF_CB, CONV_CH), colblk(OFF_CC, CONV_CH), colblk(OFF_CH, CONV_CH),
                  colblk(OFF_GA, D_MODEL), colblk(OFF_GB, D_MODEL)]
                 + prev_specs
                 + [const((CONV_K, CONV_CH)), mod_spec, mod_spec, mod_spec, const((1, D_MODEL)),
                    const((GLA_DV + CONV_CH, D_MODEL)), const((D_MODEL, LANES)), const((D_MODEL, LANES)),
                    const((1, LANES)), pl.BlockSpec(memory_space=pl.ANY)],
        out_specs=[pl.BlockSpec((tm, D_MODEL), lambda m: (m, 0)),
                   pl.BlockSpec((tm, D_MODEL), lambda m: (blk0 + m, 0)),
                   pl.BlockSpec((tm, LANES), lambda m: (m, 0)),
                   pl.BlockSpec((tm, LANES), lambda m: (m, 0)),
                   u_spec],
        out_shape=[jax.ShapeDtypeStruct((t, D_MODEL), f32),
                   jax.ShapeDtypeStruct(hn_buf.shape, bf16),
                   jax.ShapeDtypeStruct((t, LANES), f32),
                   jax.ShapeDtypeStruct((t, LANES), jnp.int32),
                   u_shape],
        input_output_aliases={n_in - 1: 1},
        compiler_params=_cparams(("arbitrary",)),
        name="post_mix",
    )(x2, oa, proj, proj, proj, proj, proj, *prev_args, wc, g1, sc2, sh2, nw, wo, wrh, wrl, br, hn_buf)


MOE_RB = 128
MOE_TM = 1536
MOE_TN = 512
MOE_KB = MOE_TM // MOE_RB
MOE_NT = D_FF // MOE_TN


def _moe_kernel(ie_ref, ib_ref, in_ref, iv_ref, iz_ref,
                x_hbm, wg_ref, wl_ref, wd_ref, bg_ref, bl_ref, bd_ref, out_hbm,
                xbuf, hbuf, obuf, wcast, sem_in, sem_out):
    i = pl.program_id(0)
    s = pl.program_id(1)
    n_items = pl.num_programs(0)
    nb = in_ref[i]

    def x_copy(item, jb, k):
        return pltpu.make_async_copy(
            x_hbm.at[pl.ds((ib_ref[item] + jb) * MOE_RB, MOE_RB), pl.ds(k * MOE_TN, MOE_TN)],
            xbuf.at[k, pl.ds(jb * MOE_RB, MOE_RB)], sem_in)

    def out_copy(item, jb, k):
        return pltpu.make_async_copy(
            obuf.at[k, pl.ds(jb * MOE_RB, MOE_RB)],
            out_hbm.at[pl.ds((ib_ref[item] + jb) * MOE_RB, MOE_RB), pl.ds(k * MOE_TN, MOE_TN)], sem_out)

    def for_blocks(n, fn):
        def body(jb, carry):
            for k in range(MOE_NT):
                fn(jb, k)
            return carry

        lax.fori_loop(0, n, body, 0)

    def for_rows(fn):
        def body(j, carry):
            fn(pl.multiple_of(j * (4 * MOE_RB), 4 * MOE_RB), 4 * MOE_RB)
            return carry

        lax.fori_loop(0, nb // 4, body, 0)
        base = (nb // 4) * (4 * MOE_RB)
        rem = nb % 4

        @pl.when(rem >= 2)
        def _():
            fn(pl.multiple_of(base, MOE_RB), 2 * MOE_RB)

        @pl.when(rem % 2 == 1)
        def _():
            fn(pl.multiple_of(base + (rem // 2) * (2 * MOE_RB), MOE_RB), MOE_RB)

    @pl.when(s == 0)
    def _():
        @pl.when(i == 0)
        def _():
            for_blocks(nb, lambda jb, k: x_copy(i, jb, k).start())

        for_blocks(nb, lambda jb, k: x_copy(i, jb, k).wait())

    @pl.when(s < MOE_NT)
    def _():
        wcast[:, 0:MOE_TN] = wg_ref[...].astype(bf16)
        wcast[:, MOE_TN:2 * MOE_TN] = wl_ref[...].astype(bf16)
        bg = bg_ref[...]
        bl = bl_ref[...]

        def up(r0, size):
            rows = pl.ds(r0, size)
            xb = jnp.concatenate([xbuf[k, rows, :] for k in range(MOE_NT)], axis=1)
            h = _dot(xb, wcast[...])
            hg = jnp.minimum(h[:, 0:MOE_TN] + bg, SWIGLU_LIMIT)
            hl = jnp.clip(h[:, MOE_TN:2 * MOE_TN] + bl, -SWIGLU_LIMIT, SWIGLU_LIMIT)
            hbuf[s, rows, :] = ((hl + 1.0) * hg * jax.nn.sigmoid(SWIGLU_ALPHA * hg)).astype(bf16)

        for_rows(up)

    @pl.when(s == MOE_NT)
    def _():
        @pl.when(i > 0)
        def _():
            prev = jnp.maximum(i - 1, 0)
            for_blocks(in_ref[prev], lambda jb, k: out_copy(prev, jb, k).wait())

        def fill(jb, k):
            obuf[k, pl.ds(pl.multiple_of(jb * MOE_RB, MOE_RB), MOE_RB), :] = jnp.zeros((MOE_RB, MOE_TN), bf16)
            out_copy(i, jb, k).start()

        nz = iz_ref[i]
        for_blocks(nz, fill)
        for_blocks(nz, lambda jb, k: out_copy(i, jb, k).wait())

        @pl.when(i + 1 < n_items)
        def _():
            nxt = jnp.minimum(i + 1, n_items - 1)
            for_blocks(in_ref[nxt], lambda jb, k: x_copy(nxt, jb, k).start())

    @pl.when(s >= MOE_NT)
    def _():
        wcast[:, 0:MOE_TN] = wd_ref[...].astype(bf16)
        bd = bd_ref[...]

        def down(r0, size):
            rows = pl.ds(r0, size)
            hb = jnp.concatenate([hbuf[k, rows, :] for k in range(MOE_NT)], axis=1)
            obuf[s - MOE_NT, rows, :] = (_dot(hb, wcast[:, 0:MOE_TN]) + bd).astype(bf16)

        for_rows(down)

    @pl.when(s == 2 * MOE_NT - 1)
    def _():
        for_blocks(nb, lambda jb, k: out_copy(i, jb, k).start())

        @pl.when(i == n_items - 1)
        def _():
            for_blocks(nb, lambda jb, k: out_copy(i, jb, k).wait())


def _moe(xs, items, w_up, b_up, w_down, b_down):
    rows = xs.shape[0]
    n_items = items[0].shape[0]

    last = MOE_NT - 1

    def up_tile(i, s, tabs):
        valid = tabs[3][i]
        return jnp.minimum(s, last) * valid + last * (1 - valid)

    def down_tile(i, s, tabs):
        valid = tabs[3][i]
        return jnp.maximum(s - MOE_NT, 0) * valid + last * (1 - valid)

    def expert(i, tabs):
        return tabs[0][i]

    grid_spec = pltpu.PrefetchScalarGridSpec(
        num_scalar_prefetch=len(items),
        grid=(n_items, 2 * MOE_NT),
        in_specs=[pl.BlockSpec(memory_space=pl.ANY),
                  pl.BlockSpec((None, D_MODEL, MOE_TN), lambda i, s, *t: (expert(i, t), 0, up_tile(i, s, t))),
                  pl.BlockSpec((None, D_MODEL, MOE_TN), lambda i, s, *t: (expert(i, t), 0, MOE_NT + up_tile(i, s, t))),
                  pl.BlockSpec((None, D_FF, MOE_TN), lambda i, s, *t: (expert(i, t), 0, down_tile(i, s, t))),
                  pl.BlockSpec((None, 1, MOE_TN), lambda i, s, *t: (expert(i, t), 0, up_tile(i, s, t))),
                  pl.BlockSpec((None, 1, MOE_TN), lambda i, s, *t: (expert(i, t), 0, MOE_NT + up_tile(i, s, t))),
                  pl.BlockSpec((None, 1, MOE_TN), lambda i, s, *t: (expert(i, t), 0, down_tile(i, s, t)))],
        out_specs=pl.BlockSpec(memory_space=pl.ANY),
        scratch_shapes=[pltpu.VMEM((MOE_NT, MOE_TM, MOE_TN), bf16),
                        pltpu.VMEM((MOE_NT, MOE_TM, MOE_TN), bf16),
                        pltpu.VMEM((MOE_NT, MOE_TM, MOE_TN), bf16),
                        pltpu.VMEM((D_MODEL, 2 * MOE_TN), bf16),
                        pltpu.SemaphoreType.DMA(()),
                        pltpu.SemaphoreType.DMA(())],
    )
    return pl.pallas_call(
        _moe_kernel,
        grid_spec=grid_spec,
        out_shape=jax.ShapeDtypeStruct((rows, D_MODEL), bf16),
        compiler_params=_cparams(("arbitrary", "arbitrary")),
        name="moe_experts",
    )(*items,
      xs, w_up, w_up, w_down,
      b_up.reshape(N_EXPERTS, 1, 2 * D_FF), b_up.reshape(N_EXPERTS, 1, 2 * D_FF),
      b_down.reshape(N_EXPERTS, 1, D_MODEL))


FIN_TM = 256


def _final_kernel(h_ref, og_ref, gate_ref, g2_ref, fw_ref, y_ref):
    gate = gate_ref[...]
    ff = og_ref[0].astype(f32) * gate[:, 0:1]
    for j in range(1, TOP_K):
        ff = ff + og_ref[j].astype(f32) * gate[:, j:j + 1]
    h = h_ref[...] + g2_ref[...] * ff
    y_ref[...] = _rms(h) * fw_ref[...]


def _final(h, og, gate, g2, fw, row0, per_row_mod, seq_len):
    t = h.shape[0]
    tm = FIN_TM
    blk0 = row0 // tm
    if per_row_mod:
        mod_spec = pl.BlockSpec((tm, D_MODEL), lambda m: (m, 0))
    else:
        tiles_per_seq = seq_len // tm
        mod_spec = pl.BlockSpec((None, 1, D_MODEL), lambda m: (m // tiles_per_seq, 0, 0))
    return pl.pallas_call(
        _final_kernel,
        grid=(t // tm,),
        in_specs=[pl.BlockSpec((tm, D_MODEL), lambda m: (m, 0)),
                  pl.BlockSpec((TOP_K, tm, D_MODEL), lambda m: (0, blk0 + m, 0)),
                  pl.BlockSpec((tm, LANES), lambda m: (blk0 + m, 0)),
                  mod_spec,
                  pl.BlockSpec((1, D_MODEL), lambda m: (0, 0))],
        out_specs=pl.BlockSpec((tm, D_MODEL), lambda m: (m, 0)),
        out_shape=jax.ShapeDtypeStruct((t, D_MODEL), f32),
        compiler_params=_cparams(("arbitrary",)),
        name="final_norm",
    )(h, og, gate, g2, fw)


def _count_le(bounds, x):
    return jnp.sum((bounds[None, :] <= x[:, None]).astype(jnp.int32), axis=1)


def _routing(top_e):
    t = top_e.shape[0]
    n_pairs = t * TOP_K
    e_flat = top_e.reshape(-1)
    onehot = (e_flat[:, None] == jnp.arange(N_EXPERTS, dtype=e_flat.dtype)[None, :]).astype(jnp.int32)
    csum = jnp.cumsum(onehot, axis=0)
    rank = jnp.sum(onehot * csum, axis=1) - 1
    counts = csum[-1]
    nblk = (counts + MOE_RB - 1) // MOE_RB
    blk_end = jnp.cumsum(nblk)
    blk_start = blk_end - nblk
    dest = blk_start[e_flat] * MOE_RB + rank
    n_rows = (n_pairs // MOE_RB + N_EXPERTS) * MOE_RB
    order = jnp.argsort(e_flat, stable=True).astype(jnp.int32)
    raw_start = jnp.cumsum(counts) - counts
    r = jnp.arange(n_rows, dtype=jnp.int32)
    e_row = jnp.minimum(_count_le(blk_end * MOE_RB, r), N_EXPERTS - 1)
    off = r - blk_start[e_row] * MOE_RB
    src_pair = order[jnp.clip(raw_start[e_row] + off, 0, n_pairs - 1)]
    src_tok = jnp.where(off < counts[e_row], src_pair // TOP_K, r % t).astype(jnp.int32)

    n_items = N_EXPERTS + n_rows // MOE_TM
    nit = (nblk + MOE_KB - 1) // MOE_KB
    it_end = jnp.cumsum(nit)
    it_start = it_end - nit
    idx = jnp.arange(n_items, dtype=jnp.int32)
    valid = idx < it_end[-1]
    last_valid = jnp.maximum(it_end[-1] - 1, 0)
    idx_c = jnp.minimum(idx, last_valid)
    e_of = jnp.minimum(_count_le(it_end, idx_c), N_EXPERTS - 1).astype(jnp.int32)
    k = idx_c - it_start[e_of]
    item_nb = jnp.where(valid, jnp.clip(nblk[e_of] - k * MOE_KB, 0, MOE_KB), 0).astype(jnp.int32)
    spare = idx - it_end[-1]
    tail_blk = blk_end[-1] + spare * MOE_KB
    item_nz = jnp.where(valid, 0, jnp.clip(n_rows // MOE_RB - tail_blk, 0, MOE_KB)).astype(jnp.int32)
    item_blk = jnp.where(valid, blk_start[e_of] + k * MOE_KB, tail_blk).astype(jnp.int32)
    items = (e_of, item_blk, item_nb, valid.astype(jnp.int32), item_nz)
    return dest.reshape(t, TOP_K), src_tok, items


def kernel(x_prompt, x_sample, state_gla, state_conv, c_prompt, c_sample, w_ada, b_ada, norm1_w,
           w_in, w_gk_up, b_gk, gla_norm_w, w_conv, w_out, norm2_w, w_router, b_router, w_up, b_up,
           w_down, b_down, final_norm_w):
    n_p, seq_p, _ = x_prompt.shape
    n_s, seq_s, _ = x_sample.shape
    t_p, t_s = n_p * seq_p, n_s * seq_s
    assert seq_s == GS_SEQ and w_ada.shape[0] == 1

    c_all = jnp.concatenate([c_prompt, c_sample], axis=0)
    pad = (-c_all.shape[0]) % 16
    c_all = jnp.pad(c_all, ((0, pad), (0, 0)))
    mod = _ada(c_all, w_ada[0], b_ada[0])
    mod_p = [mod[:n_p, i * D_MODEL:(i + 1) * D_MODEL].reshape(n_p, 1, D_MODEL) for i in range(N_MOD)]
    mod_s = [jnp.repeat(mod[n_p:n_p + n_s, i * D_MODEL:(i + 1) * D_MODEL], seq_s, axis=0) for i in range(N_MOD)]

    w_in0 = w_in[0]
    a0 = OFF_G
    w_main = jnp.concatenate([w_in0[:, :a0], w_in0[:, a0 + GLA_LOW_RANK:]], axis=1).astype(bf16)
    w_alow = jnp.pad(w_in0[:, a0:a0 + GLA_LOW_RANK], ((0, 0), (0, LANES - GLA_LOW_RANK))).astype(bf16)
    w_gk = jnp.pad(w_gk_up[0], ((0, LANES - GLA_LOW_RANK), (0, 0))).astype(bf16)
    bgk = b_gk[0].reshape(1, GLA_DK)
    wo = w_out[0].astype(bf16)
    wr = jnp.pad(w_router[0], ((0, 0), (0, LANES - N_EXPERTS)))
    wr_hi = wr.astype(bf16)
    wr_lo = (wr - wr_hi.astype(f32)).astype(bf16)
    br = jnp.pad(b_router[0], (0, LANES - N_EXPERTS), constant_values=ROUTER_PAD).reshape(1, LANES)
    n1w = norm1_w[0].reshape(1, D_MODEL)
    n2w = norm2_w[0].reshape(1, D_MODEL)
    gnw = gla_norm_w[0].reshape(1, GLA_HV)
    fw = final_norm_w.reshape(1, D_MODEL)
    wc = w_conv[0]

    xp = x_prompt.reshape(t_p, D_MODEL)
    xs_ = x_sample.reshape(t_s, D_MODEL)

    proj_p, la_p = _in_proj(xp, mod_p[1], mod_p[0], n1w, w_main, w_alow, w_gk, bgk, 1024, False)
    proj_s, la_s = _in_proj(xs_, mod_s[1], mod_s[0], n1w, w_main, w_alow, w_gk, bgk, t_s, True)

    gla0 = jnp.zeros((n_p, GLA_HEADS, GLA_HK, GLA_HV), f32)
    conv0 = jnp.zeros((n_p, CONV_K - 1, CONV_CH), f32)
    oa_p, gla_p = _gla_scan(proj_p, la_p, gnw, gla0, n_p, seq_p)
    oa_s, gla_s = _gla_step(proj_s, la_s, gnw, state_gla[0])

    hn = jnp.zeros((2 * (t_p + t_s), D_MODEL), bf16)
    h_p, hn, gate_p, eid_p, ut_p = _post(xp, oa_p, proj_p, conv0, wc, mod_p[2], mod_p[4], mod_p[3], n2w, wo,
                                         wr_hi, wr_lo, br, hn, 0, True, seq_p)
    prev_s = (jnp.repeat(state_conv[0][:, 0], seq_s, axis=0), jnp.repeat(state_conv[0][:, 1], seq_s, axis=0))
    h_s, hn, gate_s, eid_s, u_s = _post(xs_, oa_s, proj_s, prev_s, wc, mod_s[2], mod_s[4], mod_s[3], n2w, wo,
                                        wr_hi, wr_lo, br, hn, t_p, False, seq_s)
    conv_p = ut_p.reshape(n_p, seq_p // POST_TM, 8, CONV_CH)[:, -1, 8 - (CONV_K - 1):]
    conv_s = u_s.reshape(n_s, seq_s, CONV_CH)[:, seq_s - (CONV_K - 1):]

    gate = jnp.concatenate([gate_p, gate_s], axis=0)
    top_e = jnp.concatenate([eid_p, eid_s], axis=0)[:, :TOP_K]
    dest, src_tok, items = _routing(top_e)
    x_rows = hn[src_tok]
    out_rows = _moe(x_rows, items, w_up[0], b_up[0], w_down[0], b_down[0])
    og = out_rows[dest.T.reshape(-1)].reshape(TOP_K, t_p + t_s, D_MODEL)

    y_p = _final(h_p, og, gate, mod_p[5], fw, 0, False, seq_p)
    y_s = _final(h_s, og, gate, mod_s[5], fw, t_p, True, seq_s)

    return (y_p.reshape(n_p, seq_p, D_MODEL), y_s.reshape(n_s, seq_s, D_MODEL),
            gla_p[None], conv_p[None], gla_s[None], conv_s[None])
```

```python
import functools

import jax
import jax.numpy as jnp
from jax import lax
from jax.experimental import pallas as pl
from jax.experimental.pallas import tpu as pltpu

f32 = jnp.float32
bf16 = jnp.bfloat16

D_MODEL = 2048
N_MOD = 6
GLA_HEADS = 4
GLA_DK = 512
GLA_DV = 1024
GLA_HK = 128
GLA_HV = 256
GLA_LOW_RANK = 16
GLA_TAU = 16.0
GLA_CHUNK = 64
CONV_CH = 1024
CONV_K = 3
N_EXPERTS = 32
TOP_K = 4
D_FF = 2048
SWIGLU_LIMIT = 7.0
SWIGLU_ALPHA = 1.702
EPS = 1e-6

LANES = 128
D_MAIN = 10240
VMEM_LIMIT = 56 * 1024 * 1024

OFF_Q, OFF_K, OFF_V, OFF_G = 0, 512, 1024, 2048
OFF_CB, OFF_CC, OFF_CH, OFF_GA, OFF_GB = 3072, 4096, 5120, 6144, 8192

ROUTER_PAD = -1e30

HIGHEST = lax.Precision.HIGHEST


def _cparams(sem):
    return pltpu.CompilerParams(dimension_semantics=sem, vmem_limit_bytes=VMEM_LIMIT)


def _dot(a, b):
    return jnp.dot(a, b, preferred_element_type=f32)


def _dot_nt(a, b):
    return lax.dot_general(a, b, (((1,), (1,)), ((), ())), preferred_element_type=f32)


def _dot_exact(a, b):
    return jnp.dot(a, b, precision=HIGHEST, preferred_element_type=f32)


def _rms(x):
    return x * lax.rsqrt(jnp.mean(x * x, axis=-1, keepdims=True) + EPS)


ADA_TN = 1024


def _ada_kernel(c_ref, w_ref, b_ref, o_ref):
    c = c_ref[...]
    s = (c * jax.nn.sigmoid(c)).astype(bf16)
    o_ref[...] = _dot(s, w_ref[...].astype(bf16)) + b_ref[...]


def _ada(c, w, b):
    rows = c.shape[0]
    n = w.shape[1]
    return pl.pallas_call(
        _ada_kernel,
        grid=(n // ADA_TN,),
        in_specs=[pl.BlockSpec((rows, D_MODEL), lambda j: (0, 0)),
                  pl.BlockSpec((D_MODEL, ADA_TN), lambda j: (0, j)),
                  pl.BlockSpec((1, ADA_TN), lambda j: (0, j))],
        out_specs=pl.BlockSpec((rows, ADA_TN), lambda j: (0, j)),
        out_shape=jax.ShapeDtypeStruct((rows, n), f32),
        compiler_params=_cparams(("arbitrary",)),
        name="adaln",
    )(c, w, b.reshape(1, n))


PREP_TM = 512
PREP_TN = 1024


def _prep_kernel(a_ref, b_ref, low_ref, main_ref, alow_ref):
    j = pl.program_id(1)
    first_shifted = OFF_G // PREP_TN

    @pl.when(j < first_shifted)
    def _():
        main_ref[...] = a_ref[...].astype(bf16)

    @pl.when(j >= first_shifted)
    def _():
        wide = jnp.concatenate([a_ref[...], b_ref[...]], axis=1)
        main_ref[...] = wide[:, GLA_LOW_RANK:GLA_LOW_RANK + PREP_TN].astype(bf16)

    @pl.when(j == 0)
    def _():
        lane = lax.broadcasted_iota(jnp.int32, low_ref.shape, 1)
        alow_ref[...] = jnp.where(lane < GLA_LOW_RANK, low_ref[...], 0.0).astype(bf16)


def _prep_w_in(w):
    d = w.shape[0]
    return pl.pallas_call(
        _prep_kernel,
        grid=(d // PREP_TM, D_MAIN // PREP_TN),
        in_specs=[pl.BlockSpec((PREP_TM, PREP_TN), lambda i, j: (i, j)),
                  pl.BlockSpec((PREP_TM, LANES), lambda i, j: (i, (j + 1) * (PREP_TN // LANES))),
                  pl.BlockSpec((PREP_TM, LANES), lambda i, j: (i, OFF_G // LANES))],
        out_specs=[pl.BlockSpec((PREP_TM, PREP_TN), lambda i, j: (i, j)),
                   pl.BlockSpec((PREP_TM, LANES), lambda i, j: (i, 0))],
        out_shape=[jax.ShapeDtypeStruct((d, D_MAIN), bf16),
                   jax.ShapeDtypeStruct((d, LANES), bf16)],
        compiler_params=_cparams(("arbitrary", "arbitrary")),
        name="prep_w_in",
    )(w, w, w)


IN_TN = 1024


def _in_kernel(x_ref, sc_ref, sh_ref, nw_ref, w_ref, wa_ref, wgk_ref, bgk_ref,
               proj_ref, la_ref, xn_ref):
    @pl.when(pl.program_id(1) == 0)
    def _():
        xn = _rms(x_ref[...]) * nw_ref[...] * (1.0 + sc_ref[...]) + sh_ref[...]
        xnb = xn.astype(bf16)
        xn_ref[...] = xnb
        a_low = _dot(xnb, wa_ref[...])
        z = _dot(a_low.astype(bf16), wgk_ref[...]) + bgk_ref[...]
        la_ref[...] = (jnp.minimum(z, 0.0) - jnp.log(1.0 + jnp.exp(-jnp.abs(z)))) * (1.0 / GLA_TAU)

    proj_ref[...] = _dot(xn_ref[...], w_ref[...]).astype(bf16)


MOD_SH1, MOD_SC1, MOD_G1, MOD_SH2, MOD_SC2, MOD_G2 = range(N_MOD)


def _mod_spec(mod, which, tm, seq_len):
    if mod.ndim == 3:
        tiles_per_seq = seq_len // tm
        return pl.BlockSpec((None, 1, D_MODEL), lambda *g: (g[0] // tiles_per_seq, 0, which))
    return pl.BlockSpec((tm, D_MODEL), lambda *g: (g[0], which))


def _in_proj(x2, mod, nw, w_main, w_alow, w_gk, b_gk, tm, seq_len):
    t = x2.shape[0]
    return pl.pallas_call(
        _in_kernel,
        grid=(t // tm, D_MAIN // IN_TN),
        in_specs=[pl.BlockSpec((tm, D_MODEL), lambda m, n: (m, 0)),
                  _mod_spec(mod, MOD_SC1, tm, seq_len), _mod_spec(mod, MOD_SH1, tm, seq_len),
                  pl.BlockSpec((1, D_MODEL), lambda m, n: (0, 0)),
                  pl.BlockSpec((D_MODEL, IN_TN), lambda m, n: (0, n)),
                  pl.BlockSpec((D_MODEL, LANES), lambda m, n: (0, 0)),
                  pl.BlockSpec((LANES, GLA_DK), lambda m, n: (0, 0)),
                  pl.BlockSpec((1, GLA_DK), lambda m, n: (0, 0))],
        out_specs=[pl.BlockSpec((tm, IN_TN), lambda m, n: (m, n)),
                   pl.BlockSpec((tm, GLA_DK), lambda m, n: (m, 0))],
        out_shape=[jax.ShapeDtypeStruct((t, D_MAIN), bf16),
                   jax.ShapeDtypeStruct((t, GLA_DK), f32)],
        scratch_shapes=[pltpu.VMEM((tm, D_MODEL), bf16)],
        compiler_params=_cparams(("arbitrary", "arbitrary")),
        name="in_proj",
    )(x2, mod, mod, nw, w_main, w_alow, w_gk, b_gk)


GLA_R = 256


def _gla_norm_gate(o, nw, g):
    gf = g.astype(f32)
    return _rms(o) * nw * (gf * jax.nn.sigmoid(gf))


def _gla_scan_kernel(q_ref, k_ref, v_ref, g_ref, la_ref, nw_ref, s0_ref, o_ref, sout_ref, s_ref):
    step = pl.program_id(1)

    @pl.when(step == 0)
    def _():
        s_ref[...] = s0_ref[...]

    r, c = GLA_R, GLA_CHUNK
    n_chunks = r // c
    row = lax.broadcasted_iota(jnp.int32, (r, r), 0)
    col = lax.broadcasted_iota(jnp.int32, (r, r), 1)
    same_chunk = (row // c) == (col // c)
    causal = same_chunk & (col <= row)

    la = la_ref[...]
    b = _dot_exact(causal.astype(f32), la)
    b_mid = jnp.concatenate(
        [jnp.broadcast_to(b[i * c + c // 2:i * c + c // 2 + 1], (c, GLA_DK)) for i in range(n_chunks)], axis=0)
    b_last = jnp.concatenate(
        [jnp.broadcast_to(b[i * c + c - 1:i * c + c], (c, GLA_DK)) for i in range(n_chunks)], axis=0)
    q = q_ref[...].astype(f32) * (GLA_HK ** -0.5)
    k = k_ref[...].astype(f32)
    qs = (q * jnp.exp(b - b_mid)).astype(bf16)
    ks = (k * jnp.exp(b_mid - b)).astype(bf16)
    qd = (q * jnp.exp(b)).astype(bf16)
    kd_t = (k * jnp.exp(b_last - b)).T.astype(bf16)
    la_t = la.T
    lane = lax.broadcasted_iota(jnp.int32, (GLA_HK, r), 1)
    nw = nw_ref[...]

    for h in range(GLA_HEADS):
        hk = slice(h * GLA_HK, (h + 1) * GLA_HK)
        hv = slice(h * GLA_HV, (h + 1) * GLA_HV)
        v_h = v_ref[:, hv]
        scores = _dot_nt(qs[:, hk], ks[:, hk])
        o_intra = _dot(jnp.where(causal, scores, 0.0).astype(bf16), v_h)
        s = s_ref[h]
        for i in range(n_chunks):
            rows = slice(i * c, (i + 1) * c)
            in_chunk = (lane // c) == i
            o = o_intra[rows] + _dot(qd[rows, hk], s.astype(bf16))
            o_ref[rows, hv] = _gla_norm_gate(o, nw, g_ref[rows, hv]).astype(bf16)
            decay = jnp.exp(jnp.sum(jnp.where(in_chunk, la_t[hk], 0.0), axis=1, keepdims=True))
            s = decay * s + _dot(jnp.where(in_chunk, kd_t[hk], jnp.zeros_like(kd_t[hk])), v_h)
        s_ref[h] = s

    @pl.when(step == pl.num_programs(1) - 1)
    def _():
        sout_ref[...] = s_ref[...]


def _gla_scan(proj, la, nw, s0, n_seq, seq_len):
    steps = seq_len // GLA_R
    t = n_seq * seq_len

    def rows(b, s):
        return b * steps + s

    return pl.pallas_call(
        _gla_scan_kernel,
        grid=(n_seq, steps),
        in_specs=[pl.BlockSpec((GLA_R, GLA_DK), lambda b, s: (rows(b, s), OFF_Q // GLA_DK)),
                  pl.BlockSpec((GLA_R, GLA_DK), lambda b, s: (rows(b, s), OFF_K // GLA_DK)),
                  pl.BlockSpec((GLA_R, GLA_DV), lambda b, s: (rows(b, s), OFF_V // GLA_DV)),
                  pl.BlockSpec((GLA_R, GLA_DV), lambda b, s: (rows(b, s), OFF_G // GLA_DV)),
                  pl.BlockSpec((GLA_R, GLA_DK), lambda b, s: (rows(b, s), 0)),
                  pl.BlockSpec((1, GLA_HV), lambda b, s: (0, 0)),
                  pl.BlockSpec((None, GLA_HEADS, GLA_HK, GLA_HV), lambda b, s: (b, 0, 0, 0))],
        out_specs=[pl.BlockSpec((GLA_R, GLA_DV), lambda b, s: (rows(b, s), 0)),
                   pl.BlockSpec((None, GLA_HEADS, GLA_HK, GLA_HV), lambda b, s: (b, 0, 0, 0))],
        out_shape=[jax.ShapeDtypeStruct((t, GLA_DV), bf16),
                   jax.ShapeDtypeStruct((n_seq, GLA_HEADS, GLA_HK, GLA_HV), f32)],
        scratch_shapes=[pltpu.VMEM((GLA_HEADS, GLA_HK, GLA_HV), f32)],
        compiler_params=_cparams(("arbitrary", "arbitrary")),
        name="gla_scan",
    )(proj, proj, proj, proj, la, nw, s0)


GS_SEQ = 4
GS_ROWS = 128
GS_B = GS_ROWS // GS_SEQ


def _gla_step_kernel(q_ref, k_ref, v_ref, g_ref, la_ref, nw_ref, s0_ref, o_ref, sout_ref,
                     oacc_ref, qd_ref, kdt_ref, lat_ref):
    j = pl.program_id(1)
    r = GS_ROWS
    row = lax.broadcasted_iota(jnp.int32, (r, r), 0)
    col = lax.broadcasted_iota(jnp.int32, (r, r), 1)

    @pl.when(j == 0)
    def _():
        same_seq = (row // GS_SEQ) == (col // GS_SEQ)
        causal = same_seq & (col <= row)
        la = la_ref[...]
        b = _dot_exact(causal.astype(f32), la)
        b_mid = _dot_exact((same_seq & (col % GS_SEQ <= GS_SEQ // 2)).astype(f32), la)
        b_last = _dot_exact(same_seq.astype(f32), la)
        q = q_ref[...].astype(f32) * (GLA_HK ** -0.5)
        k = k_ref[...].astype(f32)
        qs = (q * jnp.exp(b - b_mid)).astype(bf16)
        ks = (k * jnp.exp(b_mid - b)).astype(bf16)
        qd_ref[...] = (q * jnp.exp(b)).astype(bf16)
        kdt_ref[...] = (k * jnp.exp(b_last - b)).T.astype(bf16)
        lat_ref[...] = la.T
        for h in range(GLA_HEADS):
            hk = slice(h * GLA_HK, (h + 1) * GLA_HK)
            hv = slice(h * GLA_HV, (h + 1) * GLA_HV)
            scores = _dot_nt(qs[:, hk], ks[:, hk])
            oacc_ref[:, hv] = _dot(jnp.where(causal, scores, 0.0).astype(bf16), v_ref[:, hv])

    in_seq_lane = (col // GS_SEQ) == j
    in_seq_row = (row[:, :GLA_HK] // GS_SEQ) == j
    for h in range(GLA_HEADS):
        hk = slice(h * GLA_HK, (h + 1) * GLA_HK)
        hv = slice(h * GLA_HV, (h + 1) * GLA_HV)
        s0 = s0_ref[h]
        kd_t = kdt_ref[hk, :]
        ds = _dot(jnp.where(in_seq_lane, kd_t, jnp.zeros_like(kd_t)), v_ref[:, hv])
        decay = jnp.exp(jnp.sum(jnp.where(in_seq_lane, lat_ref[hk, :], 0.0), axis=1, keepdims=True))
        sout_ref[h] = decay * s0 + ds
        qd = qd_ref[:, hk]
        oacc_ref[:, hv] += _dot(jnp.where(in_seq_row, qd, jnp.zeros_like(qd)), s0.astype(bf16))

    @pl.when(j == pl.num_programs(1) - 1)
    def _():
        nw = nw_ref[...]
        for h in range(GLA_HEADS):
            hv = slice(h * GLA_HV, (h + 1) * GLA_HV)
            o_ref[:, hv] = _gla_norm_gate(oacc_ref[:, hv], nw, g_ref[:, hv]).astype(bf16)


def _gla_step(proj, la, nw, s0):
    n_seq = s0.shape[0]
    t = n_seq * GS_SEQ
    groups = t // GS_ROWS
    return pl.pallas_call(
        _gla_step_kernel,
        grid=(groups, GS_B),
        in_specs=[pl.BlockSpec((GS_ROWS, GLA_DK), lambda g, j: (g, OFF_Q // GLA_DK)),
                  pl.BlockSpec((GS_ROWS, GLA_DK), lambda g, j: (g, OFF_K // GLA_DK)),
                  pl.BlockSpec((GS_ROWS, GLA_DV), lambda g, j: (g, OFF_V // GLA_DV)),
                  pl.BlockSpec((GS_ROWS, GLA_DV), lambda g, j: (g, OFF_G // GLA_DV)),
                  pl.BlockSpec((GS_ROWS, GLA_DK), lambda g, j: (g, 0)),
                  pl.BlockSpec((1, GLA_HV), lambda g, j: (0, 0)),
                  pl.BlockSpec((None, GLA_HEADS, GLA_HK, GLA_HV), lambda g, j: (g * GS_B + j, 0, 0, 0))],
        out_specs=[pl.BlockSpec((GS_ROWS, GLA_DV), lambda g, j: (g, 0)),
                   pl.BlockSpec((None, GLA_HEADS, GLA_HK, GLA_HV), lambda g, j: (g * GS_B + j, 0, 0, 0))],
        out_shape=[jax.ShapeDtypeStruct((t, GLA_DV), bf16),
                   jax.ShapeDtypeStruct((n_seq, GLA_HEADS, GLA_HK, GLA_HV), f32)],
        scratch_shapes=[pltpu.VMEM((GS_ROWS, GLA_DV), f32),
                        pltpu.VMEM((GS_ROWS, GLA_DK), bf16),
                        pltpu.VMEM((GLA_DK, GS_ROWS), bf16),
                        pltpu.VMEM((GLA_DK, GS_ROWS), f32)],
        compiler_params=_cparams(("arbitrary", "arbitrary")),
        name="gla_step",
    )(proj, proj, proj, proj, la, nw, s0)


POST_TM = 256
HALO = 16


def _post_kernel(long_seq, tiles_per_seq, *refs):
    if long_seq:
        (x_ref, oa_ref, cb_ref, cc_ref, ch_ref, ga_ref, gb_ref, hcc_ref, hch_ref, cbuf_ref,
         wc_ref, g1_ref, sc_ref, sh_ref, nw_ref, wo_ref, wrh_ref, wrl_ref, br_ref, _hn_alias,
         h_ref, hn_ref, gate_ref, eid_ref, u_ref) = refs
    else:
        (x_ref, oa_ref, cb_ref, cc_ref, ch_ref, ga_ref, gb_ref, p0_ref, p1_ref,
         wc_ref, g1_ref, sc_ref, sh_ref, nw_ref, wo_ref, wrh_ref, wrl_ref, br_ref, _hn_alias,
         h_ref, hn_ref, gate_ref, eid_ref, u_ref) = refs
    tm = x_ref.shape[0]
    u = cc_ref[...].astype(f32) * ch_ref[...].astype(f32)
    row = lax.broadcasted_iota(jnp.int32, (tm, CONV_CH), 0)
    if long_seq:
        pos = row
        first = (pl.program_id(0) % tiles_per_seq) == 0
        halo = hcc_ref[HALO - 2:HALO, :].astype(f32) * hch_ref[HALO - 2:HALO, :].astype(f32)
        cbuf = cbuf_ref[...]
        p0 = jnp.where(first, cbuf[0:1], halo[0:1])
        p1 = jnp.where(first, cbuf[1:2], halo[1:2])
        u_ref[...] = u[tm - 8:tm]
    else:
        pos = row % GS_SEQ
        p0 = p0_ref[...]
        p1 = p1_ref[...]
        u_ref[...] = u
    u1 = jnp.where(pos == 0, p1, pltpu.roll(u, 1, 0))
    u2 = jnp.where(pos == 0, p0, jnp.where(pos == 1, p1, pltpu.roll(u, 2, 0)))
    wc = wc_ref[...]
    conv = wc[0:1] * u2 + wc[1:2] * u1 + wc[2:3] * u
    ob = (cb_ref[...].astype(f32) * conv).astype(bf16)

    ya = _dot(oa_ref[...], wo_ref[0:GLA_DV, :])
    yb = _dot(ob, wo_ref[GLA_DV:GLA_DV + CONV_CH, :])
    y = jax.nn.sigmoid(ga_ref[...].astype(f32)) * ya + jax.nn.sigmoid(gb_ref[...].astype(f32)) * yb
    h = x_ref[...] + g1_ref[...] * y
    h_ref[...] = h
    hn = _rms(h) * nw_ref[...] * (1.0 + sc_ref[...]) + sh_ref[...]
    hn_hi = hn.astype(bf16)
    hn_ref[...] = hn_hi
    hn_lo = (hn - hn_hi.astype(f32)).astype(bf16)
    lg = (_dot(hn_hi, wrh_ref[...]) + _dot(hn_hi, wrl_ref[...]) + _dot(hn_lo, wrh_ref[...])
          + br_ref[...])

    lane = lax.broadcasted_iota(jnp.int32, (tm, LANES), 1)
    lane_f = lane.astype(f32)
    vals, idxs = [], []
    for _ in range(TOP_K):
        m = jnp.max(lg, axis=-1, keepdims=True)
        idx = jnp.min(jnp.where(lg == m, lane_f, float(LANES)), axis=-1, keepdims=True)
        vals.append(m)
        idxs.append(idx)
        lg = jnp.where(lane_f == idx, -jnp.inf, lg)
    exps = [jnp.exp(v - vals[0]) for v in vals]
    den = exps[0] + exps[1] + exps[2] + exps[3]
    gate = jnp.zeros((tm, LANES), f32)
    eid = jnp.zeros((tm, LANES), f32)
    for j in range(TOP_K):
        gate = jnp.where(lane == j, exps[j] / den, gate)
        eid = jnp.where(lane == j, idxs[j], eid)
    gate_ref[...] = gate
    eid_ref[...] = eid.astype(jnp.int32)


def _post(x2, oa, proj, prev, wc, mod, nw, wo, wrh, wrl, br, hn_buf, row0, long_seq, seq_len):
    t = x2.shape[0]
    tm = POST_TM
    n_tiles = t // tm
    blk0 = row0 // tm
    tiles_per_seq = seq_len // tm if long_seq else 1

    def colblk(off, width):
        return pl.BlockSpec((tm, width), lambda m: (m, off // width))

    const = lambda shape: pl.BlockSpec(shape, lambda m: tuple(0 for _ in shape))
    if long_seq:
        halo_rows = lambda m: jnp.maximum(m * (tm // HALO) - 1, 0)
        prev_specs = [pl.BlockSpec((HALO, CONV_CH), lambda m: (halo_rows(m), OFF_CC // CONV_CH)),
                      pl.BlockSpec((HALO, CONV_CH), lambda m: (halo_rows(m), OFF_CH // CONV_CH)),
                      pl.BlockSpec((None, CONV_K - 1, CONV_CH), lambda m: (m // tiles_per_seq, 0, 0))]
        prev_args = [proj, proj, prev]
        u_spec = pl.BlockSpec((None, 8, CONV_CH), lambda m: (m, 0, 0))
        u_shape = jax.ShapeDtypeStruct((n_tiles, 8, CONV_CH), f32)
    else:
        prev_specs = [pl.BlockSpec((tm, CONV_CH), lambda m: (m, 0))] * 2
        prev_args = list(prev)
        u_spec = pl.BlockSpec((tm, CONV_CH), lambda m: (m, 0))
        u_shape = jax.ShapeDtypeStruct((t, CONV_CH), f32)
    n_in = 7 + len(prev_specs) + 10
    return pl.pallas_call(
        functools.partial(_post_kernel, long_seq, tiles_per_seq),
        grid=(n_tiles,),
        in_specs=[pl.BlockSpec((tm, D_MODEL), lambda m: (m, 0)),
                  pl.BlockSpec((tm, GLA_DV), lambda m: (m, 0)),
                  colblk(OFF_CB, CONV_CH), colblk(OFF_CC, CONV_CH), colblk(OFF_CH, CONV_CH),
                  colblk(OFF_GA, D_MODEL), colblk(OFF_GB, D_MODEL)]
                 + prev_specs
                 + [const((CONV_K, CONV_CH)), _mod_spec(mod, MOD_G1, tm, seq_len),
                    _mod_spec(mod, MOD_SC2, tm, seq_len), _mod_spec(mod, MOD_SH2, tm, seq_len), const((1, D_MODEL)),
                    const((GLA_DV + CONV_CH, D_MODEL)), const((D_MODEL, LANES)), const((D_MODEL, LANES)),
                    const((1, LANES)), pl.BlockSpec(memory_space=pl.ANY)],
        out_specs=[pl.BlockSpec((tm, D_MODEL), lambda m: (m, 0)),
                   pl.BlockSpec((tm, D_MODEL), lambda m: (blk0 + m, 0)),
                   pl.BlockSpec((tm, LANES), lambda m: (m, 0)),
                   pl.BlockSpec((tm, LANES), lambda m: (m, 0)),
                   u_spec],
        out_shape=[jax.ShapeDtypeStruct((t, D_MODEL), f32),
                   jax.ShapeDtypeStruct(hn_buf.shape, bf16),
                   jax.ShapeDtypeStruct((t, LANES), f32),
                   jax.ShapeDtypeStruct((t, LANES), jnp.int32),
                   u_shape],
        input_output_aliases={n_in - 1: 1},
        compiler_params=_cparams(("arbitrary",)),
        name="post_mix",
    )(x2, oa, proj, proj, proj, proj, proj, *prev_args, wc, mod, mod, mod, nw, wo, wrh, wrl, br, hn_buf)


MOE_RB = 128
MOE_TM = 1536
MOE_TN = 512
MOE_KB = MOE_TM // MOE_RB
MOE_NT = D_FF // MOE_TN


def _moe_kernel(ie_ref, ib_ref, in_ref, iv_ref, iz_ref,
                x_hbm, wg_ref, wl_ref, wd_ref, bg_ref, bl_ref, bd_ref, out_hbm,
                xbuf, hbuf, obuf, wcast, sem_in, sem_out):
    i = pl.program_id(0)
    s = pl.program_id(1)
    n_items = pl.num_programs(0)
    nb = in_ref[i]

    def x_copy(item, jb, k):
        return pltpu.make_async_copy(
            x_hbm.at[pl.ds((ib_ref[item] + jb) * MOE_RB, MOE_RB), pl.ds(k * MOE_TN, MOE_TN)],
            xbuf.at[k, pl.ds(jb * MOE_RB, MOE_RB)], sem_in)

    def out_copy(item, jb, k):
        return pltpu.make_async_copy(
            obuf.at[k, pl.ds(jb * MOE_RB, MOE_RB)],
            out_hbm.at[pl.ds((ib_ref[item] + jb) * MOE_RB, MOE_RB), pl.ds(k * MOE_TN, MOE_TN)], sem_out)

    def for_blocks(n, fn):
        def body(jb, carry):
            for k in range(MOE_NT):
                fn(jb, k)
            return carry

        lax.fori_loop(0, n, body, 0)

    def for_rows(fn):
        def body(j, carry):
            fn(pl.multiple_of(j * (4 * MOE_RB), 4 * MOE_RB), 4 * MOE_RB)
            return carry

        lax.fori_loop(0, nb // 4, body, 0)
        base = (nb // 4) * (4 * MOE_RB)
        rem = nb % 4

        @pl.when(rem >= 2)
        def _():
            fn(pl.multiple_of(base, MOE_RB), 2 * MOE_RB)

        @pl.when(rem % 2 == 1)
        def _():
            fn(pl.multiple_of(base + (rem // 2) * (2 * MOE_RB), MOE_RB), MOE_RB)

    @pl.when(s == 0)
    def _():
        @pl.when(i == 0)
        def _():
            for_blocks(nb, lambda jb, k: x_copy(i, jb, k).start())

        for_blocks(nb, lambda jb, k: x_copy(i, jb, k).wait())

    @pl.when(s < MOE_NT)
    def _():
        wcast[:, 0:MOE_TN] = wg_ref[...].astype(bf16)
        wcast[:, MOE_TN:2 * MOE_TN] = wl_ref[...].astype(bf16)
        bg = bg_ref[...]
        bl = bl_ref[...]

        def up(r0, size):
            rows = pl.ds(r0, size)
            xb = jnp.concatenate([xbuf[k, rows, :] for k in range(MOE_NT)], axis=1)
            h = _dot(xb, wcast[...])
            hg = jnp.minimum(h[:, 0:MOE_TN] + bg, SWIGLU_LIMIT)
            hl = jnp.clip(h[:, MOE_TN:2 * MOE_TN] + bl, -SWIGLU_LIMIT, SWIGLU_LIMIT)
            hbuf[s, rows, :] = ((hl + 1.0) * hg * jax.nn.sigmoid(SWIGLU_ALPHA * hg)).astype(bf16)

        for_rows(up)

    @pl.when(s == MOE_NT)
    def _():
        @pl.when(i > 0)
        def _():
            prev = jnp.maximum(i - 1, 0)
            for_blocks(in_ref[prev], lambda jb, k: out_copy(prev, jb, k).wait())

        def fill(jb, k):
            obuf[k, pl.ds(pl.multiple_of(jb * MOE_RB, MOE_RB), MOE_RB), :] = jnp.zeros((MOE_RB, MOE_TN), bf16)
            out_copy(i, jb, k).start()

        nz = iz_ref[i]
        for_blocks(nz, fill)
        for_blocks(nz, lambda jb, k: out_copy(i, jb, k).wait())

        @pl.when(i + 1 < n_items)
        def _():
            nxt = jnp.minimum(i + 1, n_items - 1)
            for_blocks(in_ref[nxt], lambda jb, k: x_copy(nxt, jb, k).start())

    @pl.when(s >= MOE_NT)
    def _():
        wcast[:, 0:MOE_TN] = wd_ref[...].astype(bf16)
        bd = bd_ref[...]

        def down(r0, size):
            rows = pl.ds(r0, size)
            hb = jnp.concatenate([hbuf[k, rows, :] for k in range(MOE_NT)], axis=1)
            obuf[s - MOE_NT, rows, :] = (_dot(hb, wcast[:, 0:MOE_TN]) + bd).astype(bf16)

        for_rows(down)

    @pl.when(s == 2 * MOE_NT - 1)
    def _():
        for_blocks(nb, lambda jb, k: out_copy(i, jb, k).start())

        @pl.when(i == n_items - 1)
        def _():
            for_blocks(nb, lambda jb, k: out_copy(i, jb, k).wait())


def _moe(xs, items, w_up, b_up, w_down, b_down):
    rows = xs.shape[0]
    n_items = items[0].shape[0]

    last = MOE_NT - 1

    def up_tile(i, s, tabs):
        valid = tabs[3][i]
        return jnp.minimum(s, last) * valid + last * (1 - valid)

    def down_tile(i, s, tabs):
        valid = tabs[3][i]
        return jnp.maximum(s - MOE_NT, 0) * valid + last * (1 - valid)

    def expert(i, tabs):
        return tabs[0][i]

    grid_spec = pltpu.PrefetchScalarGridSpec(
        num_scalar_prefetch=len(items),
        grid=(n_items, 2 * MOE_NT),
        in_specs=[pl.BlockSpec(memory_space=pl.ANY),
                  pl.BlockSpec((None, D_MODEL, MOE_TN), lambda i, s, *t: (expert(i, t), 0, up_tile(i, s, t))),
                  pl.BlockSpec((None, D_MODEL, MOE_TN), lambda i, s, *t: (expert(i, t), 0, MOE_NT + up_tile(i, s, t))),
                  pl.BlockSpec((None, D_FF, MOE_TN), lambda i, s, *t: (expert(i, t), 0, down_tile(i, s, t))),
                  pl.BlockSpec((None, 1, MOE_TN), lambda i, s, *t: (expert(i, t), 0, up_tile(i, s, t))),
                  pl.BlockSpec((None, 1, MOE_TN), lambda i, s, *t: (expert(i, t), 0, MOE_NT + up_tile(i, s, t))),
                  pl.BlockSpec((None, 1, MOE_TN), lambda i, s, *t: (expert(i, t), 0, down_tile(i, s, t)))],
        out_specs=pl.BlockSpec(memory_space=pl.ANY),
        scratch_shapes=[pltpu.VMEM((MOE_NT, MOE_TM, MOE_TN), bf16),
                        pltpu.VMEM((MOE_NT, MOE_TM, MOE_TN), bf16),
                        pltpu.VMEM((MOE_NT, MOE_TM, MOE_TN), bf16),
                        pltpu.VMEM((D_MODEL, 2 * MOE_TN), bf16),
                        pltpu.SemaphoreType.DMA(()),
                        pltpu.SemaphoreType.DMA(())],
    )
    return pl.pallas_call(
        _moe_kernel,
        grid_spec=grid_spec,
        out_shape=jax.ShapeDtypeStruct((rows, D_MODEL), bf16),
        compiler_params=_cparams(("arbitrary", "arbitrary")),
        name="moe_experts",
    )(*items,
      xs, w_up, w_up, w_down,
      b_up.reshape(N_EXPERTS, 1, 2 * D_FF), b_up.reshape(N_EXPERTS, 1, 2 * D_FF),
      b_down.reshape(N_EXPERTS, 1, D_MODEL))


FIN_TM = 256


def _final_kernel(h_ref, og_ref, gate_ref, g2_ref, fw_ref, y_ref):
    gate = gate_ref[...]
    ff = og_ref[0].astype(f32) * gate[:, 0:1]
    for j in range(1, TOP_K):
        ff = ff + og_ref[j].astype(f32) * gate[:, j:j + 1]
    h = h_ref[...] + g2_ref[...] * ff
    y_ref[...] = _rms(h) * fw_ref[...]


def _final(h, og, gate, mod, fw, row0, seq_len):
    t = h.shape[0]
    tm = FIN_TM
    blk0 = row0 // tm
    return pl.pallas_call(
        _final_kernel,
        grid=(t // tm,),
        in_specs=[pl.BlockSpec((tm, D_MODEL), lambda m: (m, 0)),
                  pl.BlockSpec((TOP_K, tm, D_MODEL), lambda m: (0, blk0 + m, 0)),
                  pl.BlockSpec((tm, LANES), lambda m: (blk0 + m, 0)),
                  _mod_spec(mod, MOD_G2, tm, seq_len),
                  pl.BlockSpec((1, D_MODEL), lambda m: (0, 0))],
        out_specs=pl.BlockSpec((tm, D_MODEL), lambda m: (m, 0)),
        out_shape=jax.ShapeDtypeStruct((t, D_MODEL), f32),
        compiler_params=_cparams(("arbitrary",)),
        name="final_norm",
    )(h, og, gate, mod, fw)


def _count_le(bounds, x):
    return jnp.sum((bounds[None, :] <= x[:, None]).astype(jnp.int32), axis=1)


def _routing(top_e):
    t = top_e.shape[0]
    n_pairs = t * TOP_K
    e_flat = top_e.reshape(-1)
    onehot = (e_flat[:, None] == jnp.arange(N_EXPERTS, dtype=e_flat.dtype)[None, :]).astype(jnp.int32)
    csum = jnp.cumsum(onehot, axis=0)
    rank = jnp.sum(onehot * csum, axis=1) - 1
    counts = csum[-1]
    nblk = (counts + MOE_RB - 1) // MOE_RB
    blk_end = jnp.cumsum(nblk)
    blk_start = blk_end - nblk
    dest = blk_start[e_flat] * MOE_RB + rank
    n_rows = (n_pairs // MOE_RB + N_EXPERTS) * MOE_RB
    order = jnp.argsort(e_flat, stable=True).astype(jnp.int32)
    raw_start = jnp.cumsum(counts) - counts
    r = jnp.arange(n_rows, dtype=jnp.int32)
    e_row = jnp.minimum(_count_le(blk_end * MOE_RB, r), N_EXPERTS - 1)
    off = r - blk_start[e_row] * MOE_RB
    src_pair = order[jnp.clip(raw_start[e_row] + off, 0, n_pairs - 1)]
    src_tok = jnp.where(off < counts[e_row], src_pair // TOP_K, r % t).astype(jnp.int32)

    n_items = N_EXPERTS + n_rows // MOE_TM
    nit = (nblk + MOE_KB - 1) // MOE_KB
    it_end = jnp.cumsum(nit)
    it_start = it_end - nit
    idx = jnp.arange(n_items, dtype=jnp.int32)
    valid = idx < it_end[-1]
    last_valid = jnp.maximum(it_end[-1] - 1, 0)
    idx_c = jnp.minimum(idx, last_valid)
    e_of = jnp.minimum(_count_le(it_end, idx_c), N_EXPERTS - 1).astype(jnp.int32)
    k = idx_c - it_start[e_of]
    item_nb = jnp.where(valid, jnp.clip(nblk[e_of] - k * MOE_KB, 0, MOE_KB), 0).astype(jnp.int32)
    spare = idx - it_end[-1]
    tail_blk = blk_end[-1] + spare * MOE_KB
    item_nz = jnp.where(valid, 0, jnp.clip(n_rows // MOE_RB - tail_blk, 0, MOE_KB)).astype(jnp.int32)
    item_blk = jnp.where(valid, blk_start[e_of] + k * MOE_KB, tail_blk).astype(jnp.int32)
    items = (e_of, item_blk, item_nb, valid.astype(jnp.int32), item_nz)
    return dest.reshape(t, TOP_K), src_tok, items


def kernel(x_prompt, x_sample, state_gla, state_conv, c_prompt, c_sample, w_ada, b_ada, norm1_w,
           w_in, w_gk_up, b_gk, gla_norm_w, w_conv, w_out, norm2_w, w_router, b_router, w_up, b_up,
           w_down, b_down, final_norm_w):
    n_p, seq_p, _ = x_prompt.shape
    n_s, seq_s, _ = x_sample.shape
    t_p, t_s = n_p * seq_p, n_s * seq_s
    assert seq_s == GS_SEQ and w_ada.shape[0] == 1

    c_all = jnp.concatenate([c_prompt, c_sample], axis=0)
    pad = (-c_all.shape[0]) % 16
    c_all = jnp.pad(c_all, ((0, pad), (0, 0)))
    mod = _ada(c_all, w_ada[0], b_ada[0])
    mod_p = mod[:n_p].reshape(n_p, 1, N_MOD * D_MODEL)
    mod_s = jnp.repeat(mod[n_p:n_p + n_s], seq_s, axis=0)

    w_main, w_alow = _prep_w_in(w_in[0])
    w_gk = jnp.pad(w_gk_up[0], ((0, LANES - GLA_LOW_RANK), (0, 0))).astype(bf16)
    bgk = b_gk[0].reshape(1, GLA_DK)
    wo = w_out[0].astype(bf16)
    wr = jnp.pad(w_router[0], ((0, 0), (0, LANES - N_EXPERTS)))
    wr_hi = wr.astype(bf16)
    wr_lo = (wr - wr_hi.astype(f32)).astype(bf16)
    br = jnp.pad(b_router[0], (0, LANES - N_EXPERTS), constant_values=ROUTER_PAD).reshape(1, LANES)
    n1w = norm1_w[0].reshape(1, D_MODEL)
    n2w = norm2_w[0].reshape(1, D_MODEL)
    gnw = gla_norm_w[0].reshape(1, GLA_HV)
    fw = final_norm_w.reshape(1, D_MODEL)
    wc = w_conv[0]

    xp = x_prompt.reshape(t_p, D_MODEL)
    xs_ = x_sample.reshape(t_s, D_MODEL)

    proj_p, la_p = _in_proj(xp, mod_p, n1w, w_main, w_alow, w_gk, bgk, 1024, seq_p)
    proj_s, la_s = _in_proj(xs_, mod_s, n1w, w_main, w_alow, w_gk, bgk, t_s, seq_s)

    gla0 = jnp.zeros((n_p, GLA_HEADS, GLA_HK, GLA_HV), f32)
    conv0 = jnp.zeros((n_p, CONV_K - 1, CONV_CH), f32)
    oa_p, gla_p = _gla_scan(proj_p, la_p, gnw, gla0, n_p, seq_p)
    oa_s, gla_s = _gla_step(proj_s, la_s, gnw, state_gla[0])

    hn = jnp.zeros((2 * (t_p + t_s), D_MODEL), bf16)
    h_p, hn, gate_p, eid_p, ut_p = _post(xp, oa_p, proj_p, conv0, wc, mod_p, n2w, wo,
                                         wr_hi, wr_lo, br, hn, 0, True, seq_p)
    prev_s = (jnp.repeat(state_conv[0][:, 0], seq_s, axis=0), jnp.repeat(state_conv[0][:, 1], seq_s, axis=0))
    h_s, hn, gate_s, eid_s, u_s = _post(xs_, oa_s, proj_s, prev_s, wc, mod_s, n2w, wo,
                                        wr_hi, wr_lo, br, hn, t_p, False, seq_s)
    conv_p = ut_p.reshape(n_p, seq_p // POST_TM, 8, CONV_CH)[:, -1, 8 - (CONV_K - 1):]
    conv_s = u_s.reshape(n_s, seq_s, CONV_CH)[:, seq_s - (CONV_K - 1):]

    gate = jnp.concatenate([gate_p, gate_s], axis=0)
    top_e = jnp.concatenate([eid_p, eid_s], axis=0)[:, :TOP_K]
    dest, src_tok, items = _routing(top_e)
    x_rows = hn[src_tok]
    out_rows = _moe(x_rows, items, w_up[0], b_up[0], w_down[0], b_down[0])
    og = out_rows[dest.T.reshape(-1)].reshape(TOP_K, t_p + t_s, D_MODEL)

    y_p = _final(h_p, og, gate, mod_p, fw, 0, seq_p)
    y_s = _final(h_s, og, gate, mod_s, fw, t_p, seq_s)

    return (y_p.reshape(n_p, seq_p, D_MODEL), y_s.reshape(n_s, seq_s, D_MODEL),
            gla_p[None], conv_p[None], gla_s[None], conv_s[None])
```

```python
import functools

import jax
import jax.numpy as jnp
from jax import lax
from jax.experimental import pallas as pl
from jax.experimental.pallas import tpu as pltpu

f32 = jnp.float32
bf16 = jnp.bfloat16

D_MODEL = 2048
N_MOD = 6
GLA_HEADS = 4
GLA_DK = 512
GLA_DV = 1024
GLA_HK = 128
GLA_HV = 256
GLA_LOW_RANK = 16
GLA_TAU = 16.0
GLA_CHUNK = 64
CONV_CH = 1024
CONV_K = 3
N_EXPERTS = 32
TOP_K = 4
D_FF = 2048
SWIGLU_LIMIT = 7.0
SWIGLU_ALPHA = 1.702
EPS = 1e-6

LANES = 128
D_MAIN = 10240
VMEM_LIMIT = 56 * 1024 * 1024

OFF_Q, OFF_K, OFF_V, OFF_G = 0, 512, 1024, 2048
OFF_CB, OFF_CC, OFF_CH, OFF_GA, OFF_GB = 3072, 4096, 5120, 6144, 8192

ROUTER_PAD = -1e30

HIGHEST = lax.Precision.HIGHEST


def _cparams(sem):
    return pltpu.CompilerParams(dimension_semantics=sem, vmem_limit_bytes=VMEM_LIMIT)


def _dot(a, b):
    return jnp.dot(a, b, preferred_element_type=f32)


def _dot_nt(a, b):
    return lax.dot_general(a, b, (((1,), (1,)), ((), ())), preferred_element_type=f32)


def _dot_exact(a, b):
    return jnp.dot(a, b, precision=HIGHEST, preferred_element_type=f32)


def _rms(x):
    return x * lax.rsqrt(jnp.mean(x * x, axis=-1, keepdims=True) + EPS)


ADA_TN = 1024


def _ada_kernel(c_ref, w_ref, b_ref, o_ref):
    c = c_ref[...]
    s = (c * jax.nn.sigmoid(c)).astype(bf16)
    o_ref[...] = _dot(s, w_ref[...].astype(bf16)) + b_ref[...]


def _ada(c, w, b):
    rows = c.shape[0]
    n = w.shape[1]
    return pl.pallas_call(
        _ada_kernel,
        grid=(n // ADA_TN,),
        in_specs=[pl.BlockSpec((rows, D_MODEL), lambda j: (0, 0)),
                  pl.BlockSpec((D_MODEL, ADA_TN), lambda j: (0, j)),
                  pl.BlockSpec((1, ADA_TN), lambda j: (0, j))],
        out_specs=pl.BlockSpec((rows, ADA_TN), lambda j: (0, j)),
        out_shape=jax.ShapeDtypeStruct((rows, n), f32),
        compiler_params=_cparams(("arbitrary",)),
        name="adaln",
    )(c, w, b.reshape(1, n))


PREP_TN = 1024


def _prep_kernel(a_ref, b_ref, low_ref, main_ref, alow_ref):
    j = pl.program_id(0)
    first_shifted = OFF_G // PREP_TN

    @pl.when(j < first_shifted)
    def _():
        main_ref[...] = a_ref[...].T.astype(bf16)

    @pl.when(j >= first_shifted)
    def _():
        wide = jnp.concatenate([a_ref[...], b_ref[...]], axis=0)
        main_ref[...] = wide[GLA_LOW_RANK:GLA_LOW_RANK + PREP_TN].T.astype(bf16)

    @pl.when(j == 0)
    def _():
        low = low_ref[...].T
        lane = lax.broadcasted_iota(jnp.int32, low.shape, 1)
        alow_ref[...] = jnp.where(lane < GLA_LOW_RANK, low, 0.0).astype(bf16)


def _prep_w_in(wt):
    d = wt.shape[1]
    return pl.pallas_call(
        _prep_kernel,
        grid=(D_MAIN // PREP_TN,),
        in_specs=[pl.BlockSpec((PREP_TN, d), lambda j: (j, 0)),
                  pl.BlockSpec((GLA_LOW_RANK, d), lambda j: ((j + 1) * (PREP_TN // GLA_LOW_RANK), 0)),
                  pl.BlockSpec((LANES, d), lambda j: (OFF_G // LANES, 0))],
        out_specs=[pl.BlockSpec((d, PREP_TN), lambda j: (0, j)),
                   pl.BlockSpec((d, LANES), lambda j: (0, 0))],
        out_shape=[jax.ShapeDtypeStruct((d, D_MAIN), bf16),
                   jax.ShapeDtypeStruct((d, LANES), bf16)],
        compiler_params=_cparams(("arbitrary",)),
        name="prep_w_in",
    )(wt, wt, wt)


IN_TN = 1024


def _in_kernel(x_ref, sc_ref, sh_ref, nw_ref, w_ref, wa_ref, wgk_ref, bgk_ref,
               proj_ref, la_ref, xn_ref):
    @pl.when(pl.program_id(1) == 0)
    def _():
        xn = _rms(x_ref[...]) * nw_ref[...] * (1.0 + sc_ref[...]) + sh_ref[...]
        xnb = xn.astype(bf16)
        xn_ref[...] = xnb
        a_low = _dot(xnb, wa_ref[...])
        z = _dot(a_low.astype(bf16), wgk_ref[...]) + bgk_ref[...]
        la_ref[...] = (jnp.minimum(z, 0.0) - jnp.log(1.0 + jnp.exp(-jnp.abs(z)))) * (1.0 / GLA_TAU)

    proj_ref[...] = _dot(xn_ref[...], w_ref[...]).astype(bf16)


MOD_SH1, MOD_SC1, MOD_G1, MOD_SH2, MOD_SC2, MOD_G2 = range(N_MOD)


def _mod_spec(mod, which, tm, seq_len):
    if mod.ndim == 3:
        tiles_per_seq = seq_len // tm
        return pl.BlockSpec((None, 1, D_MODEL), lambda *g: (g[0] // tiles_per_seq, 0, which))
    return pl.BlockSpec((tm, D_MODEL), lambda *g: (g[0], which))


def _in_proj(x2, mod, nw, w_main, w_alow, w_gk, b_gk, tm, seq_len):
    t = x2.shape[0]
    return pl.pallas_call(
        _in_kernel,
        grid=(t // tm, D_MAIN // IN_TN),
        in_specs=[pl.BlockSpec((tm, D_MODEL), lambda m, n: (m, 0)),
                  _mod_spec(mod, MOD_SC1, tm, seq_len), _mod_spec(mod, MOD_SH1, tm, seq_len),
                  pl.BlockSpec((1, D_MODEL), lambda m, n: (0, 0)),
                  pl.BlockSpec((D_MODEL, IN_TN), lambda m, n: (0, n)),
                  pl.BlockSpec((D_MODEL, LANES), lambda m, n: (0, 0)),
                  pl.BlockSpec((LANES, GLA_DK), lambda m, n: (0, 0)),
                  pl.BlockSpec((1, GLA_DK), lambda m, n: (0, 0))],
        out_specs=[pl.BlockSpec((tm, IN_TN), lambda m, n: (m, n)),
                   pl.BlockSpec((tm, GLA_DK), lambda m, n: (m, 0))],
        out_shape=[jax.ShapeDtypeStruct((t, D_MAIN), bf16),
                   jax.ShapeDtypeStruct((t, GLA_DK), f32)],
        scratch_shapes=[pltpu.VMEM((tm, D_MODEL), bf16)],
        compiler_params=_cparams(("arbitrary", "arbitrary")),
        name="in_proj",
    )(x2, mod, mod, nw, w_main, w_alow, w_gk, b_gk)


GLA_R = 256


def _gla_norm_gate(o, nw, g):
    gf = g.astype(f32)
    return _rms(o) * nw * (gf * jax.nn.sigmoid(gf))


def _gla_scan_kernel(q_ref, k_ref, v_ref, g_ref, la_ref, nw_ref, s0_ref, o_ref, sout_ref, s_ref):
    step = pl.program_id(1)

    @pl.when(step == 0)
    def _():
        s_ref[...] = s0_ref[...]

    r, c = GLA_R, GLA_CHUNK
    n_chunks = r // c
    row = lax.broadcasted_iota(jnp.int32, (r, r), 0)
    col = lax.broadcasted_iota(jnp.int32, (r, r), 1)
    same_chunk = (row // c) == (col // c)
    causal = same_chunk & (col <= row)

    la = la_ref[...]
    b = _dot_exact(causal.astype(f32), la)
    b_mid = jnp.concatenate(
        [jnp.broadcast_to(b[i * c + c // 2:i * c + c // 2 + 1], (c, GLA_DK)) for i in range(n_chunks)], axis=0)
    b_last = jnp.concatenate(
        [jnp.broadcast_to(b[i * c + c - 1:i * c + c], (c, GLA_DK)) for i in range(n_chunks)], axis=0)
    q = q_ref[...].astype(f32) * (GLA_HK ** -0.5)
    k = k_ref[...].astype(f32)
    qs = (q * jnp.exp(b - b_mid)).astype(bf16)
    ks = (k * jnp.exp(b_mid - b)).astype(bf16)
    qd = (q * jnp.exp(b)).astype(bf16)
    kd_t = (k * jnp.exp(b_last - b)).T.astype(bf16)
    la_t = la.T
    lane = lax.broadcasted_iota(jnp.int32, (GLA_HK, r), 1)
    nw = nw_ref[...]

    for h in range(GLA_HEADS):
        hk = slice(h * GLA_HK, (h + 1) * GLA_HK)
        hv = slice(h * GLA_HV, (h + 1) * GLA_HV)
        v_h = v_ref[:, hv]
        scores = _dot_nt(qs[:, hk], ks[:, hk])
        o_intra = _dot(jnp.where(causal, scores, 0.0).astype(bf16), v_h)
        s = s_ref[h]
        for i in range(n_chunks):
            rows = slice(i * c, (i + 1) * c)
            in_chunk = (lane // c) == i
            o = o_intra[rows] + _dot(qd[rows, hk], s.astype(bf16))
            o_ref[rows, hv] = _gla_norm_gate(o, nw, g_ref[rows, hv]).astype(bf16)
            decay = jnp.exp(jnp.sum(jnp.where(in_chunk, la_t[hk], 0.0), axis=1, keepdims=True))
            s = decay * s + _dot(jnp.where(in_chunk, kd_t[hk], jnp.zeros_like(kd_t[hk])), v_h)
        s_ref[h] = s

    @pl.when(step == pl.num_programs(1) - 1)
    def _():
        sout_ref[...] = s_ref[...]


def _gla_scan(proj, la, nw, s0, n_seq, seq_len):
    steps = seq_len // GLA_R
    t = n_seq * seq_len

    def rows(b, s):
        return b * steps + s

    return pl.pallas_call(
        _gla_scan_kernel,
        grid=(n_seq, steps),
        in_specs=[pl.BlockSpec((GLA_R, GLA_DK), lambda b, s: (rows(b, s), OFF_Q // GLA_DK)),
                  pl.BlockSpec((GLA_R, GLA_DK), lambda b, s: (rows(b, s), OFF_K // GLA_DK)),
                  pl.BlockSpec((GLA_R, GLA_DV), lambda b, s: (rows(b, s), OFF_V // GLA_DV)),
                  pl.BlockSpec((GLA_R, GLA_DV), lambda b, s: (rows(b, s), OFF_G // GLA_DV)),
                  pl.BlockSpec((GLA_R, GLA_DK), lambda b, s: (rows(b, s), 0)),
                  pl.BlockSpec((1, GLA_HV), lambda b, s: (0, 0)),
                  pl.BlockSpec((None, GLA_HEADS, GLA_HK, GLA_HV), lambda b, s: (b, 0, 0, 0))],
        out_specs=[pl.BlockSpec((GLA_R, GLA_DV), lambda b, s: (rows(b, s), 0)),
                   pl.BlockSpec((None, GLA_HEADS, GLA_HK, GLA_HV), lambda b, s: (b, 0, 0, 0))],
        out_shape=[jax.ShapeDtypeStruct((t, GLA_DV), bf16),
                   jax.ShapeDtypeStruct((n_seq, GLA_HEADS, GLA_HK, GLA_HV), f32)],
        scratch_shapes=[pltpu.VMEM((GLA_HEADS, GLA_HK, GLA_HV), f32)],
        compiler_params=_cparams(("arbitrary", "arbitrary")),
        name="gla_scan",
    )(proj, proj, proj, proj, la, nw, s0)


GS_SEQ = 4
GS_ROWS = 128
GS_B = GS_ROWS // GS_SEQ


def _gla_step_kernel(q_ref, k_ref, v_ref, g_ref, la_ref, nw_ref, s0_ref, o_ref, sout_ref,
                     oacc_ref, qd_ref, kdt_ref, lat_ref):
    j = pl.program_id(1)
    r = GS_ROWS
    row = lax.broadcasted_iota(jnp.int32, (r, r), 0)
    col = lax.broadcasted_iota(jnp.int32, (r, r), 1)

    @pl.when(j == 0)
    def _():
        same_seq = (row // GS_SEQ) == (col // GS_SEQ)
        causal = same_seq & (col <= row)
        la = la_ref[...]
        b = _dot_exact(causal.astype(f32), la)
        b_mid = _dot_exact((same_seq & (col % GS_SEQ <= GS_SEQ // 2)).astype(f32), la)
        b_last = _dot_exact(same_seq.astype(f32), la)
        q = q_ref[...].astype(f32) * (GLA_HK ** -0.5)
        k = k_ref[...].astype(f32)
        qs = (q * jnp.exp(b - b_mid)).astype(bf16)
        ks = (k * jnp.exp(b_mid - b)).astype(bf16)
        qd_ref[...] = (q * jnp.exp(b)).astype(bf16)
        kdt_ref[...] = (k * jnp.exp(b_last - b)).T.astype(bf16)
        lat_ref[...] = la.T
        for h in range(GLA_HEADS):
            hk = slice(h * GLA_HK, (h + 1) * GLA_HK)
            hv = slice(h * GLA_HV, (h + 1) * GLA_HV)
            scores = _dot_nt(qs[:, hk], ks[:, hk])
            oacc_ref[:, hv] = _dot(jnp.where(causal, scores, 0.0).astype(bf16), v_ref[:, hv])

    in_seq_lane = (col // GS_SEQ) == j
    in_seq_row = (row[:, :GLA_HK] // GS_SEQ) == j
    for h in range(GLA_HEADS):
        hk = slice(h * GLA_HK, (h + 1) * GLA_HK)
        hv = slice(h * GLA_HV, (h + 1) * GLA_HV)
        s0 = s0_ref[h]
        kd_t = kdt_ref[hk, :]
        ds = _dot(jnp.where(in_seq_lane, kd_t, jnp.zeros_like(kd_t)), v_ref[:, hv])
        decay = jnp.exp(jnp.sum(jnp.where(in_seq_lane, lat_ref[hk, :], 0.0), axis=1, keepdims=True))
        sout_ref[h] = decay * s0 + ds
        qd = qd_ref[:, hk]
        oacc_ref[:, hv] += _dot(jnp.where(in_seq_row, qd, jnp.zeros_like(qd)), s0.astype(bf16))

    @pl.when(j == pl.num_programs(1) - 1)
    def _():
        nw = nw_ref[...]
        for h in range(GLA_HEADS):
            hv = slice(h * GLA_HV, (h + 1) * GLA_HV)
            o_ref[:, hv] = _gla_norm_gate(oacc_ref[:, hv], nw, g_ref[:, hv]).astype(bf16)


def _gla_step(proj, la, nw, s0):
    n_seq = s0.shape[0]
    t = n_seq * GS_SEQ
    groups = t // GS_ROWS
    return pl.pallas_call(
        _gla_step_kernel,
        grid=(groups, GS_B),
        in_specs=[pl.BlockSpec((GS_ROWS, GLA_DK), lambda g, j: (g, OFF_Q // GLA_DK)),
                  pl.BlockSpec((GS_ROWS, GLA_DK), lambda g, j: (g, OFF_K // GLA_DK)),
                  pl.BlockSpec((GS_ROWS, GLA_DV), lambda g, j: (g, OFF_V // GLA_DV)),
                  pl.BlockSpec((GS_ROWS, GLA_DV), lambda g, j: (g, OFF_G // GLA_DV)),
                  pl.BlockSpec((GS_ROWS, GLA_DK), lambda g, j: (g, 0)),
                  pl.BlockSpec((1, GLA_HV), lambda g, j: (0, 0)),
                  pl.BlockSpec((None, GLA_HEADS, GLA_HK, GLA_HV), lambda g, j: (g * GS_B + j, 0, 0, 0))],
        out_specs=[pl.BlockSpec((GS_ROWS, GLA_DV), lambda g, j: (g, 0)),
                   pl.BlockSpec((None, GLA_HEADS, GLA_HK, GLA_HV), lambda g, j: (g * GS_B + j, 0, 0, 0))],
        out_shape=[jax.ShapeDtypeStruct((t, GLA_DV), bf16),
                   jax.ShapeDtypeStruct((n_seq, GLA_HEADS, GLA_HK, GLA_HV), f32)],
        scratch_shapes=[pltpu.VMEM((GS_ROWS, GLA_DV), f32),
                        pltpu.VMEM((GS_ROWS, GLA_DK), bf16),
                        pltpu.VMEM((GLA_DK, GS_ROWS), bf16),
                        pltpu.VMEM((GLA_DK, GS_ROWS), f32)],
        compiler_params=_cparams(("arbitrary", "arbitrary")),
        name="gla_step",
    )(proj, proj, proj, proj, la, nw, s0)


POST_TM = 256
HALO = 16


def _post_kernel(long_seq, tiles_per_seq, *refs):
    if long_seq:
        (x_ref, oa_ref, cb_ref, cc_ref, ch_ref, ga_ref, gb_ref, hcc_ref, hch_ref, cbuf_ref,
         wc_ref, g1_ref, sc_ref, sh_ref, nw_ref, wo_ref, wrh_ref, wrl_ref, br_ref, _hn_alias,
         h_ref, hn_ref, gate_ref, eid_ref, u_ref) = refs
    else:
        (x_ref, oa_ref, cb_ref, cc_ref, ch_ref, ga_ref, gb_ref, p0_ref, p1_ref,
         wc_ref, g1_ref, sc_ref, sh_ref, nw_ref, wo_ref, wrh_ref, wrl_ref, br_ref, _hn_alias,
         h_ref, hn_ref, gate_ref, eid_ref, u_ref) = refs
    tm = x_ref.shape[0]
    u = cc_ref[...].astype(f32) * ch_ref[...].astype(f32)
    row = lax.broadcasted_iota(jnp.int32, (tm, CONV_CH), 0)
    if long_seq:
        pos = row
        first = (pl.program_id(0) % tiles_per_seq) == 0
        halo = hcc_ref[HALO - 2:HALO, :].astype(f32) * hch_ref[HALO - 2:HALO, :].astype(f32)
        cbuf = cbuf_ref[...]
        p0 = jnp.where(first, cbuf[0:1], halo[0:1])
        p1 = jnp.where(first, cbuf[1:2], halo[1:2])
        u_ref[...] = u[tm - 8:tm]
    else:
        pos = row % GS_SEQ
        p0 = p0_ref[...]
        p1 = p1_ref[...]
        u_ref[...] = u
    u1 = jnp.where(pos == 0, p1, pltpu.roll(u, 1, 0))
    u2 = jnp.where(pos == 0, p0, jnp.where(pos == 1, p1, pltpu.roll(u, 2, 0)))
    wc = wc_ref[...]
    conv = wc[0:1] * u2 + wc[1:2] * u1 + wc[2:3] * u
    ob = (cb_ref[...].astype(f32) * conv).astype(bf16)

    ya = _dot(oa_ref[...], wo_ref[0:GLA_DV, :])
    yb = _dot(ob, wo_ref[GLA_DV:GLA_DV + CONV_CH, :])
    y = jax.nn.sigmoid(ga_ref[...].astype(f32)) * ya + jax.nn.sigmoid(gb_ref[...].astype(f32)) * yb
    h = x_ref[...] + g1_ref[...] * y
    h_ref[...] = h
    hn = _rms(h) * nw_ref[...] * (1.0 + sc_ref[...]) + sh_ref[...]
    hn_hi = hn.astype(bf16)
    hn_ref[...] = hn_hi
    hn_lo = (hn - hn_hi.astype(f32)).astype(bf16)
    lg = (_dot(hn_hi, wrh_ref[...]) + _dot(hn_hi, wrl_ref[...]) + _dot(hn_lo, wrh_ref[...])
          + br_ref[...])

    lane = lax.broadcasted_iota(jnp.int32, (tm, LANES), 1)
    lane_f = lane.astype(f32)
    vals, idxs = [], []
    for _ in range(TOP_K):
        m = jnp.max(lg, axis=-1, keepdims=True)
        idx = jnp.min(jnp.where(lg == m, lane_f, float(LANES)), axis=-1, keepdims=True)
        vals.append(m)
        idxs.append(idx)
        lg = jnp.where(lane_f == idx, -jnp.inf, lg)
    exps = [jnp.exp(v - vals[0]) for v in vals]
    den = exps[0] + exps[1] + exps[2] + exps[3]
    gate = jnp.zeros((tm, LANES), f32)
    eid = jnp.zeros((tm, LANES), f32)
    for j in range(TOP_K):
        gate = jnp.where(lane == j, exps[j] / den, gate)
        eid = jnp.where(lane == j, idxs[j], eid)
    gate_ref[...] = gate
    eid_ref[...] = eid.astype(jnp.int32)


def _post(x2, oa, proj, prev, wc, mod, nw, wo, wrh, wrl, br, hn_buf, row0, long_seq, seq_len):
    t = x2.shape[0]
    tm = POST_TM
    n_tiles = t // tm
    blk0 = row0 // tm
    tiles_per_seq = seq_len // tm if long_seq else 1

    def colblk(off, width):
        return pl.BlockSpec((tm, width), lambda m: (m, off // width))

    const = lambda shape: pl.BlockSpec(shape, lambda m: tuple(0 for _ in shape))
    if long_seq:
        halo_rows = lambda m: jnp.maximum(m * (tm // HALO) - 1, 0)
        prev_specs = [pl.BlockSpec((HALO, CONV_CH), lambda m: (halo_rows(m), OFF_CC // CONV_CH)),
                      pl.BlockSpec((HALO, CONV_CH), lambda m: (halo_rows(m), OFF_CH // CONV_CH)),
                      pl.BlockSpec((None, CONV_K - 1, CONV_CH), lambda m: (m // tiles_per_seq, 0, 0))]
        prev_args = [proj, proj, prev]
        u_spec = pl.BlockSpec((None, 8, CONV_CH), lambda m: (m, 0, 0))
        u_shape = jax.ShapeDtypeStruct((n_tiles, 8, CONV_CH), f32)
    else:
        prev_specs = [pl.BlockSpec((tm, CONV_CH), lambda m: (m, 0))] * 2
        prev_args = list(prev)
        u_spec = pl.BlockSpec((tm, CONV_CH), lambda m: (m, 0))
        u_shape = jax.ShapeDtypeStruct((t, CONV_CH), f32)
    n_in = 7 + len(prev_specs) + 10
    return pl.pallas_call(
        functools.partial(_post_kernel, long_seq, tiles_per_seq),
        grid=(n_tiles,),
        in_specs=[pl.BlockSpec((tm, D_MODEL), lambda m: (m, 0)),
                  pl.BlockSpec((tm, GLA_DV), lambda m: (m, 0)),
                  colblk(OFF_CB, CONV_CH), colblk(OFF_CC, CONV_CH), colblk(OFF_CH, CONV_CH),
                  colblk(OFF_GA, D_MODEL), colblk(OFF_GB, D_MODEL)]
                 + prev_specs
                 + [const((CONV_K, CONV_CH)), _mod_spec(mod, MOD_G1, tm, seq_len),
                    _mod_spec(mod, MOD_SC2, tm, seq_len), _mod_spec(mod, MOD_SH2, tm, seq_len), const((1, D_MODEL)),
                    const((GLA_DV + CONV_CH, D_MODEL)), const((D_MODEL, LANES)), const((D_MODEL, LANES)),
                    const((1, LANES)), pl.BlockSpec(memory_space=pl.ANY)],
        out_specs=[pl.BlockSpec((tm, D_MODEL), lambda m: (m, 0)),
                   pl.BlockSpec((tm, D_MODEL), lambda m: (blk0 + m, 0)),
                   pl.BlockSpec((tm, LANES), lambda m: (m, 0)),
                   pl.BlockSpec((tm, LANES), lambda m: (m, 0)),
                   u_spec],
        out_shape=[jax.ShapeDtypeStruct((t, D_MODEL), f32),
                   jax.ShapeDtypeStruct(hn_buf.shape, bf16),
                   jax.ShapeDtypeStruct((t, LANES), f32),
                   jax.ShapeDtypeStruct((t, LANES), jnp.int32),
                   u_shape],
        input_output_aliases={n_in - 1: 1},
        compiler_params=_cparams(("arbitrary",)),
        name="post_mix",
    )(x2, oa, proj, proj, proj, proj, proj, *prev_args, wc, mod, mod, mod, nw, wo, wrh, wrl, br, hn_buf)


MOE_RB = 128
MOE_TM = 1536
MOE_TN = 512
MOE_KB = MOE_TM // MOE_RB
MOE_NT = D_FF // MOE_TN


def _moe_kernel(ie_ref, ib_ref, in_ref, iv_ref, iz_ref,
                x_hbm, wg_ref, wl_ref, wd_ref, bg_ref, bl_ref, bd_ref, out_hbm,
                xbuf, hbuf, obuf, wcast, sem_in, sem_out):
    i = pl.program_id(0)
    s = pl.program_id(1)
    n_items = pl.num_programs(0)
    nb = in_ref[i]

    def x_copy(item, jb, k):
        return pltpu.make_async_copy(
            x_hbm.at[pl.ds((ib_ref[item] + jb) * MOE_RB, MOE_RB), pl.ds(k * MOE_TN, MOE_TN)],
            xbuf.at[k, pl.ds(jb * MOE_RB, MOE_RB)], sem_in)

    def out_copy(item, jb, k):
        return pltpu.make_async_copy(
            obuf.at[k, pl.ds(jb * MOE_RB, MOE_RB)],
            out_hbm.at[pl.ds((ib_ref[item] + jb) * MOE_RB, MOE_RB), pl.ds(k * MOE_TN, MOE_TN)], sem_out)

    def for_blocks(n, fn):
        def body(jb, carry):
            for k in range(MOE_NT):
                fn(jb, k)
            return carry

        lax.fori_loop(0, n, body, 0)

    def for_rows(fn):
        def body(j, carry):
            fn(pl.multiple_of(j * (4 * MOE_RB), 4 * MOE_RB), 4 * MOE_RB)
            return carry

        lax.fori_loop(0, nb // 4, body, 0)
        base = (nb // 4) * (4 * MOE_RB)
        rem = nb % 4

        @pl.when(rem >= 2)
        def _():
            fn(pl.multiple_of(base, MOE_RB), 2 * MOE_RB)

        @pl.when(rem % 2 == 1)
        def _():
            fn(pl.multiple_of(base + (rem // 2) * (2 * MOE_RB), MOE_RB), MOE_RB)

    @pl.when(s == 0)
    def _():
        @pl.when(i == 0)
        def _():
            for_blocks(nb, lambda jb, k: x_copy(i, jb, k).start())

        for_blocks(nb, lambda jb, k: x_copy(i, jb, k).wait())

    @pl.when(s < MOE_NT)
    def _():
        wcast[:, 0:MOE_TN] = wg_ref[...].astype(bf16)
        wcast[:, MOE_TN:2 * MOE_TN] = wl_ref[...].astype(bf16)
        bg = bg_ref[...]
        bl = bl_ref[...]

        def up(r0, size):
            rows = pl.ds(r0, size)
            xb = jnp.concatenate([xbuf[k, rows, :] for k in range(MOE_NT)], axis=1)
            h = _dot(xb, wcast[...])
            hg = jnp.minimum(h[:, 0:MOE_TN] + bg, SWIGLU_LIMIT)
            hl = jnp.clip(h[:, MOE_TN:2 * MOE_TN] + bl, -SWIGLU_LIMIT, SWIGLU_LIMIT)
            hbuf[s, rows, :] = ((hl + 1.0) * hg * jax.nn.sigmoid(SWIGLU_ALPHA * hg)).astype(bf16)

        for_rows(up)

    @pl.when(s == MOE_NT)
    def _():
        @pl.when(i > 0)
        def _():
            prev = jnp.maximum(i - 1, 0)
            for_blocks(in_ref[prev], lambda jb, k: out_copy(prev, jb, k).wait())

        def fill(jb, k):
            obuf[k, pl.ds(pl.multiple_of(jb * MOE_RB, MOE_RB), MOE_RB), :] = jnp.zeros((MOE_RB, MOE_TN), bf16)
            out_copy(i, jb, k).start()

        nz = iz_ref[i]
        for_blocks(nz, fill)
        for_blocks(nz, lambda jb, k: out_copy(i, jb, k).wait())

        @pl.when(i + 1 < n_items)
        def _():
            nxt = jnp.minimum(i + 1, n_items - 1)
            for_blocks(in_ref[nxt], lambda jb, k: x_copy(nxt, jb, k).start())

    @pl.when(s >= MOE_NT)
    def _():
        wcast[:, 0:MOE_TN] = wd_ref[...].astype(bf16)
        bd = bd_ref[...]

        def down(r0, size):
            rows = pl.ds(r0, size)
            hb = jnp.concatenate([hbuf[k, rows, :] for k in range(MOE_NT)], axis=1)
            obuf[s - MOE_NT, rows, :] = (_dot(hb, wcast[:, 0:MOE_TN]) + bd).astype(bf16)

        for_rows(down)

    @pl.when(s == 2 * MOE_NT - 1)
    def _():
        for_blocks(nb, lambda jb, k: out_copy(i, jb, k).start())

        @pl.when(i == n_items - 1)
        def _():
            for_blocks(nb, lambda jb, k: out_copy(i, jb, k).wait())


def _moe(xs, items, w_up, b_up, w_down, b_down):
    rows = xs.shape[0]
    n_items = items[0].shape[0]

    last = MOE_NT - 1

    def up_tile(i, s, tabs):
        valid = tabs[3][i]
        return jnp.minimum(s, last) * valid + last * (1 - valid)

    def down_tile(i, s, tabs):
        valid = tabs[3][i]
        return jnp.maximum(s - MOE_NT, 0) * valid + last * (1 - valid)

    def expert(i, tabs):
        return tabs[0][i]

    grid_spec = pltpu.PrefetchScalarGridSpec(
        num_scalar_prefetch=len(items),
        grid=(n_items, 2 * MOE_NT),
        in_specs=[pl.BlockSpec(memory_space=pl.ANY),
                  pl.BlockSpec((None, D_MODEL, MOE_TN), lambda i, s, *t: (expert(i, t), 0, up_tile(i, s, t))),
                  pl.BlockSpec((None, D_MODEL, MOE_TN), lambda i, s, *t: (expert(i, t), 0, MOE_NT + up_tile(i, s, t))),
                  pl.BlockSpec((None, D_FF, MOE_TN), lambda i, s, *t: (expert(i, t), 0, down_tile(i, s, t))),
                  pl.BlockSpec((None, 1, MOE_TN), lambda i, s, *t: (expert(i, t), 0, up_tile(i, s, t))),
                  pl.BlockSpec((None, 1, MOE_TN), lambda i, s, *t: (expert(i, t), 0, MOE_NT + up_tile(i, s, t))),
                  pl.BlockSpec((None, 1, MOE_TN), lambda i, s, *t: (expert(i, t), 0, down_tile(i, s, t)))],
        out_specs=pl.BlockSpec(memory_space=pl.ANY),
        scratch_shapes=[pltpu.VMEM((MOE_NT, MOE_TM, MOE_TN), bf16),
                        pltpu.VMEM((MOE_NT, MOE_TM, MOE_TN), bf16),
                        pltpu.VMEM((MOE_NT, MOE_TM, MOE_TN), bf16),
                        pltpu.VMEM((D_MODEL, 2 * MOE_TN), bf16),
                        pltpu.SemaphoreType.DMA(()),
                        pltpu.SemaphoreType.DMA(())],
    )
    return pl.pallas_call(
        _moe_kernel,
        grid_spec=grid_spec,
        out_shape=jax.ShapeDtypeStruct((rows, D_MODEL), bf16),
        compiler_params=_cparams(("arbitrary", "arbitrary")),
        name="moe_experts",
    )(*items,
      xs, w_up, w_up, w_down,
      b_up.reshape(N_EXPERTS, 1, 2 * D_FF), b_up.reshape(N_EXPERTS, 1, 2 * D_FF),
      b_down.reshape(N_EXPERTS, 1, D_MODEL))


FIN_TM = 256


def _final_kernel(h_ref, og_ref, gate_ref, g2_ref, fw_ref, y_ref):
    gate = gate_ref[...]
    ff = og_ref[0].astype(f32) * gate[:, 0:1]
    for j in range(1, TOP_K):
        ff = ff + og_ref[j].astype(f32) * gate[:, j:j + 1]
    h = h_ref[...] + g2_ref[...] * ff
    y_ref[...] = _rms(h) * fw_ref[...]


def _final(h, og, gate, mod, fw, row0, seq_len):
    t = h.shape[0]
    tm = FIN_TM
    blk0 = row0 // tm
    return pl.pallas_call(
        _final_kernel,
        grid=(t // tm,),
        in_specs=[pl.BlockSpec((tm, D_MODEL), lambda m: (m, 0)),
                  pl.BlockSpec((TOP_K, tm, D_MODEL), lambda m: (0, blk0 + m, 0)),
                  pl.BlockSpec((tm, LANES), lambda m: (blk0 + m, 0)),
                  _mod_spec(mod, MOD_G2, tm, seq_len),
                  pl.BlockSpec((1, D_MODEL), lambda m: (0, 0))],
        out_specs=pl.BlockSpec((tm, D_MODEL), lambda m: (m, 0)),
        out_shape=jax.ShapeDtypeStruct((t, D_MODEL), f32),
        compiler_params=_cparams(("arbitrary",)),
        name="final_norm",
    )(h, og, gate, mod, fw)


def _count_le(bounds, x):
    return jnp.sum((bounds[None, :] <= x[:, None]).astype(jnp.int32), axis=1)


def _routing(top_e):
    t = top_e.shape[0]
    n_pairs = t * TOP_K
    e_flat = top_e.reshape(-1)
    onehot = (e_flat[:, None] == jnp.arange(N_EXPERTS, dtype=e_flat.dtype)[None, :]).astype(jnp.int32)
    csum = jnp.cumsum(onehot, axis=0)
    counts = csum[-1]
    nblk = (counts + MOE_RB - 1) // MOE_RB
    blk_end = jnp.cumsum(nblk)
    blk_start = blk_end - nblk
    row_start = blk_start * MOE_RB
    dest = jnp.sum(onehot * (csum + row_start[None, :]), axis=1) - 1
    n_rows = (n_pairs // MOE_RB + N_EXPERTS) * MOE_RB
    order = jnp.argsort(e_flat, stable=True).astype(jnp.int32)
    raw_start = jnp.cumsum(counts) - counts
    r = jnp.arange(n_rows, dtype=jnp.int32)
    begun = row_start[None, :] <= r[:, None]

    def at_row(table):
        steps = jnp.diff(table, prepend=0)
        return jnp.sum(jnp.where(begun, steps[None, :], 0), axis=1)

    src_pos = r - at_row(row_start - raw_start)
    is_real = r < at_row(row_start + counts)
    src_pair = order[jnp.clip(src_pos, 0, n_pairs - 1)]
    src_tok = jnp.where(is_real, src_pair // TOP_K, r % t).astype(jnp.int32)

    n_items = N_EXPERTS + n_rows // MOE_TM
    nit = (nblk + MOE_KB - 1) // MOE_KB
    it_end = jnp.cumsum(nit)
    it_start = it_end - nit
    idx = jnp.arange(n_items, dtype=jnp.int32)
    valid = idx < it_end[-1]
    last_valid = jnp.maximum(it_end[-1] - 1, 0)
    idx_c = jnp.minimum(idx, last_valid)
    e_of = jnp.minimum(_count_le(it_end, idx_c), N_EXPERTS - 1).astype(jnp.int32)
    k = idx_c - it_start[e_of]
    item_nb = jnp.where(valid, jnp.clip(nblk[e_of] - k * MOE_KB, 0, MOE_KB), 0).astype(jnp.int32)
    spare = idx - it_end[-1]
    tail_blk = blk_end[-1] + spare * MOE_KB
    item_nz = jnp.where(valid, 0, jnp.clip(n_rows // MOE_RB - tail_blk, 0, MOE_KB)).astype(jnp.int32)
    item_blk = jnp.where(valid, blk_start[e_of] + k * MOE_KB, tail_blk).astype(jnp.int32)
    items = (e_of, item_blk, item_nb, valid.astype(jnp.int32), item_nz)
    return dest.reshape(t, TOP_K), src_tok, items


def kernel(x_prompt, x_sample, state_gla, state_conv, c_prompt, c_sample, w_ada, b_ada, norm1_w,
           w_in, w_gk_up, b_gk, gla_norm_w, w_conv, w_out, norm2_w, w_router, b_router, w_up, b_up,
           w_down, b_down, final_norm_w):
    n_p, seq_p, _ = x_prompt.shape
    n_s, seq_s, _ = x_sample.shape
    t_p, t_s = n_p * seq_p, n_s * seq_s
    assert seq_s == GS_SEQ and w_ada.shape[0] == 1

    c_all = jnp.concatenate([c_prompt, c_sample], axis=0)
    pad = (-c_all.shape[0]) % 16
    c_all = jnp.pad(c_all, ((0, pad), (0, 0)))
    mod = _ada(c_all, w_ada[0], b_ada[0])
    mod_p = mod[:n_p].reshape(n_p, 1, N_MOD * D_MODEL)
    mod_s = jnp.repeat(mod[n_p:n_p + n_s], seq_s, axis=0)

    w_main, w_alow = _prep_w_in(w_in[0].T)
    w_gk = jnp.pad(w_gk_up[0], ((0, LANES - GLA_LOW_RANK), (0, 0))).astype(bf16)
    bgk = b_gk[0].reshape(1, GLA_DK)
    wo = w_out[0].astype(bf16)
    wr = jnp.pad(w_router[0], ((0, 0), (0, LANES - N_EXPERTS)))
    wr_hi = wr.astype(bf16)
    wr_lo = (wr - wr_hi.astype(f32)).astype(bf16)
    br = jnp.pad(b_router[0], (0, LANES - N_EXPERTS), constant_values=ROUTER_PAD).reshape(1, LANES)
    n1w = norm1_w[0].reshape(1, D_MODEL)
    n2w = norm2_w[0].reshape(1, D_MODEL)
    gnw = gla_norm_w[0].reshape(1, GLA_HV)
    fw = final_norm_w.reshape(1, D_MODEL)
    wc = w_conv[0]

    xp = x_prompt.reshape(t_p, D_MODEL)
    xs_ = x_sample.reshape(t_s, D_MODEL)

    proj_p, la_p = _in_proj(xp, mod_p, n1w, w_main, w_alow, w_gk, bgk, 1024, seq_p)
    proj_s, la_s = _in_proj(xs_, mod_s, n1w, w_main, w_alow, w_gk, bgk, t_s, seq_s)

    gla0 = jnp.zeros((n_p, GLA_HEADS, GLA_HK, GLA_HV), f32)
    conv0 = jnp.zeros((n_p, CONV_K - 1, CONV_CH), f32)
    oa_p, gla_p = _gla_scan(proj_p, la_p, gnw, gla0, n_p, seq_p)
    oa_s, gla_s = _gla_step(proj_s, la_s, gnw, state_gla[0])

    hn = jnp.zeros((2 * (t_p + t_s), D_MODEL), bf16)
    h_p, hn, gate_p, eid_p, ut_p = _post(xp, oa_p, proj_p, conv0, wc, mod_p, n2w, wo,
                                         wr_hi, wr_lo, br, hn, 0, True, seq_p)
    prev_s = (jnp.repeat(state_conv[0][:, 0], seq_s, axis=0), jnp.repeat(state_conv[0][:, 1], seq_s, axis=0))
    h_s, hn, gate_s, eid_s, u_s = _post(xs_, oa_s, proj_s, prev_s, wc, mod_s, n2w, wo,
                                        wr_hi, wr_lo, br, hn, t_p, False, seq_s)
    conv_p = ut_p.reshape(n_p, seq_p // POST_TM, 8, CONV_CH)[:, -1, 8 - (CONV_K - 1):]
    conv_s = u_s.reshape(n_s, seq_s, CONV_CH)[:, seq_s - (CONV_K - 1):]

    gate = jnp.concatenate([gate_p, gate_s], axis=0)
    top_e = jnp.concatenate([eid_p, eid_s], axis=0)[:, :TOP_K]
    dest, src_tok, items = _routing(top_e)
    x_rows = hn[src_tok]
    out_rows = _moe(x_rows, items, w_up[0], b_up[0], w_down[0], b_down[0])
    og = out_rows[dest.T.reshape(-1)].reshape(TOP_K, t_p + t_s, D_MODEL)

    y_p = _final(h_p, og, gate, mod_p, fw, 0, seq_p)
    y_s = _final(h_s, og, gate, mod_s, fw, t_p, seq_s)

    return (y_p.reshape(n_p, seq_p, D_MODEL), y_s.reshape(n_s, seq_s, D_MODEL),
            gla_p[None], conv_p[None], gla_s[None], conv_s[None])
```

```python
import functools

import jax
import jax.numpy as jnp
from jax import lax
from jax.experimental import pallas as pl
from jax.experimental.pallas import tpu as pltpu

f32 = jnp.float32
bf16 = jnp.bfloat16

D_MODEL = 2048
N_MOD = 6
GLA_HEADS = 4
GLA_DK = 512
GLA_DV = 1024
GLA_HK = 128
GLA_HV = 256
GLA_LOW_RANK = 16
GLA_TAU = 16.0
GLA_CHUNK = 64
CONV_CH = 1024
CONV_K = 3
N_EXPERTS = 32
TOP_K = 4
D_FF = 2048
SWIGLU_LIMIT = 7.0
SWIGLU_ALPHA = 1.702
EPS = 1e-6

LANES = 128
D_MAIN = 10240
VMEM_LIMIT = 56 * 1024 * 1024

OFF_Q, OFF_K, OFF_V, OFF_G = 0, 512, 1024, 2048
OFF_CB, OFF_CC, OFF_CH, OFF_GA, OFF_GB = 3072, 4096, 5120, 6144, 8192

ROUTER_PAD = -1e30

HIGHEST = lax.Precision.HIGHEST


def _cparams(sem):
    return pltpu.CompilerParams(dimension_semantics=sem, vmem_limit_bytes=VMEM_LIMIT)


def _dot(a, b):
    return jnp.dot(a, b, preferred_element_type=f32)


def _dot_nt(a, b):
    return lax.dot_general(a, b, (((1,), (1,)), ((), ())), preferred_element_type=f32)


def _dot_exact(a, b):
    return jnp.dot(a, b, precision=HIGHEST, preferred_element_type=f32)


def _sigmoid_tanh(x):
    return 0.5 * jnp.tanh(0.5 * x) + 0.5


def _rms(x):
    return x * lax.rsqrt(jnp.mean(x * x, axis=-1, keepdims=True) + EPS)


ADA_TN = 1024


def _ada_kernel(c_ref, w_ref, b_ref, o_ref):
    c = c_ref[...]
    s = (c * jax.nn.sigmoid(c)).astype(bf16)
    o_ref[...] = _dot(s, w_ref[...].astype(bf16)) + b_ref[...]


def _ada(c, w, b):
    rows = c.shape[0]
    n = w.shape[1]
    return pl.pallas_call(
        _ada_kernel,
        grid=(n // ADA_TN,),
        in_specs=[pl.BlockSpec((rows, D_MODEL), lambda j: (0, 0)),
                  pl.BlockSpec((D_MODEL, ADA_TN), lambda j: (0, j)),
                  pl.BlockSpec((1, ADA_TN), lambda j: (0, j))],
        out_specs=pl.BlockSpec((rows, ADA_TN), lambda j: (0, j)),
        out_shape=jax.ShapeDtypeStruct((rows, n), f32),
        compiler_params=_cparams(("arbitrary",)),
        name="adaln",
    )(c, w, b.reshape(1, n))


PREP_TN = 1024


def _prep_kernel(a_ref, b_ref, low_ref, main_ref, alow_ref):
    j = pl.program_id(0)
    first_shifted = OFF_G // PREP_TN

    @pl.when(j < first_shifted)
    def _():
        main_ref[...] = a_ref[...].T.astype(bf16)

    @pl.when(j >= first_shifted)
    def _():
        wide = jnp.concatenate([a_ref[...], b_ref[...]], axis=0)
        main_ref[...] = wide[GLA_LOW_RANK:GLA_LOW_RANK + PREP_TN].T.astype(bf16)

    @pl.when(j == 0)
    def _():
        low = low_ref[...].T
        lane = lax.broadcasted_iota(jnp.int32, low.shape, 1)
        alow_ref[...] = jnp.where(lane < GLA_LOW_RANK, low, 0.0).astype(bf16)


def _prep_w_in(wt):
    d = wt.shape[1]
    return pl.pallas_call(
        _prep_kernel,
        grid=(D_MAIN // PREP_TN,),
        in_specs=[pl.BlockSpec((PREP_TN, d), lambda j: (j, 0)),
                  pl.BlockSpec((GLA_LOW_RANK, d), lambda j: ((j + 1) * (PREP_TN // GLA_LOW_RANK), 0)),
                  pl.BlockSpec((LANES, d), lambda j: (OFF_G // LANES, 0))],
        out_specs=[pl.BlockSpec((d, PREP_TN), lambda j: (0, j)),
                   pl.BlockSpec((d, LANES), lambda j: (0, 0))],
        out_shape=[jax.ShapeDtypeStruct((d, D_MAIN), bf16),
                   jax.ShapeDtypeStruct((d, LANES), bf16)],
        compiler_params=_cparams(("arbitrary",)),
        name="prep_w_in",
    )(wt, wt, wt)


IN_TN = 1024


def _in_kernel(x_ref, sc_ref, sh_ref, nw_ref, w_ref, wa_ref, wgk_ref, bgk_ref,
               proj_ref, la_ref, xn_ref):
    @pl.when(pl.program_id(1) == 0)
    def _():
        xn = _rms(x_ref[...]) * nw_ref[...] * (1.0 + sc_ref[...]) + sh_ref[...]
        xnb = xn.astype(bf16)
        xn_ref[...] = xnb
        a_low = _dot(xnb, wa_ref[...])
        z = _dot(a_low.astype(bf16), wgk_ref[...]) + bgk_ref[...]
        la_ref[...] = (jnp.minimum(z, 0.0) - jnp.log(1.0 + jnp.exp(-jnp.abs(z)))) * (1.0 / GLA_TAU)

    proj_ref[...] = _dot(xn_ref[...], w_ref[...]).astype(bf16)


MOD_SH1, MOD_SC1, MOD_G1, MOD_SH2, MOD_SC2, MOD_G2 = range(N_MOD)


def _mod_spec(mod, which, tm, seq_len):
    if mod.ndim == 3:
        tiles_per_seq = seq_len // tm
        return pl.BlockSpec((None, 1, D_MODEL), lambda *g: (g[0] // tiles_per_seq, 0, which))
    return pl.BlockSpec((tm, D_MODEL), lambda *g: (g[0], which))


def _in_proj(x2, mod, nw, w_main, w_alow, w_gk, b_gk, tm, seq_len):
    t = x2.shape[0]
    return pl.pallas_call(
        _in_kernel,
        grid=(t // tm, D_MAIN // IN_TN),
        in_specs=[pl.BlockSpec((tm, D_MODEL), lambda m, n: (m, 0)),
                  _mod_spec(mod, MOD_SC1, tm, seq_len), _mod_spec(mod, MOD_SH1, tm, seq_len),
                  pl.BlockSpec((1, D_MODEL), lambda m, n: (0, 0)),
                  pl.BlockSpec((D_MODEL, IN_TN), lambda m, n: (0, n)),
                  pl.BlockSpec((D_MODEL, LANES), lambda m, n: (0, 0)),
                  pl.BlockSpec((LANES, GLA_DK), lambda m, n: (0, 0)),
                  pl.BlockSpec((1, GLA_DK), lambda m, n: (0, 0))],
        out_specs=[pl.BlockSpec((tm, IN_TN), lambda m, n: (m, n)),
                   pl.BlockSpec((tm, GLA_DK), lambda m, n: (m, 0))],
        out_shape=[jax.ShapeDtypeStruct((t, D_MAIN), bf16),
                   jax.ShapeDtypeStruct((t, GLA_DK), f32)],
        scratch_shapes=[pltpu.VMEM((tm, D_MODEL), bf16)],
        compiler_params=_cparams(("arbitrary", "arbitrary")),
        name="in_proj",
    )(x2, mod, mod, nw, w_main, w_alow, w_gk, b_gk)


GLA_R = 256


def _gla_norm_gate(o, nw, g):
    gf = g.astype(f32)
    return _rms(o) * nw * (gf * jax.nn.sigmoid(gf))


def _gla_scan_kernel(q_ref, k_ref, v_ref, g_ref, la_ref, nw_ref, s0_ref, o_ref, sout_ref, s_ref):
    step = pl.program_id(1)

    @pl.when(step == 0)
    def _():
        s_ref[...] = s0_ref[...]

    r, c = GLA_R, GLA_CHUNK
    n_chunks = r // c
    row = lax.broadcasted_iota(jnp.int32, (r, r), 0)
    col = lax.broadcasted_iota(jnp.int32, (r, r), 1)
    same_chunk = (row // c) == (col // c)
    causal = same_chunk & (col <= row)

    la = la_ref[...]
    b = _dot_exact(causal.astype(f32), la)
    b_mid = jnp.concatenate(
        [jnp.broadcast_to(b[i * c + c // 2:i * c + c // 2 + 1], (c, GLA_DK)) for i in range(n_chunks)], axis=0)
    b_last = jnp.concatenate(
        [jnp.broadcast_to(b[i * c + c - 1:i * c + c], (c, GLA_DK)) for i in range(n_chunks)], axis=0)
    q = q_ref[...].astype(f32) * (GLA_HK ** -0.5)
    k = k_ref[...].astype(f32)
    qs = (q * jnp.exp(b - b_mid)).astype(bf16)
    ks = (k * jnp.exp(b_mid - b)).astype(bf16)
    qd = (q * jnp.exp(b)).astype(bf16)
    kd_t = (k * jnp.exp(b_last - b)).T.astype(bf16)
    la_t = la.T
    lane = lax.broadcasted_iota(jnp.int32, (GLA_HK, r), 1)
    nw = nw_ref[...]

    for h in range(GLA_HEADS):
        hk = slice(h * GLA_HK, (h + 1) * GLA_HK)
        hv = slice(h * GLA_HV, (h + 1) * GLA_HV)
        v_h = v_ref[:, hv]
        scores = _dot_nt(qs[:, hk], ks[:, hk])
        o_intra = _dot(jnp.where(causal, scores, 0.0).astype(bf16), v_h)
        s = s_ref[h]
        for i in range(n_chunks):
            rows = slice(i * c, (i + 1) * c)
            in_chunk = (lane // c) == i
            o = o_intra[rows] + _dot(qd[rows, hk], s.astype(bf16))
            o_ref[rows, hv] = _gla_norm_gate(o, nw, g_ref[rows, hv]).astype(bf16)
            decay = jnp.exp(jnp.sum(jnp.where(in_chunk, la_t[hk], 0.0), axis=1, keepdims=True))
            s = decay * s + _dot(jnp.where(in_chunk, kd_t[hk], jnp.zeros_like(kd_t[hk])), v_h)
        s_ref[h] = s

    @pl.when(step == pl.num_programs(1) - 1)
    def _():
        sout_ref[...] = s_ref[...]


def _gla_scan(proj, la, nw, s0, n_seq, seq_len):
    steps = seq_len // GLA_R
    t = n_seq * seq_len

    def rows(b, s):
        return b * steps + s

    return pl.pallas_call(
        _gla_scan_kernel,
        grid=(n_seq, steps),
        in_specs=[pl.BlockSpec((GLA_R, GLA_DK), lambda b, s: (rows(b, s), OFF_Q // GLA_DK)),
                  pl.BlockSpec((GLA_R, GLA_DK), lambda b, s: (rows(b, s), OFF_K // GLA_DK)),
                  pl.BlockSpec((GLA_R, GLA_DV), lambda b, s: (rows(b, s), OFF_V // GLA_DV)),
                  pl.BlockSpec((GLA_R, GLA_DV), lambda b, s: (rows(b, s), OFF_G // GLA_DV)),
                  pl.BlockSpec((GLA_R, GLA_DK), lambda b, s: (rows(b, s), 0)),
                  pl.BlockSpec((1, GLA_HV), lambda b, s: (0, 0)),
                  pl.BlockSpec((None, GLA_HEADS, GLA_HK, GLA_HV), lambda b, s: (b, 0, 0, 0))],
        out_specs=[pl.BlockSpec((GLA_R, GLA_DV), lambda b, s: (rows(b, s), 0)),
                   pl.BlockSpec((None, GLA_HEADS, GLA_HK, GLA_HV), lambda b, s: (b, 0, 0, 0))],
        out_shape=[jax.ShapeDtypeStruct((t, GLA_DV), bf16),
                   jax.ShapeDtypeStruct((n_seq, GLA_HEADS, GLA_HK, GLA_HV), f32)],
        scratch_shapes=[pltpu.VMEM((GLA_HEADS, GLA_HK, GLA_HV), f32)],
        compiler_params=_cparams(("arbitrary", "arbitrary")),
        name="gla_scan",
    )(proj, proj, proj, proj, la, nw, s0)


GS_SEQ = 4
GS_ROWS = 128
GS_B = GS_ROWS // GS_SEQ
GS_STEP_B = 4


def _gla_step_kernel(q_ref, k_ref, v_ref, g_ref, la_ref, nw_ref, s0_ref, o_ref, sout_ref,
                     oacc_ref, qd_ref, kdt_ref, lat_ref):
    j = pl.program_id(1)
    r = GS_ROWS
    row = lax.broadcasted_iota(jnp.int32, (r, r), 0)
    col = lax.broadcasted_iota(jnp.int32, (r, r), 1)

    @pl.when(j == 0)
    def _():
        same_seq = (row // GS_SEQ) == (col // GS_SEQ)
        causal = same_seq & (col <= row)
        la = la_ref[...]
        b = _dot_exact(causal.astype(f32), la)
        b_mid = _dot_exact((same_seq & (col % GS_SEQ <= GS_SEQ // 2)).astype(f32), la)
        b_last = _dot_exact(same_seq.astype(f32), la)
        q = q_ref[...].astype(f32) * (GLA_HK ** -0.5)
        k = k_ref[...].astype(f32)
        qs = (q * jnp.exp(b - b_mid)).astype(bf16)
        ks = (k * jnp.exp(b_mid - b)).astype(bf16)
        qd_ref[...] = (q * jnp.exp(b)).astype(bf16)
        kdt_ref[...] = (k * jnp.exp(b_last - b)).T.astype(bf16)
        lat_ref[...] = la.T
        for h in range(GLA_HEADS):
            hk = slice(h * GLA_HK, (h + 1) * GLA_HK)
            hv = slice(h * GLA_HV, (h + 1) * GLA_HV)
            scores = _dot_nt(qs[:, hk], ks[:, hk])
            oacc_ref[:, hv] = _dot(jnp.where(causal, scores, 0.0).astype(bf16), v_ref[:, hv])

    for bb in range(GS_STEP_B):
        seq = j * GS_STEP_B + bb
        in_seq_lane = (col // GS_SEQ) == seq
        in_seq_row = (row[:, :GLA_HK] // GS_SEQ) == seq
        for h in range(GLA_HEADS):
            hk = slice(h * GLA_HK, (h + 1) * GLA_HK)
            hv = slice(h * GLA_HV, (h + 1) * GLA_HV)
            s0 = s0_ref[bb, h]
            kd_t = kdt_ref[hk, :]
            ds = _dot(jnp.where(in_seq_lane, kd_t, jnp.zeros_like(kd_t)), v_ref[:, hv])
            decay = jnp.exp(jnp.sum(jnp.where(in_seq_lane, lat_ref[hk, :], 0.0), axis=1, keepdims=True))
            sout_ref[bb, h] = decay * s0 + ds
            qd = qd_ref[:, hk]
            oacc_ref[:, hv] += _dot(jnp.where(in_seq_row, qd, jnp.zeros_like(qd)), s0.astype(bf16))

    @pl.when(j == pl.num_programs(1) - 1)
    def _():
        nw = nw_ref[...]
        for h in range(GLA_HEADS):
            hv = slice(h * GLA_HV, (h + 1) * GLA_HV)
            o_ref[:, hv] = _gla_norm_gate(oacc_ref[:, hv], nw, g_ref[:, hv]).astype(bf16)


def _gla_step(proj, la, nw, s0):
    n_seq = s0.shape[0]
    t = n_seq * GS_SEQ
    groups = t // GS_ROWS
    steps = GS_B // GS_STEP_B
    return pl.pallas_call(
        _gla_step_kernel,
        grid=(groups, steps),
        in_specs=[pl.BlockSpec((GS_ROWS, GLA_DK), lambda g, j: (g, OFF_Q // GLA_DK)),
                  pl.BlockSpec((GS_ROWS, GLA_DK), lambda g, j: (g, OFF_K // GLA_DK)),
                  pl.BlockSpec((GS_ROWS, GLA_DV), lambda g, j: (g, OFF_V // GLA_DV)),
                  pl.BlockSpec((GS_ROWS, GLA_DV), lambda g, j: (g, OFF_G // GLA_DV)),
                  pl.BlockSpec((GS_ROWS, GLA_DK), lambda g, j: (g, 0)),
                  pl.BlockSpec((1, GLA_HV), lambda g, j: (0, 0)),
                  pl.BlockSpec((GS_STEP_B, GLA_HEADS, GLA_HK, GLA_HV), lambda g, j: (g * steps + j, 0, 0, 0))],
        out_specs=[pl.BlockSpec((GS_ROWS, GLA_DV), lambda g, j: (g, 0)),
                   pl.BlockSpec((GS_STEP_B, GLA_HEADS, GLA_HK, GLA_HV), lambda g, j: (g * steps + j, 0, 0, 0))],
        out_shape=[jax.ShapeDtypeStruct((t, GLA_DV), bf16),
                   jax.ShapeDtypeStruct((n_seq, GLA_HEADS, GLA_HK, GLA_HV), f32)],
        scratch_shapes=[pltpu.VMEM((GS_ROWS, GLA_DV), f32),
                        pltpu.VMEM((GS_ROWS, GLA_DK), bf16),
                        pltpu.VMEM((GLA_DK, GS_ROWS), bf16),
                        pltpu.VMEM((GLA_DK, GS_ROWS), f32)],
        compiler_params=_cparams(("arbitrary", "arbitrary")),
        name="gla_step",
    )(proj, proj, proj, proj, la, nw, s0)


POST_TM = 256
HALO = 16


def _post_kernel(long_seq, tiles_per_seq, *refs):
    if long_seq:
        (x_ref, oa_ref, cb_ref, cc_ref, ch_ref, ga_ref, gb_ref, hcc_ref, hch_ref, cbuf_ref,
         wc_ref, g1_ref, sc_ref, sh_ref, nw_ref, wo_ref, wrh_ref, wrl_ref, br_ref, _hn_alias,
         h_ref, hn_ref, gate_ref, eid_ref, u_ref) = refs
    else:
        (x_ref, oa_ref, cb_ref, cc_ref, ch_ref, ga_ref, gb_ref, p0_ref, p1_ref,
         wc_ref, g1_ref, sc_ref, sh_ref, nw_ref, wo_ref, wrh_ref, wrl_ref, br_ref, _hn_alias,
         h_ref, hn_ref, gate_ref, eid_ref, u_ref) = refs
    tm = x_ref.shape[0]
    u = cc_ref[...].astype(f32) * ch_ref[...].astype(f32)
    row = lax.broadcasted_iota(jnp.int32, (tm, CONV_CH), 0)
    if long_seq:
        pos = row
        first = (pl.program_id(0) % tiles_per_seq) == 0
        halo = hcc_ref[HALO - 2:HALO, :].astype(f32) * hch_ref[HALO - 2:HALO, :].astype(f32)
        cbuf = cbuf_ref[...]
        p0 = jnp.where(first, cbuf[0:1], halo[0:1])
        p1 = jnp.where(first, cbuf[1:2], halo[1:2])
        u_ref[...] = u[tm - 8:tm]
    else:
        pos = row % GS_SEQ
        p0 = p0_ref[...]
        p1 = p1_ref[...]
        u_ref[...] = u
    u1 = jnp.where(pos == 0, p1, pltpu.roll(u, 1, 0))
    u2 = jnp.where(pos == 0, p0, jnp.where(pos == 1, p1, pltpu.roll(u, 2, 0)))
    wc = wc_ref[...]
    conv = wc[0:1] * u2 + wc[1:2] * u1 + wc[2:3] * u
    ob = (cb_ref[...].astype(f32) * conv).astype(bf16)

    ya = _dot(oa_ref[...], wo_ref[0:GLA_DV, :])
    yb = _dot(ob, wo_ref[GLA_DV:GLA_DV + CONV_CH, :])
    y = _sigmoid_tanh(ga_ref[...].astype(f32)) * ya + _sigmoid_tanh(gb_ref[...].astype(f32)) * yb
    h = x_ref[...] + g1_ref[...] * y
    h_ref[...] = h
    hn = _rms(h) * nw_ref[...] * (1.0 + sc_ref[...]) + sh_ref[...]
    hn_hi = hn.astype(bf16)
    hn_ref[...] = hn_hi
    hn_lo = (hn - hn_hi.astype(f32)).astype(bf16)
    lg = (_dot(hn_hi, wrh_ref[...]) + _dot(hn_hi, wrl_ref[...]) + _dot(hn_lo, wrh_ref[...])
          + br_ref[...])

    lane = lax.broadcasted_iota(jnp.int32, (tm, LANES), 1)
    lane_f = lane.astype(f32)
    vals, idxs = [], []
    for _ in range(TOP_K):
        m = jnp.max(lg, axis=-1, keepdims=True)
        idx = jnp.min(jnp.where(lg == m, lane_f, float(LANES)), axis=-1, keepdims=True)
        vals.append(m)
        idxs.append(idx)
        lg = jnp.where(lane_f == idx, -jnp.inf, lg)
    exps = [jnp.exp(v - vals[0]) for v in vals]
    den = exps[0] + exps[1] + exps[2] + exps[3]
    gate = jnp.zeros((tm, LANES), f32)
    eid = jnp.zeros((tm, LANES), f32)
    for j in range(TOP_K):
        gate = jnp.where(lane == j, exps[j] / den, gate)
        eid = jnp.where(lane == j, idxs[j], eid)
    gate_ref[...] = gate
    eid_ref[...] = eid.astype(jnp.int32)


def _post(x2, oa, proj, prev, wc, mod, nw, wo, wrh, wrl, br, hn_buf, row0, long_seq, seq_len):
    t = x2.shape[0]
    tm = POST_TM
    n_tiles = t // tm
    blk0 = row0 // tm
    tiles_per_seq = seq_len // tm if long_seq else 1

    def colblk(off, width):
        return pl.BlockSpec((tm, width), lambda m: (m, off // width))

    const = lambda shape: pl.BlockSpec(shape, lambda m: tuple(0 for _ in shape))
    if long_seq:
        halo_rows = lambda m: jnp.maximum(m * (tm // HALO) - 1, 0)
        prev_specs = [pl.BlockSpec((HALO, CONV_CH), lambda m: (halo_rows(m), OFF_CC // CONV_CH)),
                      pl.BlockSpec((HALO, CONV_CH), lambda m: (halo_rows(m), OFF_CH // CONV_CH)),
                      pl.BlockSpec((None, CONV_K - 1, CONV_CH), lambda m: (m // tiles_per_seq, 0, 0))]
        prev_args = [proj, proj, prev]
        u_spec = pl.BlockSpec((None, 8, CONV_CH), lambda m: (m, 0, 0))
        u_shape = jax.ShapeDtypeStruct((n_tiles, 8, CONV_CH), f32)
    else:
        prev_specs = [pl.BlockSpec((tm, CONV_CH), lambda m: (m, 0))] * 2
        prev_args = list(prev)
        u_spec = pl.BlockSpec((tm, CONV_CH), lambda m: (m, 0))
        u_shape = jax.ShapeDtypeStruct((t, CONV_CH), f32)
    n_in = 7 + len(prev_specs) + 10
    return pl.pallas_call(
        functools.partial(_post_kernel, long_seq, tiles_per_seq),
        grid=(n_tiles,),
        in_specs=[pl.BlockSpec((tm, D_MODEL), lambda m: (m, 0)),
                  pl.BlockSpec((tm, GLA_DV), lambda m: (m, 0)),
                  colblk(OFF_CB, CONV_CH), colblk(OFF_CC, CONV_CH), colblk(OFF_CH, CONV_CH),
                  colblk(OFF_GA, D_MODEL), colblk(OFF_GB, D_MODEL)]
                 + prev_specs
                 + [const((CONV_K, CONV_CH)), _mod_spec(mod, MOD_G1, tm, seq_len),
                    _mod_spec(mod, MOD_SC2, tm, seq_len), _mod_spec(mod, MOD_SH2, tm, seq_len), const((1, D_MODEL)),
                    const((GLA_DV + CONV_CH, D_MODEL)), const((D_MODEL, LANES)), const((D_MODEL, LANES)),
                    const((1, LANES)), pl.BlockSpec(memory_space=pl.ANY)],
        out_specs=[pl.BlockSpec((tm, D_MODEL), lambda m: (m, 0)),
                   pl.BlockSpec((tm, D_MODEL), lambda m: (blk0 + m, 0)),
                   pl.BlockSpec((tm, LANES), lambda m: (m, 0)),
                   pl.BlockSpec((tm, LANES), lambda m: (m, 0)),
                   u_spec],
        out_shape=[jax.ShapeDtypeStruct((t, D_MODEL), f32),
                   jax.ShapeDtypeStruct(hn_buf.shape, bf16),
                   jax.ShapeDtypeStruct((t, LANES), f32),
                   jax.ShapeDtypeStruct((t, LANES), jnp.int32),
                   u_shape],
        input_output_aliases={n_in - 1: 1},
        compiler_params=_cparams(("arbitrary",)),
        name="post_mix",
    )(x2, oa, proj, proj, proj, proj, proj, *prev_args, wc, mod, mod, mod, nw, wo, wrh, wrl, br, hn_buf)


MOE_RB = 128
MOE_TM = 1536
MOE_TN = 512
MOE_KB = MOE_TM // MOE_RB
MOE_NT = D_FF // MOE_TN


def _moe_kernel(ie_ref, ib_ref, in_ref, iv_ref, iz_ref,
                x_hbm, wg_ref, wl_ref, wd_ref, bg_ref, bl_ref, bd_ref, out_hbm,
                xbuf, hbuf, obuf, wcast, sem_in, sem_out):
    i = pl.program_id(0)
    s = pl.program_id(1)
    n_items = pl.num_programs(0)
    nb = in_ref[i]

    def x_copy(item, jb, k):
        return pltpu.make_async_copy(
            x_hbm.at[pl.ds((ib_ref[item] + jb) * MOE_RB, MOE_RB), pl.ds(k * MOE_TN, MOE_TN)],
            xbuf.at[k, pl.ds(jb * MOE_RB, MOE_RB)], sem_in)

    def out_copy(item, jb, k):
        return pltpu.make_async_copy(
            obuf.at[k, pl.ds(jb * MOE_RB, MOE_RB)],
            out_hbm.at[pl.ds((ib_ref[item] + jb) * MOE_RB, MOE_RB), pl.ds(k * MOE_TN, MOE_TN)], sem_out)

    def for_blocks(n, fn):
        def body(jb, carry):
            for k in range(MOE_NT):
                fn(jb, k)
            return carry

        lax.fori_loop(0, n, body, 0)

    def for_rows(fn):
        def body(j, carry):
            fn(pl.multiple_of(j * (4 * MOE_RB), 4 * MOE_RB), 4 * MOE_RB)
            return carry

        lax.fori_loop(0, nb // 4, body, 0)
        base = (nb // 4) * (4 * MOE_RB)
        rem = nb % 4

        @pl.when(rem >= 2)
        def _():
            fn(pl.multiple_of(base, MOE_RB), 2 * MOE_RB)

        @pl.when(rem % 2 == 1)
        def _():
            fn(pl.multiple_of(base + (rem // 2) * (2 * MOE_RB), MOE_RB), MOE_RB)

    @pl.when(s == 0)
    def _():
        @pl.when(i == 0)
        def _():
            for_blocks(nb, lambda jb, k: x_copy(i, jb, k).start())

        for_blocks(nb, lambda jb, k: x_copy(i, jb, k).wait())

    @pl.when(s < MOE_NT)
    def _():
        wcast[:, 0:MOE_TN] = wg_ref[...].astype(bf16)
        wcast[:, MOE_TN:2 * MOE_TN] = wl_ref[...].astype(bf16)
        bg = bg_ref[...]
        bl = bl_ref[...]

        def up(r0, size):
            rows = pl.ds(r0, size)
            xb = jnp.concatenate([xbuf[k, rows, :] for k in range(MOE_NT)], axis=1)
            h = _dot(xb, wcast[...])
            hg = jnp.minimum(h[:, 0:MOE_TN] + bg, SWIGLU_LIMIT)
            hl = jnp.clip(h[:, MOE_TN:2 * MOE_TN] + bl, -SWIGLU_LIMIT, SWIGLU_LIMIT)
            hbuf[s, rows, :] = ((hl + 1.0) * hg * jax.nn.sigmoid(SWIGLU_ALPHA * hg)).astype(bf16)

        for_rows(up)

    @pl.when(s == MOE_NT)
    def _():
        @pl.when(i > 0)
        def _():
            prev = jnp.maximum(i - 1, 0)
            for_blocks(in_ref[prev], lambda jb, k: out_copy(prev, jb, k).wait())

        def fill(jb, k):
            obuf[k, pl.ds(pl.multiple_of(jb * MOE_RB, MOE_RB), MOE_RB), :] = jnp.zeros((MOE_RB, MOE_TN), bf16)
            out_copy(i, jb, k).start()

        nz = iz_ref[i]
        for_blocks(nz, fill)
        for_blocks(nz, lambda jb, k: out_copy(i, jb, k).wait())

        @pl.when(i + 1 < n_items)
        def _():
            nxt = jnp.minimum(i + 1, n_items - 1)
            for_blocks(in_ref[nxt], lambda jb, k: x_copy(nxt, jb, k).start())

    @pl.when(s >= MOE_NT)
    def _():
        wcast[:, 0:MOE_TN] = wd_ref[...].astype(bf16)
        bd = bd_ref[...]

        def down(r0, size):
            rows = pl.ds(r0, size)
            hb = jnp.concatenate([hbuf[k, rows, :] for k in range(MOE_NT)], axis=1)
            obuf[s - MOE_NT, rows, :] = (_dot(hb, wcast[:, 0:MOE_TN]) + bd).astype(bf16)

        for_rows(down)

    @pl.when(s == 2 * MOE_NT - 1)
    def _():
        for_blocks(nb, lambda jb, k: out_copy(i, jb, k).start())

        @pl.when(i == n_items - 1)
        def _():
            for_blocks(nb, lambda jb, k: out_copy(i, jb, k).wait())


def _moe(xs, items, w_up, b_up, w_down, b_down):
    rows = xs.shape[0]
    n_items = items[0].shape[0]

    last = MOE_NT - 1

    def up_tile(i, s, tabs):
        valid = tabs[3][i]
        return jnp.minimum(s, last) * valid + last * (1 - valid)

    def down_tile(i, s, tabs):
        valid = tabs[3][i]
        return jnp.maximum(s - MOE_NT, 0) * valid + last * (1 - valid)

    def expert(i, tabs):
        return tabs[0][i]

    grid_spec = pltpu.PrefetchScalarGridSpec(
        num_scalar_prefetch=len(items),
        grid=(n_items, 2 * MOE_NT),
        in_specs=[pl.BlockSpec(memory_space=pl.ANY),
                  pl.BlockSpec((None, D_MODEL, MOE_TN), lambda i, s, *t: (expert(i, t), 0, up_tile(i, s, t))),
                  pl.BlockSpec((None, D_MODEL, MOE_TN), lambda i, s, *t: (expert(i, t), 0, MOE_NT + up_tile(i, s, t))),
                  pl.BlockSpec((None, D_FF, MOE_TN), lambda i, s, *t: (expert(i, t), 0, down_tile(i, s, t))),
                  pl.BlockSpec((None, 1, MOE_TN), lambda i, s, *t: (expert(i, t), 0, up_tile(i, s, t))),
                  pl.BlockSpec((None, 1, MOE_TN), lambda i, s, *t: (expert(i, t), 0, MOE_NT + up_tile(i, s, t))),
                  pl.BlockSpec((None, 1, MOE_TN), lambda i, s, *t: (expert(i, t), 0, down_tile(i, s, t)))],
        out_specs=pl.BlockSpec(memory_space=pl.ANY),
        scratch_shapes=[pltpu.VMEM((MOE_NT, MOE_TM, MOE_TN), bf16),
                        pltpu.VMEM((MOE_NT, MOE_TM, MOE_TN), bf16),
                        pltpu.VMEM((MOE_NT, MOE_TM, MOE_TN), bf16),
                        pltpu.VMEM((D_MODEL, 2 * MOE_TN), bf16),
                        pltpu.SemaphoreType.DMA(()),
                        pltpu.SemaphoreType.DMA(())],
    )
    return pl.pallas_call(
        _moe_kernel,
        grid_spec=grid_spec,
        out_shape=jax.ShapeDtypeStruct((rows, D_MODEL), bf16),
        compiler_params=_cparams(("arbitrary", "arbitrary")),
        name="moe_experts",
    )(*items,
      xs, w_up, w_up, w_down,
      b_up.reshape(N_EXPERTS, 1, 2 * D_FF), b_up.reshape(N_EXPERTS, 1, 2 * D_FF),
      b_down.reshape(N_EXPERTS, 1, D_MODEL))


FIN_TM = 256


def _final_kernel(h_ref, og_ref, gate_ref, g2_ref, fw_ref, y_ref):
    gate = gate_ref[...]
    ff = og_ref[0].astype(f32) * gate[:, 0:1]
    for j in range(1, TOP_K):
        ff = ff + og_ref[j].astype(f32) * gate[:, j:j + 1]
    h = h_ref[...] + g2_ref[...] * ff
    y_ref[...] = _rms(h) * fw_ref[...]


def _final(h, og, gate, mod, fw, row0, seq_len):
    t = h.shape[0]
    tm = FIN_TM
    blk0 = row0 // tm
    return pl.pallas_call(
        _final_kernel,
        grid=(t // tm,),
        in_specs=[pl.BlockSpec((tm, D_MODEL), lambda m: (m, 0)),
                  pl.BlockSpec((TOP_K, tm, D_MODEL), lambda m: (0, blk0 + m, 0)),
                  pl.BlockSpec((tm, LANES), lambda m: (blk0 + m, 0)),
                  _mod_spec(mod, MOD_G2, tm, seq_len),
                  pl.BlockSpec((1, D_MODEL), lambda m: (0, 0))],
        out_specs=pl.BlockSpec((tm, D_MODEL), lambda m: (m, 0)),
        out_shape=jax.ShapeDtypeStruct((t, D_MODEL), f32),
        compiler_params=_cparams(("arbitrary",)),
        name="final_norm",
    )(h, og, gate, mod, fw)


def _count_le(bounds, x):
    return jnp.sum((bounds[None, :] <= x[:, None]).astype(jnp.int32), axis=1)


def _routing(top_e):
    t = top_e.shape[0]
    n_pairs = t * TOP_K
    e_flat = top_e.reshape(-1)
    onehot = (e_flat[:, None] == jnp.arange(N_EXPERTS, dtype=e_flat.dtype)[None, :]).astype(jnp.int32)
    csum = jnp.cumsum(onehot, axis=0)
    counts = csum[-1]
    nblk = (counts + MOE_RB - 1) // MOE_RB
    blk_end = jnp.cumsum(nblk)
    blk_start = blk_end - nblk
    row_start = blk_start * MOE_RB
    dest = jnp.sum(onehot * (csum + row_start[None, :]), axis=1) - 1
    n_rows = (n_pairs // MOE_RB + N_EXPERTS) * MOE_RB
    order = jnp.argsort(e_flat, stable=True).astype(jnp.int32)
    raw_start = jnp.cumsum(counts) - counts
    r = jnp.arange(n_rows, dtype=jnp.int32)
    begun = row_start[None, :] <= r[:, None]

    def at_row(table):
        steps = jnp.diff(table, prepend=0)
        return jnp.sum(jnp.where(begun, steps[None, :], 0), axis=1)

    src_pos = r - at_row(row_start - raw_start)
    is_real = r < at_row(row_start + counts)
    src_pair = order[jnp.clip(src_pos, 0, n_pairs - 1)]
    src_tok = jnp.where(is_real, src_pair // TOP_K, r % t).astype(jnp.int32)

    n_items = N_EXPERTS + n_rows // MOE_TM
    nit = (nblk + MOE_KB - 1) // MOE_KB
    it_end = jnp.cumsum(nit)
    it_start = it_end - nit
    idx = jnp.arange(n_items, dtype=jnp.int32)
    valid = idx < it_end[-1]
    last_valid = jnp.maximum(it_end[-1] - 1, 0)
    idx_c = jnp.minimum(idx, last_valid)
    e_of = jnp.minimum(_count_le(it_end, idx_c), N_EXPERTS - 1).astype(jnp.int32)
    k = idx_c - it_start[e_of]
    item_nb = jnp.where(valid, jnp.clip(nblk[e_of] - k * MOE_KB, 0, MOE_KB), 0).astype(jnp.int32)
    spare = idx - it_end[-1]
    tail_blk = blk_end[-1] + spare * MOE_KB
    item_nz = jnp.where(valid, 0, jnp.clip(n_rows // MOE_RB - tail_blk, 0, MOE_KB)).astype(jnp.int32)
    item_blk = jnp.where(valid, blk_start[e_of] + k * MOE_KB, tail_blk).astype(jnp.int32)
    items = (e_of, item_blk, item_nb, valid.astype(jnp.int32), item_nz)
    return dest.reshape(t, TOP_K), src_tok, items


def kernel(x_prompt, x_sample, state_gla, state_conv, c_prompt, c_sample, w_ada, b_ada, norm1_w,
           w_in, w_gk_up, b_gk, gla_norm_w, w_conv, w_out, norm2_w, w_router, b_router, w_up, b_up,
           w_down, b_down, final_norm_w):
    n_p, seq_p, _ = x_prompt.shape
    n_s, seq_s, _ = x_sample.shape
    t_p, t_s = n_p * seq_p, n_s * seq_s
    assert seq_s == GS_SEQ and w_ada.shape[0] == 1

    c_all = jnp.concatenate([c_prompt, c_sample], axis=0)
    pad = (-c_all.shape[0]) % 16
    c_all = jnp.pad(c_all, ((0, pad), (0, 0)))
    mod = _ada(c_all, w_ada[0], b_ada[0])
    mod_p = mod[:n_p].reshape(n_p, 1, N_MOD * D_MODEL)
    mod_s = jnp.repeat(mod[n_p:n_p + n_s], seq_s, axis=0)

    w_main, w_alow = _prep_w_in(w_in[0].T)
    w_gk = jnp.pad(w_gk_up[0], ((0, LANES - GLA_LOW_RANK), (0, 0))).astype(bf16)
    bgk = b_gk[0].reshape(1, GLA_DK)
    wo = w_out[0].astype(bf16)
    wr = jnp.pad(w_router[0], ((0, 0), (0, LANES - N_EXPERTS)))
    wr_hi = wr.astype(bf16)
    wr_lo = (wr - wr_hi.astype(f32)).astype(bf16)
    br = jnp.pad(b_router[0], (0, LANES - N_EXPERTS), constant_values=ROUTER_PAD).reshape(1, LANES)
    n1w = norm1_w[0].reshape(1, D_MODEL)
    n2w = norm2_w[0].reshape(1, D_MODEL)
    gnw = gla_norm_w[0].reshape(1, GLA_HV)
    fw = final_norm_w.reshape(1, D_MODEL)
    wc = w_conv[0]

    xp = x_prompt.reshape(t_p, D_MODEL)
    xs_ = x_sample.reshape(t_s, D_MODEL)

    proj_p, la_p = _in_proj(xp, mod_p, n1w, w_main, w_alow, w_gk, bgk, 1024, seq_p)
    proj_s, la_s = _in_proj(xs_, mod_s, n1w, w_main, w_alow, w_gk, bgk, t_s, seq_s)

    gla0 = jnp.zeros((n_p, GLA_HEADS, GLA_HK, GLA_HV), f32)
    conv0 = jnp.zeros((n_p, CONV_K - 1, CONV_CH), f32)
    oa_p, gla_p = _gla_scan(proj_p, la_p, gnw, gla0, n_p, seq_p)
    oa_s, gla_s = _gla_step(proj_s, la_s, gnw, state_gla[0])

    hn = jnp.zeros((2 * (t_p + t_s), D_MODEL), bf16)
    h_p, hn, gate_p, eid_p, ut_p = _post(xp, oa_p, proj_p, conv0, wc, mod_p, n2w, wo,
                                         wr_hi, wr_lo, br, hn, 0, True, seq_p)
    prev_s = (jnp.repeat(state_conv[0][:, 0], seq_s, axis=0), jnp.repeat(state_conv[0][:, 1], seq_s, axis=0))
    h_s, hn, gate_s, eid_s, u_s = _post(xs_, oa_s, proj_s, prev_s, wc, mod_s, n2w, wo,
                                        wr_hi, wr_lo, br, hn, t_p, False, seq_s)
    conv_p = ut_p.reshape(n_p, seq_p // POST_TM, 8, CONV_CH)[:, -1, 8 - (CONV_K - 1):]
    conv_s = u_s.reshape(n_s, seq_s, CONV_CH)[:, seq_s - (CONV_K - 1):]

    gate = jnp.concatenate([gate_p, gate_s], axis=0)
    top_e = jnp.concatenate([eid_p, eid_s], axis=0)[:, :TOP_K]
    dest, src_tok, items = _routing(top_e)
    x_rows = hn[src_tok]
    out_rows = _moe(x_rows, items, w_up[0], b_up[0], w_down[0], b_down[0])
    og = out_rows[dest.T.reshape(-1)].reshape(TOP_K, t_p + t_s, D_MODEL)

    y_p = _final(h_p, og, gate, mod_p, fw, 0, seq_p)
    y_s = _final(h_s, og, gate, mod_s, fw, t_p, seq_s)

    return (y_p.reshape(n_p, seq_p, D_MODEL), y_s.reshape(n_s, seq_s, D_MODEL),
            gla_p[None], conv_p[None], gla_s[None], conv_s[None])
```

```python
import functools

import jax
import jax.numpy as jnp
from jax import lax
from jax.experimental import pallas as pl
from jax.experimental.pallas import tpu as pltpu

f32 = jnp.float32
bf16 = jnp.bfloat16

D_MODEL = 2048
N_MOD = 6
GLA_HEADS = 4
GLA_DK = 512
GLA_DV = 1024
GLA_HK = 128
GLA_HV = 256
GLA_LOW_RANK = 16
GLA_TAU = 16.0
GLA_CHUNK = 64
CONV_CH = 1024
CONV_K = 3
N_EXPERTS = 32
TOP_K = 4
D_FF = 2048
SWIGLU_LIMIT = 7.0
SWIGLU_ALPHA = 1.702
EPS = 1e-6

LANES = 128
SUBLANES = 8
BF16_ROWS = 2 * SUBLANES
VMEM_LIMIT = 56 * 1024 * 1024

OFF_Q = 0
OFF_K = OFF_Q + GLA_DK
OFF_V = OFF_K + GLA_DK
OFF_G = OFF_V + GLA_DV
OFF_CB = OFF_G + GLA_DV
OFF_CC = OFF_CB + CONV_CH
OFF_CH = OFF_CC + CONV_CH
OFF_GA = OFF_CH + CONV_CH
OFF_GB = OFF_GA + D_MODEL
D_MAIN = OFF_GB + D_MODEL

ROUTER_PAD = -1e30
HN_BUF_FACTOR = 2

HIGHEST = lax.Precision.HIGHEST


def _cparams(sem):
    return pltpu.CompilerParams(dimension_semantics=sem, vmem_limit_bytes=VMEM_LIMIT)


def _dot(a, b):
    return jnp.dot(a, b, preferred_element_type=f32)


def _dot_nt(a, b):
    return lax.dot_general(a, b, (((1,), (1,)), ((), ())), preferred_element_type=f32)


def _dot_exact(a, b):
    return jnp.dot(a, b, precision=HIGHEST, preferred_element_type=f32)


def _sigmoid_tanh(x):
    return 0.5 * jnp.tanh(0.5 * x) + 0.5


def _rms(x):
    return x * lax.rsqrt(jnp.mean(x * x, axis=-1, keepdims=True) + EPS)


ADA_TN = 2048


def _ada_kernel(c_ref, w_ref, b_ref, o_ref):
    c = c_ref[...]
    s = (c * jax.nn.sigmoid(c)).astype(bf16)
    o_ref[...] = _dot(s, w_ref[...].astype(bf16)) + b_ref[...]


def _ada(c, w, b):
    rows = c.shape[0]
    n = w.shape[1]
    return pl.pallas_call(
        _ada_kernel,
        grid=(n // ADA_TN,),
        in_specs=[pl.BlockSpec((rows, D_MODEL), lambda j: (0, 0)),
                  pl.BlockSpec((D_MODEL, ADA_TN), lambda j: (0, j)),
                  pl.BlockSpec((1, ADA_TN), lambda j: (0, j))],
        out_specs=pl.BlockSpec((rows, ADA_TN), lambda j: (0, j)),
        out_shape=jax.ShapeDtypeStruct((rows, n), f32),
        compiler_params=_cparams(("arbitrary",)),
        name="adaln",
    )(c, w, b.reshape(1, n))


PREP_TN = 1024


def _prep_kernel(a_ref, b_ref, low_ref, main_ref, alow_ref):
    j = pl.program_id(0)
    first_shifted = OFF_G // PREP_TN

    @pl.when(j < first_shifted)
    def _():
        main_ref[...] = a_ref[...].T.astype(bf16)

    @pl.when(j >= first_shifted)
    def _():
        wide = jnp.concatenate([a_ref[...], b_ref[...]], axis=0)
        main_ref[...] = wide[GLA_LOW_RANK:GLA_LOW_RANK + PREP_TN].T.astype(bf16)

    @pl.when(j == 0)
    def _():
        low = low_ref[...].T
        lane = lax.broadcasted_iota(jnp.int32, low.shape, 1)
        alow_ref[...] = jnp.where(lane < GLA_LOW_RANK, low, 0.0).astype(bf16)


def _prep_w_in(wt):
    d = wt.shape[1]
    return pl.pallas_call(
        _prep_kernel,
        grid=(D_MAIN // PREP_TN,),
        in_specs=[pl.BlockSpec((PREP_TN, d), lambda j: (j, 0)),
                  pl.BlockSpec((GLA_LOW_RANK, d), lambda j: ((j + 1) * (PREP_TN // GLA_LOW_RANK), 0)),
                  pl.BlockSpec((LANES, d), lambda j: (OFF_G // LANES, 0))],
        out_specs=[pl.BlockSpec((d, PREP_TN), lambda j: (0, j)),
                   pl.BlockSpec((d, LANES), lambda j: (0, 0))],
        out_shape=[jax.ShapeDtypeStruct((d, D_MAIN), bf16),
                   jax.ShapeDtypeStruct((d, LANES), bf16)],
        compiler_params=_cparams(("arbitrary",)),
        name="prep_w_in",
    )(wt, wt, wt)


IN_TN = 1024


def _in_kernel(x_ref, sc_ref, sh_ref, nw_ref, w_ref, wa_ref, wgk_ref, bgk_ref,
               proj_ref, la_ref, xn_ref):
    @pl.when(pl.program_id(1) == 0)
    def _():
        xn = _rms(x_ref[...]) * nw_ref[...] * (1.0 + sc_ref[...]) + sh_ref[...]
        xnb = xn.astype(bf16)
        xn_ref[...] = xnb
        a_low = _dot(xnb, wa_ref[...])
        z = _dot(a_low.astype(bf16), wgk_ref[...]) + bgk_ref[...]
        la_ref[...] = (jnp.minimum(z, 0.0) - jnp.log(1.0 + jnp.exp(-jnp.abs(z)))) * (1.0 / GLA_TAU)

    proj_ref[...] = _dot(xn_ref[...], w_ref[...]).astype(bf16)


MOD_SH1, MOD_SC1, MOD_G1, MOD_SH2, MOD_SC2, MOD_G2 = range(N_MOD)


def _mod_spec(mod, which, tm, seq_len):
    if mod.ndim == 3:
        tiles_per_seq = seq_len // tm
        return pl.BlockSpec((None, 1, D_MODEL), lambda *g: (g[0] // tiles_per_seq, 0, which))
    return pl.BlockSpec((tm, D_MODEL), lambda *g: (g[0], which))


def _in_proj(x2, mod, nw, w_main, w_alow, w_gk, b_gk, tm, seq_len):
    t = x2.shape[0]
    return pl.pallas_call(
        _in_kernel,
        grid=(t // tm, D_MAIN // IN_TN),
        in_specs=[pl.BlockSpec((tm, D_MODEL), lambda m, n: (m, 0)),
                  _mod_spec(mod, MOD_SC1, tm, seq_len), _mod_spec(mod, MOD_SH1, tm, seq_len),
                  pl.BlockSpec((1, D_MODEL), lambda m, n: (0, 0)),
                  pl.BlockSpec((D_MODEL, IN_TN), lambda m, n: (0, n)),
                  pl.BlockSpec((D_MODEL, LANES), lambda m, n: (0, 0)),
                  pl.BlockSpec((LANES, GLA_DK), lambda m, n: (0, 0)),
                  pl.BlockSpec((1, GLA_DK), lambda m, n: (0, 0))],
        out_specs=[pl.BlockSpec((tm, IN_TN), lambda m, n: (m, n)),
                   pl.BlockSpec((tm, GLA_DK), lambda m, n: (m, 0))],
        out_shape=[jax.ShapeDtypeStruct((t, D_MAIN), bf16),
                   jax.ShapeDtypeStruct((t, GLA_DK), f32)],
        scratch_shapes=[pltpu.VMEM((tm, D_MODEL), bf16)],
        compiler_params=_cparams(("arbitrary", "arbitrary")),
        name="in_proj",
    )(x2, mod, mod, nw, w_main, w_alow, w_gk, b_gk)


GLA_R = 256


def _gla_norm_gate(o, nw, g):
    gf = g.astype(f32)
    return _rms(o) * nw * (gf * jax.nn.sigmoid(gf))


def _gla_scan_kernel(q_ref, k_ref, v_ref, g_ref, la_ref, nw_ref, s0_ref, o_ref, sout_ref, s_ref):
    step = pl.program_id(1)

    @pl.when(step == 0)
    def _():
        s_ref[...] = s0_ref[...]

    r, c = GLA_R, GLA_CHUNK
    n_chunks = r // c
    row = lax.broadcasted_iota(jnp.int32, (r, r), 0)
    col = lax.broadcasted_iota(jnp.int32, (r, r), 1)
    same_chunk = (row // c) == (col // c)
    causal = same_chunk & (col <= row)

    la = la_ref[...]
    b = _dot_exact(causal.astype(f32), la)
    b_mid = jnp.concatenate(
        [jnp.broadcast_to(b[i * c + c // 2:i * c + c // 2 + 1], (c, GLA_DK)) for i in range(n_chunks)], axis=0)
    b_last = jnp.concatenate(
        [jnp.broadcast_to(b[i * c + c - 1:i * c + c], (c, GLA_DK)) for i in range(n_chunks)], axis=0)
    q = q_ref[...].astype(f32) * (GLA_HK ** -0.5)
    k = k_ref[...].astype(f32)
    qs = (q * jnp.exp(b - b_mid)).astype(bf16)
    ks = (k * jnp.exp(b_mid - b)).astype(bf16)
    qd = (q * jnp.exp(b)).astype(bf16)
    kd_t = (k * jnp.exp(b_last - b)).T.astype(bf16)
    la_t = la.T
    lane = lax.broadcasted_iota(jnp.int32, (GLA_HK, r), 1)
    nw = nw_ref[...]

    for h in range(GLA_HEADS):
        hk = slice(h * GLA_HK, (h + 1) * GLA_HK)
        hv = slice(h * GLA_HV, (h + 1) * GLA_HV)
        v_h = v_ref[:, hv]
        scores = _dot_nt(qs[:, hk], ks[:, hk])
        o_intra = _dot(jnp.where(causal, scores, 0.0).astype(bf16), v_h)
        s = s_ref[h]
        for i in range(n_chunks):
            rows = slice(i * c, (i + 1) * c)
            in_chunk = (lane // c) == i
            o = o_intra[rows] + _dot(qd[rows, hk], s.astype(bf16))
            o_ref[rows, hv] = _gla_norm_gate(o, nw, g_ref[rows, hv]).astype(bf16)
            decay = jnp.exp(jnp.sum(jnp.where(in_chunk, la_t[hk], 0.0), axis=1, keepdims=True))
            s = decay * s + _dot(jnp.where(in_chunk, kd_t[hk], jnp.zeros_like(kd_t[hk])), v_h)
        s_ref[h] = s

    @pl.when(step == pl.num_programs(1) - 1)
    def _():
        sout_ref[...] = s_ref[...]


def _gla_scan(proj, la, nw, s0, n_seq, seq_len):
    steps = seq_len // GLA_R
    t = n_seq * seq_len

    def rows(b, s):
        return b * steps + s

    return pl.pallas_call(
        _gla_scan_kernel,
        grid=(n_seq, steps),
        in_specs=[pl.BlockSpec((GLA_R, GLA_DK), lambda b, s: (rows(b, s), OFF_Q // GLA_DK)),
                  pl.BlockSpec((GLA_R, GLA_DK), lambda b, s: (rows(b, s), OFF_K // GLA_DK)),
                  pl.BlockSpec((GLA_R, GLA_DV), lambda b, s: (rows(b, s), OFF_V // GLA_DV)),
                  pl.BlockSpec((GLA_R, GLA_DV), lambda b, s: (rows(b, s), OFF_G // GLA_DV)),
                  pl.BlockSpec((GLA_R, GLA_DK), lambda b, s: (rows(b, s), 0)),
                  pl.BlockSpec((1, GLA_HV), lambda b, s: (0, 0)),
                  pl.BlockSpec((None, GLA_HEADS, GLA_HK, GLA_HV), lambda b, s: (b, 0, 0, 0))],
        out_specs=[pl.BlockSpec((GLA_R, GLA_DV), lambda b, s: (rows(b, s), 0)),
                   pl.BlockSpec((None, GLA_HEADS, GLA_HK, GLA_HV), lambda b, s: (b, 0, 0, 0))],
        out_shape=[jax.ShapeDtypeStruct((t, GLA_DV), bf16),
                   jax.ShapeDtypeStruct((n_seq, GLA_HEADS, GLA_HK, GLA_HV), f32)],
        scratch_shapes=[pltpu.VMEM((GLA_HEADS, GLA_HK, GLA_HV), f32)],
        compiler_params=_cparams(("arbitrary", "arbitrary")),
        name="gla_scan",
    )(proj, proj, proj, proj, la, nw, s0)


GS_SEQ = 4
GS_ROWS = 128
GS_B = GS_ROWS // GS_SEQ
GS_STEP_B = 4


def _gla_step_kernel(q_ref, k_ref, v_ref, g_ref, la_ref, nw_ref, s0_ref, o_ref, sout_ref,
                     oacc_ref, qd_ref, kdt_ref, lat_ref):
    j = pl.program_id(1)
    r = GS_ROWS
    row = lax.broadcasted_iota(jnp.int32, (r, r), 0)
    col = lax.broadcasted_iota(jnp.int32, (r, r), 1)

    @pl.when(j == 0)
    def _():
        same_seq = (row // GS_SEQ) == (col // GS_SEQ)
        causal = same_seq & (col <= row)
        la = la_ref[...]
        b = _dot_exact(causal.astype(f32), la)
        b_mid = _dot_exact((same_seq & (col % GS_SEQ <= GS_SEQ // 2)).astype(f32), la)
        b_last = _dot_exact(same_seq.astype(f32), la)
        q = q_ref[...].astype(f32) * (GLA_HK ** -0.5)
        k = k_ref[...].astype(f32)
        qs = (q * jnp.exp(b - b_mid)).astype(bf16)
        ks = (k * jnp.exp(b_mid - b)).astype(bf16)
        qd_ref[...] = (q * jnp.exp(b)).astype(bf16)
        kdt_ref[...] = (k * jnp.exp(b_last - b)).T.astype(bf16)
        lat_ref[...] = la.T
        for h in range(GLA_HEADS):
            hk = slice(h * GLA_HK, (h + 1) * GLA_HK)
            hv = slice(h * GLA_HV, (h + 1) * GLA_HV)
            scores = _dot_nt(qs[:, hk], ks[:, hk])
            oacc_ref[:, hv] = _dot(jnp.where(causal, scores, 0.0).astype(bf16), v_ref[:, hv])

    for bb in range(GS_STEP_B):
        seq = j * GS_STEP_B + bb
        in_seq_lane = (col // GS_SEQ) == seq
        in_seq_row = (row[:, :GLA_HK] // GS_SEQ) == seq
        for h in range(GLA_HEADS):
            hk = slice(h * GLA_HK, (h + 1) * GLA_HK)
            hv = slice(h * GLA_HV, (h + 1) * GLA_HV)
            s0 = s0_ref[bb, h]
            kd_t = kdt_ref[hk, :]
            ds = _dot(jnp.where(in_seq_lane, kd_t, jnp.zeros_like(kd_t)), v_ref[:, hv])
            decay = jnp.exp(jnp.sum(jnp.where(in_seq_lane, lat_ref[hk, :], 0.0), axis=1, keepdims=True))
            sout_ref[bb, h] = decay * s0 + ds
            qd = qd_ref[:, hk]
            oacc_ref[:, hv] += _dot(jnp.where(in_seq_row, qd, jnp.zeros_like(qd)), s0.astype(bf16))

    @pl.when(j == pl.num_programs(1) - 1)
    def _():
        nw = nw_ref[...]
        for h in range(GLA_HEADS):
            hv = slice(h * GLA_HV, (h + 1) * GLA_HV)
            o_ref[:, hv] = _gla_norm_gate(oacc_ref[:, hv], nw, g_ref[:, hv]).astype(bf16)


def _gla_step(proj, la, nw, s0):
    n_seq = s0.shape[0]
    t = n_seq * GS_SEQ
    groups = t // GS_ROWS
    steps = GS_B // GS_STEP_B
    return pl.pallas_call(
        _gla_step_kernel,
        grid=(groups, steps),
        in_specs=[pl.BlockSpec((GS_ROWS, GLA_DK), lambda g, j: (g, OFF_Q // GLA_DK)),
                  pl.BlockSpec((GS_ROWS, GLA_DK), lambda g, j: (g, OFF_K // GLA_DK)),
                  pl.BlockSpec((GS_ROWS, GLA_DV), lambda g, j: (g, OFF_V // GLA_DV)),
                  pl.BlockSpec((GS_ROWS, GLA_DV), lambda g, j: (g, OFF_G // GLA_DV)),
                  pl.BlockSpec((GS_ROWS, GLA_DK), lambda g, j: (g, 0)),
                  pl.BlockSpec((1, GLA_HV), lambda g, j: (0, 0)),
                  pl.BlockSpec((GS_STEP_B, GLA_HEADS, GLA_HK, GLA_HV), lambda g, j: (g * steps + j, 0, 0, 0))],
        out_specs=[pl.BlockSpec((GS_ROWS, GLA_DV), lambda g, j: (g, 0)),
                   pl.BlockSpec((GS_STEP_B, GLA_HEADS, GLA_HK, GLA_HV), lambda g, j: (g * steps + j, 0, 0, 0))],
        out_shape=[jax.ShapeDtypeStruct((t, GLA_DV), bf16),
                   jax.ShapeDtypeStruct((n_seq, GLA_HEADS, GLA_HK, GLA_HV), f32)],
        scratch_shapes=[pltpu.VMEM((GS_ROWS, GLA_DV), f32),
                        pltpu.VMEM((GS_ROWS, GLA_DK), bf16),
                        pltpu.VMEM((GLA_DK, GS_ROWS), bf16),
                        pltpu.VMEM((GLA_DK, GS_ROWS), f32)],
        compiler_params=_cparams(("arbitrary", "arbitrary")),
        name="gla_step",
    )(proj, proj, proj, proj, la, nw, s0)


POST_TM = 256
HALO = BF16_ROWS
U_TAIL = SUBLANES


def _post_kernel(long_seq, tiles_per_seq, *refs):
    if long_seq:
        (x_ref, oa_ref, cb_ref, cc_ref, ch_ref, ga_ref, gb_ref, hcc_ref, hch_ref, cbuf_ref,
         wc_ref, g1_ref, sc_ref, sh_ref, nw_ref, wo_ref, wrh_ref, wrl_ref, br_ref, _hn_alias,
         h_ref, hn_ref, gate_ref, eid_ref, u_ref) = refs
    else:
        (x_ref, oa_ref, cb_ref, cc_ref, ch_ref, ga_ref, gb_ref, p0_ref, p1_ref,
         wc_ref, g1_ref, sc_ref, sh_ref, nw_ref, wo_ref, wrh_ref, wrl_ref, br_ref, _hn_alias,
         h_ref, hn_ref, gate_ref, eid_ref, u_ref) = refs
    tm = x_ref.shape[0]
    u = cc_ref[...].astype(f32) * ch_ref[...].astype(f32)
    row = lax.broadcasted_iota(jnp.int32, (tm, CONV_CH), 0)
    if long_seq:
        pos = row
        first = (pl.program_id(0) % tiles_per_seq) == 0
        halo = hcc_ref[HALO - 2:HALO, :].astype(f32) * hch_ref[HALO - 2:HALO, :].astype(f32)
        cbuf = cbuf_ref[...]
        p0 = jnp.where(first, cbuf[0:1], halo[0:1])
        p1 = jnp.where(first, cbuf[1:2], halo[1:2])
        u_ref[...] = u[tm - U_TAIL:tm]
    else:
        pos = row % GS_SEQ
        p0 = p0_ref[...]
        p1 = p1_ref[...]
        u_ref[...] = u
    u1 = jnp.where(pos == 0, p1, pltpu.roll(u, 1, 0))
    u2 = jnp.where(pos == 0, p0, jnp.where(pos == 1, p1, pltpu.roll(u, 2, 0)))
    wc = wc_ref[...]
    conv = wc[0:1] * u2 + wc[1:2] * u1 + wc[2:3] * u
    ob = (cb_ref[...].astype(f32) * conv).astype(bf16)

    ya = _dot(oa_ref[...], wo_ref[0:GLA_DV, :])
    yb = _dot(ob, wo_ref[GLA_DV:GLA_DV + CONV_CH, :])
    y = _sigmoid_tanh(ga_ref[...].astype(f32)) * ya + _sigmoid_tanh(gb_ref[...].astype(f32)) * yb
    h = x_ref[...] + g1_ref[...] * y
    h_ref[...] = h
    hn = _rms(h) * nw_ref[...] * (1.0 + sc_ref[...]) + sh_ref[...]
    hn_hi = hn.astype(bf16)
    hn_ref[...] = hn_hi
    hn_lo = (hn - hn_hi.astype(f32)).astype(bf16)
    lg = (_dot(hn_hi, wrh_ref[...]) + _dot(hn_hi, wrl_ref[...]) + _dot(hn_lo, wrh_ref[...])
          + br_ref[...])

    lane = lax.broadcasted_iota(jnp.int32, (tm, LANES), 1)
    lane_f = lane.astype(f32)
    vals, idxs = [], []
    for _ in range(TOP_K):
        m = jnp.max(lg, axis=-1, keepdims=True)
        idx = jnp.min(jnp.where(lg == m, lane_f, float(LANES)), axis=-1, keepdims=True)
        vals.append(m)
        idxs.append(idx)
        lg = jnp.where(lane_f == idx, -jnp.inf, lg)
    exps = [jnp.exp(v - vals[0]) for v in vals]
    den = exps[0] + exps[1] + exps[2] + exps[3]
    gate = jnp.zeros((tm, LANES), f32)
    eid = jnp.zeros((tm, LANES), f32)
    for j in range(TOP_K):
        gate = jnp.where(lane == j, exps[j] / den, gate)
        eid = jnp.where(lane == j, idxs[j], eid)
    gate_ref[...] = gate
    eid_ref[...] = eid.astype(jnp.int32)


def _post(x2, oa, proj, prev, wc, mod, nw, wo, wrh, wrl, br, hn_buf, row0, long_seq, seq_len):
    t = x2.shape[0]
    tm = POST_TM
    n_tiles = t // tm
    blk0 = row0 // tm
    tiles_per_seq = seq_len // tm if long_seq else 1

    def colblk(off, width):
        return pl.BlockSpec((tm, width), lambda m: (m, off // width))

    const = lambda shape: pl.BlockSpec(shape, lambda m: tuple(0 for _ in shape))
    if long_seq:
        halo_rows = lambda m: jnp.maximum(m * (tm // HALO) - 1, 0)
        prev_specs = [pl.BlockSpec((HALO, CONV_CH), lambda m: (halo_rows(m), OFF_CC // CONV_CH)),
                      pl.BlockSpec((HALO, CONV_CH), lambda m: (halo_rows(m), OFF_CH // CONV_CH)),
                      pl.BlockSpec((None, CONV_K - 1, CONV_CH), lambda m: (m // tiles_per_seq, 0, 0))]
        prev_args = [proj, proj, prev]
        u_spec = pl.BlockSpec((None, U_TAIL, CONV_CH), lambda m: (m, 0, 0))
        u_shape = jax.ShapeDtypeStruct((n_tiles, U_TAIL, CONV_CH), f32)
    else:
        prev_specs = [pl.BlockSpec((tm, CONV_CH), lambda m: (m, 0))] * 2
        prev_args = list(prev)
        u_spec = pl.BlockSpec((tm, CONV_CH), lambda m: (m, 0))
        u_shape = jax.ShapeDtypeStruct((t, CONV_CH), f32)
    n_in = 7 + len(prev_specs) + 10
    return pl.pallas_call(
        functools.partial(_post_kernel, long_seq, tiles_per_seq),
        grid=(n_tiles,),
        in_specs=[pl.BlockSpec((tm, D_MODEL), lambda m: (m, 0)),
                  pl.BlockSpec((tm, GLA_DV), lambda m: (m, 0)),
                  colblk(OFF_CB, CONV_CH), colblk(OFF_CC, CONV_CH), colblk(OFF_CH, CONV_CH),
                  colblk(OFF_GA, D_MODEL), colblk(OFF_GB, D_MODEL)]
                 + prev_specs
                 + [const((CONV_K, CONV_CH)), _mod_spec(mod, MOD_G1, tm, seq_len),
                    _mod_spec(mod, MOD_SC2, tm, seq_len), _mod_spec(mod, MOD_SH2, tm, seq_len), const((1, D_MODEL)),
                    const((GLA_DV + CONV_CH, D_MODEL)), const((D_MODEL, LANES)), const((D_MODEL, LANES)),
                    const((1, LANES)), pl.BlockSpec(memory_space=pl.ANY)],
        out_specs=[pl.BlockSpec((tm, D_MODEL), lambda m: (m, 0)),
                   pl.BlockSpec((tm, D_MODEL), lambda m: (blk0 + m, 0)),
                   pl.BlockSpec((tm, LANES), lambda m: (m, 0)),
                   pl.BlockSpec((tm, LANES), lambda m: (m, 0)),
                   u_spec],
        out_shape=[jax.ShapeDtypeStruct((t, D_MODEL), f32),
                   jax.ShapeDtypeStruct(hn_buf.shape, bf16),
                   jax.ShapeDtypeStruct((t, LANES), f32),
                   jax.ShapeDtypeStruct((t, LANES), jnp.int32),
                   u_shape],
        input_output_aliases={n_in - 1: 1},
        compiler_params=_cparams(("arbitrary",)),
        name="post_mix",
    )(x2, oa, proj, proj, proj, proj, proj, *prev_args, wc, mod, mod, mod, nw, wo, wrh, wrl, br, hn_buf)


MOE_RB = 128
MOE_TM = 1536
MOE_TN = 512
MOE_KB = MOE_TM // MOE_RB
MOE_NT = D_FF // MOE_TN


def _moe_kernel(ie_ref, ib_ref, in_ref, iv_ref, iz_ref,
                x_hbm, wg_ref, wl_ref, wd_ref, bg_ref, bl_ref, bd_ref, out_hbm,
                xbuf, hbuf, obuf, wcast, sem_in, sem_out):
    i = pl.program_id(0)
    s = pl.program_id(1)
    n_items = pl.num_programs(0)
    nb = in_ref[i]

    def x_copy(item, jb, k):
        return pltpu.make_async_copy(
            x_hbm.at[pl.ds((ib_ref[item] + jb) * MOE_RB, MOE_RB), pl.ds(k * MOE_TN, MOE_TN)],
            xbuf.at[k, pl.ds(jb * MOE_RB, MOE_RB)], sem_in)

    def out_copy(item, jb, k):
        return pltpu.make_async_copy(
            obuf.at[k, pl.ds(jb * MOE_RB, MOE_RB)],
            out_hbm.at[pl.ds((ib_ref[item] + jb) * MOE_RB, MOE_RB), pl.ds(k * MOE_TN, MOE_TN)], sem_out)

    def for_blocks(n, fn):
        def body(jb, carry):
            for k in range(MOE_NT):
                fn(jb, k)
            return carry

        lax.fori_loop(0, n, body, 0)

    def for_rows(fn):
        def body(j, carry):
            fn(pl.multiple_of(j * (4 * MOE_RB), 4 * MOE_RB), 4 * MOE_RB)
            return carry

        lax.fori_loop(0, nb // 4, body, 0)
        base = (nb // 4) * (4 * MOE_RB)
        rem = nb % 4

        @pl.when(rem >= 2)
        def _():
            fn(pl.multiple_of(base, MOE_RB), 2 * MOE_RB)

        @pl.when(rem % 2 == 1)
        def _():
            fn(pl.multiple_of(base + (rem // 2) * (2 * MOE_RB), MOE_RB), MOE_RB)

    @pl.when(s == 0)
    def _():
        @pl.when(i == 0)
        def _():
            for_blocks(nb, lambda jb, k: x_copy(i, jb, k).start())

        for_blocks(nb, lambda jb, k: x_copy(i, jb, k).wait())

    @pl.when(s < MOE_NT)
    def _():
        wcast[:, 0:MOE_TN] = wg_ref[...].astype(bf16)
        wcast[:, MOE_TN:2 * MOE_TN] = wl_ref[...].astype(bf16)
        bg = bg_ref[...]
        bl = bl_ref[...]

        def up(r0, size):
            rows = pl.ds(r0, size)
            xb = jnp.concatenate([xbuf[k, rows, :] for k in range(MOE_NT)], axis=1)
            h = _dot(xb, wcast[...])
            hg = jnp.minimum(h[:, 0:MOE_TN] + bg, SWIGLU_LIMIT)
            hl = jnp.clip(h[:, MOE_TN:2 * MOE_TN] + bl, -SWIGLU_LIMIT, SWIGLU_LIMIT)
            hbuf[s, rows, :] = ((hl + 1.0) * hg * jax.nn.sigmoid(SWIGLU_ALPHA * hg)).astype(bf16)

        for_rows(up)

    @pl.when(s == MOE_NT)
    def _():
        @pl.when(i > 0)
        def _():
            prev = jnp.maximum(i - 1, 0)
            for_blocks(in_ref[prev], lambda jb, k: out_copy(prev, jb, k).wait())

        def fill(jb, k):
            obuf[k, pl.ds(pl.multiple_of(jb * MOE_RB, MOE_RB), MOE_RB), :] = jnp.zeros((MOE_RB, MOE_TN), bf16)
            out_copy(i, jb, k).start()

        nz = iz_ref[i]
        for_blocks(nz, fill)
        for_blocks(nz, lambda jb, k: out_copy(i, jb, k).wait())

        @pl.when(i + 1 < n_items)
        def _():
            nxt = jnp.minimum(i + 1, n_items - 1)
            for_blocks(in_ref[nxt], lambda jb, k: x_copy(nxt, jb, k).start())

    @pl.when(s >= MOE_NT)
    def _():
        wcast[:, 0:MOE_TN] = wd_ref[...].astype(bf16)
        bd = bd_ref[...]

        def down(r0, size):
            rows = pl.ds(r0, size)
            hb = jnp.concatenate([hbuf[k, rows, :] for k in range(MOE_NT)], axis=1)
            obuf[s - MOE_NT, rows, :] = (_dot(hb, wcast[:, 0:MOE_TN]) + bd).astype(bf16)

        for_rows(down)

    @pl.when(s == 2 * MOE_NT - 1)
    def _():
        for_blocks(nb, lambda jb, k: out_copy(i, jb, k).start())

        @pl.when(i == n_items - 1)
        def _():
            for_blocks(nb, lambda jb, k: out_copy(i, jb, k).wait())


def _moe(xs, items, w_up, b_up, w_down, b_down):
    rows = xs.shape[0]
    n_items = items[0].shape[0]

    last = MOE_NT - 1

    def up_tile(i, s, tabs):
        valid = tabs[3][i]
        return jnp.minimum(s, last) * valid + last * (1 - valid)

    def down_tile(i, s, tabs):
        valid = tabs[3][i]
        return jnp.maximum(s - MOE_NT, 0) * valid + last * (1 - valid)

    def expert(i, tabs):
        return tabs[0][i]

    grid_spec = pltpu.PrefetchScalarGridSpec(
        num_scalar_prefetch=len(items),
        grid=(n_items, 2 * MOE_NT),
        in_specs=[pl.BlockSpec(memory_space=pl.ANY),
                  pl.BlockSpec((None, D_MODEL, MOE_TN), lambda i, s, *t: (expert(i, t), 0, up_tile(i, s, t))),
                  pl.BlockSpec((None, D_MODEL, MOE_TN), lambda i, s, *t: (expert(i, t), 0, MOE_NT + up_tile(i, s, t))),
                  pl.BlockSpec((None, D_FF, MOE_TN), lambda i, s, *t: (expert(i, t), 0, down_tile(i, s, t))),
                  pl.BlockSpec((None, 1, MOE_TN), lambda i, s, *t: (expert(i, t), 0, up_tile(i, s, t))),
                  pl.BlockSpec((None, 1, MOE_TN), lambda i, s, *t: (expert(i, t), 0, MOE_NT + up_tile(i, s, t))),
                  pl.BlockSpec((None, 1, MOE_TN), lambda i, s, *t: (expert(i, t), 0, down_tile(i, s, t)))],
        out_specs=pl.BlockSpec(memory_space=pl.ANY),
        scratch_shapes=[pltpu.VMEM((MOE_NT, MOE_TM, MOE_TN), bf16),
                        pltpu.VMEM((MOE_NT, MOE_TM, MOE_TN), bf16),
                        pltpu.VMEM((MOE_NT, MOE_TM, MOE_TN), bf16),
                        pltpu.VMEM((D_MODEL, 2 * MOE_TN), bf16),
                        pltpu.SemaphoreType.DMA(()),
                        pltpu.SemaphoreType.DMA(())],
    )
    return pl.pallas_call(
        _moe_kernel,
        grid_spec=grid_spec,
        out_shape=jax.ShapeDtypeStruct((rows, D_MODEL), bf16),
        compiler_params=_cparams(("arbitrary", "arbitrary")),
        name="moe_experts",
    )(*items,
      xs, w_up, w_up, w_down,
      b_up.reshape(N_EXPERTS, 1, 2 * D_FF), b_up.reshape(N_EXPERTS, 1, 2 * D_FF),
      b_down.reshape(N_EXPERTS, 1, D_MODEL))


FIN_TM = 512


def _final_kernel(h_ref, og_ref, gate_ref, g2_ref, fw_ref, y_ref):
    gate = gate_ref[...]
    ff = og_ref[0].astype(f32) * gate[:, 0:1]
    for j in range(1, TOP_K):
        ff = ff + og_ref[j].astype(f32) * gate[:, j:j + 1]
    h = h_ref[...] + g2_ref[...] * ff
    y_ref[...] = _rms(h) * fw_ref[...]


def _final(h, og, gate, mod, fw, row0, seq_len):
    t = h.shape[0]
    tm = FIN_TM
    blk0 = row0 // tm
    return pl.pallas_call(
        _final_kernel,
        grid=(t // tm,),
        in_specs=[pl.BlockSpec((tm, D_MODEL), lambda m: (m, 0)),
                  pl.BlockSpec((TOP_K, tm, D_MODEL), lambda m: (0, blk0 + m, 0)),
                  pl.BlockSpec((tm, LANES), lambda m: (blk0 + m, 0)),
                  _mod_spec(mod, MOD_G2, tm, seq_len),
                  pl.BlockSpec((1, D_MODEL), lambda m: (0, 0))],
        out_specs=pl.BlockSpec((tm, D_MODEL), lambda m: (m, 0)),
        out_shape=jax.ShapeDtypeStruct((t, D_MODEL), f32),
        compiler_params=_cparams(("arbitrary",)),
        name="final_norm",
    )(h, og, gate, mod, fw)


def _count_le(bounds, x):
    return jnp.sum((bounds[None, :] <= x[:, None]).astype(jnp.int32), axis=1)


def _routing(top_e):
    t = top_e.shape[0]
    n_pairs = t * TOP_K
    e_flat = top_e.reshape(-1)
    onehot = (e_flat[:, None] == jnp.arange(N_EXPERTS, dtype=e_flat.dtype)[None, :]).astype(jnp.int32)
    csum = jnp.cumsum(onehot, axis=0)
    counts = csum[-1]
    nblk = (counts + MOE_RB - 1) // MOE_RB
    blk_end = jnp.cumsum(nblk)
    blk_start = blk_end - nblk
    row_start = blk_start * MOE_RB
    dest = jnp.sum(onehot * (csum + row_start[None, :]), axis=1) - 1
    n_rows = (n_pairs // MOE_RB + N_EXPERTS) * MOE_RB
    order = jnp.argsort(e_flat, stable=True).astype(jnp.int32)
    raw_start = jnp.cumsum(counts) - counts
    r = jnp.arange(n_rows, dtype=jnp.int32)
    begun = row_start[None, :] <= r[:, None]

    def at_row(table):
        steps = jnp.diff(table, prepend=0)
        return jnp.sum(jnp.where(begun, steps[None, :], 0), axis=1)

    src_pos = r - at_row(row_start - raw_start)
    is_real = r < at_row(row_start + counts)
    src_pair = order[jnp.clip(src_pos, 0, n_pairs - 1)]
    src_tok = jnp.where(is_real, src_pair // TOP_K, r % t).astype(jnp.int32)

    n_items = N_EXPERTS + n_rows // MOE_TM
    nit = (nblk + MOE_KB - 1) // MOE_KB
    it_end = jnp.cumsum(nit)
    it_start = it_end - nit
    idx = jnp.arange(n_items, dtype=jnp.int32)
    valid = idx < it_end[-1]
    last_valid = jnp.maximum(it_end[-1] - 1, 0)
    idx_c = jnp.minimum(idx, last_valid)
    e_of = jnp.minimum(_count_le(it_end, idx_c), N_EXPERTS - 1).astype(jnp.int32)
    k = idx_c - it_start[e_of]
    item_nb = jnp.where(valid, jnp.clip(nblk[e_of] - k * MOE_KB, 0, MOE_KB), 0).astype(jnp.int32)
    spare = idx - it_end[-1]
    tail_blk = blk_end[-1] + spare * MOE_KB
    item_nz = jnp.where(valid, 0, jnp.clip(n_rows // MOE_RB - tail_blk, 0, MOE_KB)).astype(jnp.int32)
    item_blk = jnp.where(valid, blk_start[e_of] + k * MOE_KB, tail_blk).astype(jnp.int32)
    items = (e_of, item_blk, item_nb, valid.astype(jnp.int32), item_nz)
    return dest.reshape(t, TOP_K), src_tok, items


def kernel(x_prompt, x_sample, state_gla, state_conv, c_prompt, c_sample, w_ada, b_ada, norm1_w,
           w_in, w_gk_up, b_gk, gla_norm_w, w_conv, w_out, norm2_w, w_router, b_router, w_up, b_up,
           w_down, b_down, final_norm_w):
    n_p, seq_p, _ = x_prompt.shape
    n_s, seq_s, _ = x_sample.shape
    t_p, t_s = n_p * seq_p, n_s * seq_s
    assert seq_s == GS_SEQ and w_ada.shape[0] == 1

    c_all = jnp.concatenate([c_prompt, c_sample], axis=0)
    pad = (-c_all.shape[0]) % BF16_ROWS
    c_all = jnp.pad(c_all, ((0, pad), (0, 0)))
    mod = _ada(c_all, w_ada[0], b_ada[0])
    mod_p = mod[:n_p].reshape(n_p, 1, N_MOD * D_MODEL)
    mod_s = jnp.repeat(mod[n_p:n_p + n_s], seq_s, axis=0)

    w_main, w_alow = _prep_w_in(w_in[0].T)
    w_gk = jnp.pad(w_gk_up[0], ((0, LANES - GLA_LOW_RANK), (0, 0))).astype(bf16)
    bgk = b_gk[0].reshape(1, GLA_DK)
    wo = w_out[0].astype(bf16)
    wr = jnp.pad(w_router[0], ((0, 0), (0, LANES - N_EXPERTS)))
    wr_hi = wr.astype(bf16)
    wr_lo = (wr - wr_hi.astype(f32)).astype(bf16)
    br = jnp.pad(b_router[0], (0, LANES - N_EXPERTS), constant_values=ROUTER_PAD).reshape(1, LANES)
    n1w = norm1_w[0].reshape(1, D_MODEL)
    n2w = norm2_w[0].reshape(1, D_MODEL)
    gnw = gla_norm_w[0].reshape(1, GLA_HV)
    fw = final_norm_w.reshape(1, D_MODEL)
    wc = w_conv[0]

    xp = x_prompt.reshape(t_p, D_MODEL)
    xs_ = x_sample.reshape(t_s, D_MODEL)

    proj_p, la_p = _in_proj(xp, mod_p, n1w, w_main, w_alow, w_gk, bgk, 1024, seq_p)
    proj_s, la_s = _in_proj(xs_, mod_s, n1w, w_main, w_alow, w_gk, bgk, t_s, seq_s)

    gla0 = jnp.zeros((n_p, GLA_HEADS, GLA_HK, GLA_HV), f32)
    conv0 = jnp.zeros((n_p, CONV_K - 1, CONV_CH), f32)
    oa_p, gla_p = _gla_scan(proj_p, la_p, gnw, gla0, n_p, seq_p)
    oa_s, gla_s = _gla_step(proj_s, la_s, gnw, state_gla[0])

    hn = jnp.zeros((HN_BUF_FACTOR * (t_p + t_s), D_MODEL), bf16)
    h_p, hn, gate_p, eid_p, ut_p = _post(xp, oa_p, proj_p, conv0, wc, mod_p, n2w, wo,
                                         wr_hi, wr_lo, br, hn, 0, True, seq_p)
    prev_s = (jnp.repeat(state_conv[0][:, 0], seq_s, axis=0), jnp.repeat(state_conv[0][:, 1], seq_s, axis=0))
    h_s, hn, gate_s, eid_s, u_s = _post(xs_, oa_s, proj_s, prev_s, wc, mod_s, n2w, wo,
                                        wr_hi, wr_lo, br, hn, t_p, False, seq_s)
    conv_p = ut_p.reshape(n_p, seq_p // POST_TM, U_TAIL, CONV_CH)[:, -1, U_TAIL - (CONV_K - 1):]
    conv_s = u_s.reshape(n_s, seq_s, CONV_CH)[:, seq_s - (CONV_K - 1):]

    gate = jnp.concatenate([gate_p, gate_s], axis=0)
    top_e = jnp.concatenate([eid_p, eid_s], axis=0)[:, :TOP_K]
    dest, src_tok, items = _routing(top_e)
    x_rows = hn[src_tok]
    out_rows = _moe(x_rows, items, w_up[0], b_up[0], w_down[0], b_down[0])
    og = out_rows[dest.T.reshape(-1)].reshape(TOP_K, t_p + t_s, D_MODEL)

    y_p = _final(h_p, og, gate, mod_p, fw, 0, seq_p)
    y_s = _final(h_s, og, gate, mod_s, fw, t_p, seq_s)

    return (y_p.reshape(n_p, seq_p, D_MODEL), y_s.reshape(n_s, seq_s, D_MODEL),
            gla_p[None], conv_p[None], gla_s[None], conv_s[None])
```

```python
import functools

import jax
import jax.numpy as jnp
from jax import lax
from jax.experimental import pallas as pl
from jax.experimental.pallas import tpu as pltpu

f32 = jnp.float32
bf16 = jnp.bfloat16

D_MODEL = 2048
N_MOD = 6
GLA_HEADS = 4
GLA_DK = 512
GLA_DV = 1024
GLA_HK = 128
GLA_HV = 256
GLA_LOW_RANK = 16
GLA_TAU = 16.0
GLA_CHUNK = 64
CONV_CH = 1024
CONV_K = 3
N_EXPERTS = 32
TOP_K = 4
D_FF = 2048
SWIGLU_LIMIT = 7.0
SWIGLU_ALPHA = 1.702
EPS = 1e-6

LANES = 128
SUBLANES = 8
BF16_ROWS = 2 * SUBLANES
VMEM_LIMIT = 56 * 1024 * 1024

OFF_Q = 0
OFF_K = OFF_Q + GLA_DK
OFF_V = OFF_K + GLA_DK
OFF_G = OFF_V + GLA_DV
OFF_CB = OFF_G + GLA_DV
OFF_CC = OFF_CB + CONV_CH
OFF_CH = OFF_CC + CONV_CH
OFF_GA = OFF_CH + CONV_CH
OFF_GB = OFF_GA + D_MODEL
D_MAIN = OFF_GB + D_MODEL

ROUTER_PAD = -1e30
HN_BUF_FACTOR = 2

HIGHEST = lax.Precision.HIGHEST


def _cparams(sem):
    return pltpu.CompilerParams(dimension_semantics=sem, vmem_limit_bytes=VMEM_LIMIT)


def _dot(a, b):
    return jnp.dot(a, b, preferred_element_type=f32)


def _dot_nt(a, b):
    return lax.dot_general(a, b, (((1,), (1,)), ((), ())), preferred_element_type=f32)


def _dot_exact(a, b):
    return jnp.dot(a, b, precision=HIGHEST, preferred_element_type=f32)


def _sigmoid_tanh(x):
    return 0.5 * jnp.tanh(0.5 * x) + 0.5


def _rms(x):
    return x * lax.rsqrt(jnp.mean(x * x, axis=-1, keepdims=True) + EPS)


ADA_TN = 2048


def _ada_kernel(c_ref, w_ref, b_ref, o_ref):
    c = c_ref[...]
    s = (c * jax.nn.sigmoid(c)).astype(bf16)
    o_ref[...] = _dot(s, w_ref[...].astype(bf16)) + b_ref[...]


def _ada(c, w, b):
    rows = c.shape[0]
    n = w.shape[1]
    return pl.pallas_call(
        _ada_kernel,
        grid=(n // ADA_TN,),
        in_specs=[pl.BlockSpec((rows, D_MODEL), lambda j: (0, 0)),
                  pl.BlockSpec((D_MODEL, ADA_TN), lambda j: (0, j)),
                  pl.BlockSpec((1, ADA_TN), lambda j: (0, j))],
        out_specs=pl.BlockSpec((rows, ADA_TN), lambda j: (0, j)),
        out_shape=jax.ShapeDtypeStruct((rows, n), f32),
        compiler_params=_cparams(("arbitrary",)),
        name="adaln",
    )(c, w, b.reshape(1, n))


PREP_TN = 1024


def _prep_kernel(a_ref, b_ref, low_ref, main_ref, alow_ref):
    j = pl.program_id(0)
    first_shifted = OFF_G // PREP_TN

    @pl.when(j < first_shifted)
    def _():
        main_ref[...] = a_ref[...].T.astype(bf16)

    @pl.when(j >= first_shifted)
    def _():
        wide = jnp.concatenate([a_ref[...], b_ref[...]], axis=0)
        main_ref[...] = wide[GLA_LOW_RANK:GLA_LOW_RANK + PREP_TN].T.astype(bf16)

    @pl.when(j == 0)
    def _():
        low = low_ref[...].T
        lane = lax.broadcasted_iota(jnp.int32, low.shape, 1)
        alow_ref[...] = jnp.where(lane < GLA_LOW_RANK, low, 0.0).astype(bf16)


def _prep_w_in(wt):
    d = wt.shape[1]
    return pl.pallas_call(
        _prep_kernel,
        grid=(D_MAIN // PREP_TN,),
        in_specs=[pl.BlockSpec((PREP_TN, d), lambda j: (j, 0)),
                  pl.BlockSpec((GLA_LOW_RANK, d), lambda j: ((j + 1) * (PREP_TN // GLA_LOW_RANK), 0)),
                  pl.BlockSpec((LANES, d), lambda j: (OFF_G // LANES, 0))],
        out_specs=[pl.BlockSpec((d, PREP_TN), lambda j: (0, j)),
                   pl.BlockSpec((d, LANES), lambda j: (0, 0))],
        out_shape=[jax.ShapeDtypeStruct((d, D_MAIN), bf16),
                   jax.ShapeDtypeStruct((d, LANES), bf16)],
        compiler_params=_cparams(("arbitrary",)),
        name="prep_w_in",
    )(wt, wt, wt)


IN_TN = 1024


def _in_kernel(x_ref, sc_ref, sh_ref, nw_ref, w_ref, wa_ref, wgk_ref, bgk_ref,
               proj_ref, la_ref, xn_ref):
    @pl.when(pl.program_id(1) == 0)
    def _():
        xn = _rms(x_ref[...]) * nw_ref[...] * (1.0 + sc_ref[...]) + sh_ref[...]
        xnb = xn.astype(bf16)
        xn_ref[...] = xnb
        a_low = _dot(xnb, wa_ref[...])
        z = _dot(a_low.astype(bf16), wgk_ref[...]) + bgk_ref[...]
        la_ref[...] = (jnp.minimum(z, 0.0) - jnp.log(1.0 + jnp.exp(-jnp.abs(z)))) * (1.0 / GLA_TAU)

    proj_ref[...] = _dot(xn_ref[...], w_ref[...]).astype(bf16)


MOD_SH1, MOD_SC1, MOD_G1, MOD_SH2, MOD_SC2, MOD_G2 = range(N_MOD)


def _mod_spec(mod, which, tm, seq_len):
    if mod.ndim == 3:
        tiles_per_seq = seq_len // tm
        return pl.BlockSpec((None, 1, D_MODEL), lambda *g: (g[0] // tiles_per_seq, 0, which))
    return pl.BlockSpec((tm, D_MODEL), lambda *g: (g[0], which))


def _in_proj(x2, mod, nw, w_main, w_alow, w_gk, b_gk, tm, seq_len):
    t = x2.shape[0]
    return pl.pallas_call(
        _in_kernel,
        grid=(t // tm, D_MAIN // IN_TN),
        in_specs=[pl.BlockSpec((tm, D_MODEL), lambda m, n: (m, 0)),
                  _mod_spec(mod, MOD_SC1, tm, seq_len), _mod_spec(mod, MOD_SH1, tm, seq_len),
                  pl.BlockSpec((1, D_MODEL), lambda m, n: (0, 0)),
                  pl.BlockSpec((D_MODEL, IN_TN), lambda m, n: (0, n)),
                  pl.BlockSpec((D_MODEL, LANES), lambda m, n: (0, 0)),
                  pl.BlockSpec((LANES, GLA_DK), lambda m, n: (0, 0)),
                  pl.BlockSpec((1, GLA_DK), lambda m, n: (0, 0))],
        out_specs=[pl.BlockSpec((tm, IN_TN), lambda m, n: (m, n)),
                   pl.BlockSpec((tm, GLA_DK), lambda m, n: (m, 0))],
        out_shape=[jax.ShapeDtypeStruct((t, D_MAIN), bf16),
                   jax.ShapeDtypeStruct((t, GLA_DK), f32)],
        scratch_shapes=[pltpu.VMEM((tm, D_MODEL), bf16)],
        compiler_params=_cparams(("arbitrary", "arbitrary")),
        name="in_proj",
    )(x2, mod, mod, nw, w_main, w_alow, w_gk, b_gk)


GLA_R = 256


def _gla_norm_gate(o, nw, g):
    gf = g.astype(f32)
    return _rms(o) * nw * (gf * jax.nn.sigmoid(gf))


def _gla_scan_kernel(q_ref, k_ref, v_ref, g_ref, la_ref, nw_ref, s0_ref, o_ref, sout_ref, s_ref):
    step = pl.program_id(1)

    @pl.when(step == 0)
    def _():
        s_ref[...] = s0_ref[...]

    r, c = GLA_R, GLA_CHUNK
    n_chunks = r // c
    row = lax.broadcasted_iota(jnp.int32, (r, r), 0)
    col = lax.broadcasted_iota(jnp.int32, (r, r), 1)
    same_chunk = (row // c) == (col // c)
    causal = same_chunk & (col <= row)

    la = la_ref[...]
    b = _dot_exact(causal.astype(f32), la)
    b_mid = jnp.concatenate(
        [jnp.broadcast_to(b[i * c + c // 2:i * c + c // 2 + 1], (c, GLA_DK)) for i in range(n_chunks)], axis=0)
    b_last = jnp.concatenate(
        [jnp.broadcast_to(b[i * c + c - 1:i * c + c], (c, GLA_DK)) for i in range(n_chunks)], axis=0)
    q = q_ref[...].astype(f32) * (GLA_HK ** -0.5)
    k = k_ref[...].astype(f32)
    qs = (q * jnp.exp(b - b_mid)).astype(bf16)
    ks = (k * jnp.exp(b_mid - b)).astype(bf16)
    qd = (q * jnp.exp(b)).astype(bf16)
    kd_t = (k * jnp.exp(b_last - b)).T.astype(bf16)
    la_t = la.T
    lane = lax.broadcasted_iota(jnp.int32, (GLA_HK, r), 1)
    nw = nw_ref[...]

    for h in range(GLA_HEADS):
        hk = slice(h * GLA_HK, (h + 1) * GLA_HK)
        hv = slice(h * GLA_HV, (h + 1) * GLA_HV)
        v_h = v_ref[:, hv]
        scores = _dot_nt(qs[:, hk], ks[:, hk])
        o_intra = _dot(jnp.where(causal, scores, 0.0).astype(bf16), v_h)
        s = s_ref[h]
        for i in range(n_chunks):
            rows = slice(i * c, (i + 1) * c)
            in_chunk = (lane // c) == i
            o = o_intra[rows] + _dot(qd[rows, hk], s.astype(bf16))
            o_ref[rows, hv] = _gla_norm_gate(o, nw, g_ref[rows, hv]).astype(bf16)
            decay = jnp.exp(jnp.sum(jnp.where(in_chunk, la_t[hk], 0.0), axis=1, keepdims=True))
            s = decay * s + _dot(jnp.where(in_chunk, kd_t[hk], jnp.zeros_like(kd_t[hk])), v_h)
        s_ref[h] = s

    @pl.when(step == pl.num_programs(1) - 1)
    def _():
        sout_ref[...] = s_ref[...]


def _gla_scan(proj, la, nw, s0, n_seq, seq_len):
    steps = seq_len // GLA_R
    t = n_seq * seq_len

    def rows(b, s):
        return b * steps + s

    return pl.pallas_call(
        _gla_scan_kernel,
        grid=(n_seq, steps),
        in_specs=[pl.BlockSpec((GLA_R, GLA_DK), lambda b, s: (rows(b, s), OFF_Q // GLA_DK)),
                  pl.BlockSpec((GLA_R, GLA_DK), lambda b, s: (rows(b, s), OFF_K // GLA_DK)),
                  pl.BlockSpec((GLA_R, GLA_DV), lambda b, s: (rows(b, s), OFF_V // GLA_DV)),
                  pl.BlockSpec((GLA_R, GLA_DV), lambda b, s: (rows(b, s), OFF_G // GLA_DV)),
                  pl.BlockSpec((GLA_R, GLA_DK), lambda b, s: (rows(b, s), 0)),
                  pl.BlockSpec((1, GLA_HV), lambda b, s: (0, 0)),
                  pl.BlockSpec((None, GLA_HEADS, GLA_HK, GLA_HV), lambda b, s: (b, 0, 0, 0))],
        out_specs=[pl.BlockSpec((GLA_R, GLA_DV), lambda b, s: (rows(b, s), 0)),
                   pl.BlockSpec((None, GLA_HEADS, GLA_HK, GLA_HV), lambda b, s: (b, 0, 0, 0))],
        out_shape=[jax.ShapeDtypeStruct((t, GLA_DV), bf16),
                   jax.ShapeDtypeStruct((n_seq, GLA_HEADS, GLA_HK, GLA_HV), f32)],
        scratch_shapes=[pltpu.VMEM((GLA_HEADS, GLA_HK, GLA_HV), f32)],
        compiler_params=_cparams(("arbitrary", "arbitrary")),
        name="gla_scan",
    )(proj, proj, proj, proj, la, nw, s0)


GS_SEQ = 4
GS_ROWS = 128
GS_B = GS_ROWS // GS_SEQ
GS_STEP_B = 4


def _gla_step_kernel(q_ref, k_ref, v_ref, g_ref, la_ref, nw_ref, s0_ref, o_ref, sout_ref,
                     oacc_ref, qd_ref, kdt_ref, lat_ref):
    j = pl.program_id(1)
    r = GS_ROWS
    row = lax.broadcasted_iota(jnp.int32, (r, r), 0)
    col = lax.broadcasted_iota(jnp.int32, (r, r), 1)

    @pl.when(j == 0)
    def _():
        same_seq = (row // GS_SEQ) == (col // GS_SEQ)
        causal = same_seq & (col <= row)
        la = la_ref[...]
        b = _dot_exact(causal.astype(f32), la)
        b_mid = _dot_exact((same_seq & (col % GS_SEQ <= GS_SEQ // 2)).astype(f32), la)
        b_last = _dot_exact(same_seq.astype(f32), la)
        q = q_ref[...].astype(f32) * (GLA_HK ** -0.5)
        k = k_ref[...].astype(f32)
        qs = (q * jnp.exp(b - b_mid)).astype(bf16)
        ks = (k * jnp.exp(b_mid - b)).astype(bf16)
        qd_ref[...] = (q * jnp.exp(b)).astype(bf16)
        kdt_ref[...] = (k * jnp.exp(b_last - b)).T.astype(bf16)
        lat_ref[...] = la.T
        for h in range(GLA_HEADS):
            hk = slice(h * GLA_HK, (h + 1) * GLA_HK)
            hv = slice(h * GLA_HV, (h + 1) * GLA_HV)
            scores = _dot_nt(qs[:, hk], ks[:, hk])
            oacc_ref[:, hv] = _dot(jnp.where(causal, scores, 0.0).astype(bf16), v_ref[:, hv])

    for bb in range(GS_STEP_B):
        seq = j * GS_STEP_B + bb
        in_seq_lane = (col // GS_SEQ) == seq
        in_seq_row = (row[:, :GLA_HK] // GS_SEQ) == seq
        for h in range(GLA_HEADS):
            hk = slice(h * GLA_HK, (h + 1) * GLA_HK)
            hv = slice(h * GLA_HV, (h + 1) * GLA_HV)
            s0 = s0_ref[bb, h]
            kd_t = kdt_ref[hk, :]
            ds = _dot(jnp.where(in_seq_lane, kd_t, jnp.zeros_like(kd_t)), v_ref[:, hv])
            decay = jnp.exp(jnp.sum(jnp.where(in_seq_lane, lat_ref[hk, :], 0.0), axis=1, keepdims=True))
            sout_ref[bb, h] = decay * s0 + ds
            qd = qd_ref[:, hk]
            oacc_ref[:, hv] += _dot(jnp.where(in_seq_row, qd, jnp.zeros_like(qd)), s0.astype(bf16))

    @pl.when(j == pl.num_programs(1) - 1)
    def _():
        nw = nw_ref[...]
        for h in range(GLA_HEADS):
            hv = slice(h * GLA_HV, (h + 1) * GLA_HV)
            o_ref[:, hv] = _gla_norm_gate(oacc_ref[:, hv], nw, g_ref[:, hv]).astype(bf16)


def _gla_step(proj, la, nw, s0):
    n_seq = s0.shape[0]
    t = n_seq * GS_SEQ
    groups = t // GS_ROWS
    steps = GS_B // GS_STEP_B
    return pl.pallas_call(
        _gla_step_kernel,
        grid=(groups, steps),
        in_specs=[pl.BlockSpec((GS_ROWS, GLA_DK), lambda g, j: (g, OFF_Q // GLA_DK)),
                  pl.BlockSpec((GS_ROWS, GLA_DK), lambda g, j: (g, OFF_K // GLA_DK)),
                  pl.BlockSpec((GS_ROWS, GLA_DV), lambda g, j: (g, OFF_V // GLA_DV)),
                  pl.BlockSpec((GS_ROWS, GLA_DV), lambda g, j: (g, OFF_G // GLA_DV)),
                  pl.BlockSpec((GS_ROWS, GLA_DK), lambda g, j: (g, 0)),
                  pl.BlockSpec((1, GLA_HV), lambda g, j: (0, 0)),
                  pl.BlockSpec((GS_STEP_B, GLA_HEADS, GLA_HK, GLA_HV), lambda g, j: (g * steps + j, 0, 0, 0))],
        out_specs=[pl.BlockSpec((GS_ROWS, GLA_DV), lambda g, j: (g, 0)),
                   pl.BlockSpec((GS_STEP_B, GLA_HEADS, GLA_HK, GLA_HV), lambda g, j: (g * steps + j, 0, 0, 0))],
        out_shape=[jax.ShapeDtypeStruct((t, GLA_DV), bf16),
                   jax.ShapeDtypeStruct((n_seq, GLA_HEADS, GLA_HK, GLA_HV), f32)],
        scratch_shapes=[pltpu.VMEM((GS_ROWS, GLA_DV), f32),
                        pltpu.VMEM((GS_ROWS, GLA_DK), bf16),
                        pltpu.VMEM((GLA_DK, GS_ROWS), bf16),
                        pltpu.VMEM((GLA_DK, GS_ROWS), f32)],
        compiler_params=_cparams(("arbitrary", "arbitrary")),
        name="gla_step",
    )(proj, proj, proj, proj, la, nw, s0)


POST_TM = 256
HALO = BF16_ROWS
U_TAIL = SUBLANES


def _post_kernel(long_seq, tiles_per_seq, *refs):
    if long_seq:
        (x_ref, oa_ref, cb_ref, cc_ref, ch_ref, ga_ref, gb_ref, hcc_ref, hch_ref, cbuf_ref,
         wc_ref, g1_ref, sc_ref, sh_ref, nw_ref, wo_ref, wrh_ref, wrl_ref, br_ref, _hn_alias,
         h_ref, hn_ref, gate_ref, eid_ref, u_ref) = refs
    else:
        (x_ref, oa_ref, cb_ref, cc_ref, ch_ref, ga_ref, gb_ref, p0_ref, p1_ref,
         wc_ref, g1_ref, sc_ref, sh_ref, nw_ref, wo_ref, wrh_ref, wrl_ref, br_ref, _hn_alias,
         h_ref, hn_ref, gate_ref, eid_ref, u_ref) = refs
    tm = x_ref.shape[0]
    u = cc_ref[...].astype(f32) * ch_ref[...].astype(f32)
    row = lax.broadcasted_iota(jnp.int32, (tm, CONV_CH), 0)
    if long_seq:
        pos = row
        first = (pl.program_id(0) % tiles_per_seq) == 0
        halo = hcc_ref[HALO - 2:HALO, :].astype(f32) * hch_ref[HALO - 2:HALO, :].astype(f32)
        cbuf = cbuf_ref[...]
        p0 = jnp.where(first, cbuf[0:1], halo[0:1])
        p1 = jnp.where(first, cbuf[1:2], halo[1:2])
        u_ref[...] = u[tm - U_TAIL:tm]
    else:
        pos = row % GS_SEQ
        p0 = p0_ref[...]
        p1 = p1_ref[...]
        u_ref[...] = u
    u1 = jnp.where(pos == 0, p1, pltpu.roll(u, 1, 0))
    u2 = jnp.where(pos == 0, p0, jnp.where(pos == 1, p1, pltpu.roll(u, 2, 0)))
    wc = wc_ref[...]
    conv = wc[0:1] * u2 + wc[1:2] * u1 + wc[2:3] * u
    ob = (cb_ref[...].astype(f32) * conv).astype(bf16)

    ya = _dot(oa_ref[...], wo_ref[0:GLA_DV, :])
    yb = _dot(ob, wo_ref[GLA_DV:GLA_DV + CONV_CH, :])
    y = _sigmoid_tanh(ga_ref[...].astype(f32)) * ya + _sigmoid_tanh(gb_ref[...].astype(f32)) * yb
    h = x_ref[...] + g1_ref[...] * y
    h_ref[...] = h
    hn = _rms(h) * nw_ref[...] * (1.0 + sc_ref[...]) + sh_ref[...]
    hn_hi = hn.astype(bf16)
    hn_ref[...] = hn_hi
    hn_lo = (hn - hn_hi.astype(f32)).astype(bf16)
    lg = (_dot(hn_hi, wrh_ref[...]) + _dot(hn_hi, wrl_ref[...]) + _dot(hn_lo, wrh_ref[...])
          + br_ref[...])

    lane = lax.broadcasted_iota(jnp.int32, (tm, LANES), 1)
    lane_f = lane.astype(f32)
    vals, idxs = [], []
    for _ in range(TOP_K):
        m = jnp.max(lg, axis=-1, keepdims=True)
        idx = jnp.min(jnp.where(lg == m, lane_f, float(LANES)), axis=-1, keepdims=True)
        vals.append(m)
        idxs.append(idx)
        lg = jnp.where(lane_f == idx, -jnp.inf, lg)
    exps = [jnp.exp(v - vals[0]) for v in vals]
    den = exps[0] + exps[1] + exps[2] + exps[3]
    gate = jnp.zeros((tm, LANES), f32)
    eid = jnp.zeros((tm, LANES), f32)
    for j in range(TOP_K):
        gate = jnp.where(lane == j, exps[j] / den, gate)
        eid = jnp.where(lane == j, idxs[j], eid)
    gate_ref[...] = gate
    eid_ref[...] = eid.astype(jnp.int32)


def _post(x2, oa, proj, prev, wc, mod, nw, wo, wrh, wrl, br, hn_buf, row0, long_seq, seq_len):
    t = x2.shape[0]
    tm = POST_TM
    n_tiles = t // tm
    blk0 = row0 // tm
    tiles_per_seq = seq_len // tm if long_seq else 1

    def colblk(off, width):
        return pl.BlockSpec((tm, width), lambda m: (m, off // width))

    const = lambda shape: pl.BlockSpec(shape, lambda m: tuple(0 for _ in shape))
    if long_seq:
        halo_rows = lambda m: jnp.maximum(m * (tm // HALO) - 1, 0)
        prev_specs = [pl.BlockSpec((HALO, CONV_CH), lambda m: (halo_rows(m), OFF_CC // CONV_CH)),
                      pl.BlockSpec((HALO, CONV_CH), lambda m: (halo_rows(m), OFF_CH // CONV_CH)),
                      pl.BlockSpec((None, CONV_K - 1, CONV_CH), lambda m: (m // tiles_per_seq, 0, 0))]
        prev_args = [proj, proj, prev]
        u_spec = pl.BlockSpec((None, U_TAIL, CONV_CH), lambda m: (m, 0, 0))
        u_shape = jax.ShapeDtypeStruct((n_tiles, U_TAIL, CONV_CH), f32)
    else:
        prev_specs = [pl.BlockSpec((tm, CONV_CH), lambda m: (m, 0))] * 2
        prev_args = list(prev)
        u_spec = pl.BlockSpec((tm, CONV_CH), lambda m: (m, 0))
        u_shape = jax.ShapeDtypeStruct((t, CONV_CH), f32)
    n_in = 7 + len(prev_specs) + 10
    return pl.pallas_call(
        functools.partial(_post_kernel, long_seq, tiles_per_seq),
        grid=(n_tiles,),
        in_specs=[pl.BlockSpec((tm, D_MODEL), lambda m: (m, 0)),
                  pl.BlockSpec((tm, GLA_DV), lambda m: (m, 0)),
                  colblk(OFF_CB, CONV_CH), colblk(OFF_CC, CONV_CH), colblk(OFF_CH, CONV_CH),
                  colblk(OFF_GA, D_MODEL), colblk(OFF_GB, D_MODEL)]
                 + prev_specs
                 + [const((CONV_K, CONV_CH)), _mod_spec(mod, MOD_G1, tm, seq_len),
                    _mod_spec(mod, MOD_SC2, tm, seq_len), _mod_spec(mod, MOD_SH2, tm, seq_len), const((1, D_MODEL)),
                    const((GLA_DV + CONV_CH, D_MODEL)), const((D_MODEL, LANES)), const((D_MODEL, LANES)),
                    const((1, LANES)), pl.BlockSpec(memory_space=pl.ANY)],
        out_specs=[pl.BlockSpec((tm, D_MODEL), lambda m: (m, 0)),
                   pl.BlockSpec((tm, D_MODEL), lambda m: (blk0 + m, 0)),
                   pl.BlockSpec((tm, LANES), lambda m: (m, 0)),
                   pl.BlockSpec((tm, LANES), lambda m: (m, 0)),
                   u_spec],
        out_shape=[jax.ShapeDtypeStruct((t, D_MODEL), f32),
                   jax.ShapeDtypeStruct(hn_buf.shape, bf16),
                   jax.ShapeDtypeStruct((t, LANES), f32),
                   jax.ShapeDtypeStruct((t, LANES), jnp.int32),
                   u_shape],
        input_output_aliases={n_in - 1: 1},
        compiler_params=_cparams(("arbitrary",)),
        name="post_mix",
    )(x2, oa, proj, proj, proj, proj, proj, *prev_args, wc, mod, mod, mod, nw, wo, wrh, wrl, br, hn_buf)


MOE_RB = 128
MOE_TM = 1536
MOE_TN = 512
MOE_KB = MOE_TM // MOE_RB
MOE_NT = D_FF // MOE_TN


def _moe_kernel(ie_ref, ib_ref, in_ref, iv_ref, iz_ref,
                x_hbm, wg_ref, wl_ref, wd_ref, bg_ref, bl_ref, bd_ref, out_hbm,
                xbuf, hbuf, obuf, wcast, sem_in, sem_out):
    i = pl.program_id(0)
    s = pl.program_id(1)
    n_items = pl.num_programs(0)
    nb = in_ref[i]

    def x_copy(item, jb, k):
        return pltpu.make_async_copy(
            x_hbm.at[pl.ds((ib_ref[item] + jb) * MOE_RB, MOE_RB), pl.ds(k * MOE_TN, MOE_TN)],
            xbuf.at[k, pl.ds(jb * MOE_RB, MOE_RB)], sem_in)

    def out_copy(item, jb, k):
        return pltpu.make_async_copy(
            obuf.at[k, pl.ds(jb * MOE_RB, MOE_RB)],
            out_hbm.at[pl.ds((ib_ref[item] + jb) * MOE_RB, MOE_RB), pl.ds(k * MOE_TN, MOE_TN)], sem_out)

    def for_blocks(n, fn):
        def body(jb, carry):
            for k in range(MOE_NT):
                fn(jb, k)
            return carry

        lax.fori_loop(0, n, body, 0)

    def for_rows(fn):
        def body(j, carry):
            fn(pl.multiple_of(j * (4 * MOE_RB), 4 * MOE_RB), 4 * MOE_RB)
            return carry

        lax.fori_loop(0, nb // 4, body, 0)
        base = (nb // 4) * (4 * MOE_RB)
        rem = nb % 4

        @pl.when(rem >= 2)
        def _():
            fn(pl.multiple_of(base, MOE_RB), 2 * MOE_RB)

        @pl.when(rem % 2 == 1)
        def _():
            fn(pl.multiple_of(base + (rem // 2) * (2 * MOE_RB), MOE_RB), MOE_RB)

    @pl.when(s == 0)
    def _():
        @pl.when(i == 0)
        def _():
            for_blocks(nb, lambda jb, k: x_copy(i, jb, k).start())

        for_blocks(nb, lambda jb, k: x_copy(i, jb, k).wait())

    @pl.when(s < MOE_NT)
    def _():
        wcast[:, 0:MOE_TN] = wg_ref[...].astype(bf16)
        wcast[:, MOE_TN:2 * MOE_TN] = wl_ref[...].astype(bf16)
        bg = bg_ref[...]
        bl = bl_ref[...]

        def up(r0, size):
            rows = pl.ds(r0, size)
            xb = jnp.concatenate([xbuf[k, rows, :] for k in range(MOE_NT)], axis=1)
            h = _dot(xb, wcast[...])
            hg = jnp.minimum(h[:, 0:MOE_TN] + bg, SWIGLU_LIMIT)
            hl = jnp.clip(h[:, MOE_TN:2 * MOE_TN] + bl, -SWIGLU_LIMIT, SWIGLU_LIMIT)
            hbuf[s, rows, :] = ((hl + 1.0) * hg * jax.nn.sigmoid(SWIGLU_ALPHA * hg)).astype(bf16)

        for_rows(up)

    @pl.when(s == MOE_NT)
    def _():
        @pl.when(i > 0)
        def _():
            prev = jnp.maximum(i - 1, 0)
            for_blocks(in_ref[prev], lambda jb, k: out_copy(prev, jb, k).wait())

        def fill(jb, k):
            obuf[k, pl.ds(pl.multiple_of(jb * MOE_RB, MOE_RB), MOE_RB), :] = jnp.zeros((MOE_RB, MOE_TN), bf16)
            out_copy(i, jb, k).start()

        nz = iz_ref[i]
        for_blocks(nz, fill)
        for_blocks(nz, lambda jb, k: out_copy(i, jb, k).wait())

        @pl.when(i + 1 < n_items)
        def _():
            nxt = jnp.minimum(i + 1, n_items - 1)
            for_blocks(in_ref[nxt], lambda jb, k: x_copy(nxt, jb, k).start())

    @pl.when(s >= MOE_NT)
    def _():
        wcast[:, 0:MOE_TN] = wd_ref[...].astype(bf16)
        bd = bd_ref[...]

        def down(r0, size):
            rows = pl.ds(r0, size)
            hb = jnp.concatenate([hbuf[k, rows, :] for k in range(MOE_NT)], axis=1)
            obuf[s - MOE_NT, rows, :] = (_dot(hb, wcast[:, 0:MOE_TN]) + bd).astype(bf16)

        for_rows(down)

    @pl.when(s == 2 * MOE_NT - 1)
    def _():
        for_blocks(nb, lambda jb, k: out_copy(i, jb, k).start())

        @pl.when(i == n_items - 1)
        def _():
            for_blocks(nb, lambda jb, k: out_copy(i, jb, k).wait())


def _moe(xs, items, w_up, b_up, w_down, b_down):
    rows = xs.shape[0]
    n_items = items[0].shape[0]

    last = MOE_NT - 1

    def up_tile(i, s, tabs):
        valid = tabs[3][i]
        return jnp.minimum(s, last) * valid + last * (1 - valid)

    def down_tile(i, s, tabs):
        valid = tabs[3][i]
        return jnp.maximum(s - MOE_NT, 0) * valid + last * (1 - valid)

    def expert(i, tabs):
        return tabs[0][i]

    grid_spec = pltpu.PrefetchScalarGridSpec(
        num_scalar_prefetch=len(items),
        grid=(n_items, 2 * MOE_NT),
        in_specs=[pl.BlockSpec(memory_space=pl.ANY),
                  pl.BlockSpec((None, D_MODEL, MOE_TN), lambda i, s, *t: (expert(i, t), 0, up_tile(i, s, t))),
                  pl.BlockSpec((None, D_MODEL, MOE_TN), lambda i, s, *t: (expert(i, t), 0, MOE_NT + up_tile(i, s, t))),
                  pl.BlockSpec((None, D_FF, MOE_TN), lambda i, s, *t: (expert(i, t), 0, down_tile(i, s, t))),
                  pl.BlockSpec((None, 1, MOE_TN), lambda i, s, *t: (expert(i, t), 0, up_tile(i, s, t))),
                  pl.BlockSpec((None, 1, MOE_TN), lambda i, s, *t: (expert(i, t), 0, MOE_NT + up_tile(i, s, t))),
                  pl.BlockSpec((None, 1, MOE_TN), lambda i, s, *t: (expert(i, t), 0, down_tile(i, s, t)))],
        out_specs=pl.BlockSpec(memory_space=pl.ANY),
        scratch_shapes=[pltpu.VMEM((MOE_NT, MOE_TM, MOE_TN), bf16),
                        pltpu.VMEM((MOE_NT, MOE_TM, MOE_TN), bf16),
                        pltpu.VMEM((MOE_NT, MOE_TM, MOE_TN), bf16),
                        pltpu.VMEM((D_MODEL, 2 * MOE_TN), bf16),
                        pltpu.SemaphoreType.DMA(()),
                        pltpu.SemaphoreType.DMA(())],
    )
    return pl.pallas_call(
        _moe_kernel,
        grid_spec=grid_spec,
        out_shape=jax.ShapeDtypeStruct((rows, D_MODEL), bf16),
        compiler_params=_cparams(("arbitrary", "arbitrary")),
        name="moe_experts",
    )(*items,
      xs, w_up, w_up, w_down,
      b_up.reshape(N_EXPERTS, 1, 2 * D_FF), b_up.reshape(N_EXPERTS, 1, 2 * D_FF),
      b_down.reshape(N_EXPERTS, 1, D_MODEL))


FIN_TM = 512


def _final_kernel(h_ref, og_ref, gate_ref, g2_ref, fw_ref, y_ref):
    gate = gate_ref[...]
    ff = og_ref[0].astype(f32) * gate[:, 0:1]
    for j in range(1, TOP_K):
        ff = ff + og_ref[j].astype(f32) * gate[:, j:j + 1]
    h = h_ref[...] + g2_ref[...] * ff
    y_ref[...] = _rms(h) * fw_ref[...]


def _final(h, og, gate, mod, fw, row0, seq_len):
    t = h.shape[0]
    tm = FIN_TM
    blk0 = row0 // tm
    return pl.pallas_call(
        _final_kernel,
        grid=(t // tm,),
        in_specs=[pl.BlockSpec((tm, D_MODEL), lambda m: (m, 0)),
                  pl.BlockSpec((TOP_K, tm, D_MODEL), lambda m: (0, blk0 + m, 0)),
                  pl.BlockSpec((tm, LANES), lambda m: (blk0 + m, 0)),
                  _mod_spec(mod, MOD_G2, tm, seq_len),
                  pl.BlockSpec((1, D_MODEL), lambda m: (0, 0))],
        out_specs=pl.BlockSpec((tm, D_MODEL), lambda m: (m, 0)),
        out_shape=jax.ShapeDtypeStruct((t, D_MODEL), f32),
        compiler_params=_cparams(("arbitrary",)),
        name="final_norm",
    )(h, og, gate, mod, fw)


def _count_le(bounds, x):
    return jnp.sum((bounds[None, :] <= x[:, None]).astype(jnp.int32), axis=1)


def _routing(top_e):
    t = top_e.shape[0]
    n_pairs = t * TOP_K
    e_flat = top_e.reshape(-1)
    onehot = (e_flat[:, None] == jnp.arange(N_EXPERTS, dtype=e_flat.dtype)[None, :]).astype(jnp.int32)
    csum = jnp.cumsum(onehot, axis=0)
    counts = csum[-1]
    nblk = (counts + MOE_RB - 1) // MOE_RB
    blk_end = jnp.cumsum(nblk)
    blk_start = blk_end - nblk
    row_start = blk_start * MOE_RB
    dest = jnp.sum(onehot * (csum + row_start[None, :]), axis=1) - 1
    n_rows = (n_pairs // MOE_RB + N_EXPERTS) * MOE_RB
    pair_bits = (n_pairs - 1).bit_length()
    pair_ids = jnp.arange(n_pairs, dtype=jnp.int32)
    order = jnp.sort(e_flat * (1 << pair_bits) + pair_ids) & ((1 << pair_bits) - 1)
    raw_start = jnp.cumsum(counts) - counts
    r = jnp.arange(n_rows, dtype=jnp.int32)
    begun = row_start[None, :] <= r[:, None]

    def at_row(table):
        steps = jnp.diff(table, prepend=0)
        return jnp.sum(jnp.where(begun, steps[None, :], 0), axis=1)

    src_pos = r - at_row(row_start - raw_start)
    is_real = r < at_row(row_start + counts)
    src_pair = order[jnp.clip(src_pos, 0, n_pairs - 1)]
    src_tok = jnp.where(is_real, src_pair // TOP_K, r % t).astype(jnp.int32)

    n_items = N_EXPERTS + n_rows // MOE_TM
    nit = (nblk + MOE_KB - 1) // MOE_KB
    it_end = jnp.cumsum(nit)
    it_start = it_end - nit
    idx = jnp.arange(n_items, dtype=jnp.int32)
    valid = idx < it_end[-1]
    last_valid = jnp.maximum(it_end[-1] - 1, 0)
    idx_c = jnp.minimum(idx, last_valid)
    e_of = jnp.minimum(_count_le(it_end, idx_c), N_EXPERTS - 1).astype(jnp.int32)
    k = idx_c - it_start[e_of]
    item_nb = jnp.where(valid, jnp.clip(nblk[e_of] - k * MOE_KB, 0, MOE_KB), 0).astype(jnp.int32)
    spare = idx - it_end[-1]
    tail_blk = blk_end[-1] + spare * MOE_KB
    item_nz = jnp.where(valid, 0, jnp.clip(n_rows // MOE_RB - tail_blk, 0, MOE_KB)).astype(jnp.int32)
    item_blk = jnp.where(valid, blk_start[e_of] + k * MOE_KB, tail_blk).astype(jnp.int32)
    items = (e_of, item_blk, item_nb, valid.astype(jnp.int32), item_nz)
    return dest.reshape(t, TOP_K), src_tok, items


def kernel(x_prompt, x_sample, state_gla, state_conv, c_prompt, c_sample, w_ada, b_ada, norm1_w,
           w_in, w_gk_up, b_gk, gla_norm_w, w_conv, w_out, norm2_w, w_router, b_router, w_up, b_up,
           w_down, b_down, final_norm_w):
    n_p, seq_p, _ = x_prompt.shape
    n_s, seq_s, _ = x_sample.shape
    t_p, t_s = n_p * seq_p, n_s * seq_s
    assert seq_s == GS_SEQ and w_ada.shape[0] == 1

    c_all = jnp.concatenate([jnp.repeat(c_sample, seq_s, axis=0), c_prompt], axis=0)
    pad = (-c_all.shape[0]) % BF16_ROWS
    c_all = jnp.pad(c_all, ((0, pad), (0, 0)))
    mod = _ada(c_all, w_ada[0], b_ada[0])
    mod_s = mod
    mod_p = mod[t_s:t_s + n_p].reshape(n_p, 1, N_MOD * D_MODEL)

    w_main, w_alow = _prep_w_in(w_in[0].T)
    w_gk = jnp.pad(w_gk_up[0], ((0, LANES - GLA_LOW_RANK), (0, 0))).astype(bf16)
    bgk = b_gk[0].reshape(1, GLA_DK)
    wo = w_out[0].astype(bf16)
    wr = jnp.pad(w_router[0], ((0, 0), (0, LANES - N_EXPERTS)))
    wr_hi = wr.astype(bf16)
    wr_lo = (wr - wr_hi.astype(f32)).astype(bf16)
    br = jnp.pad(b_router[0], (0, LANES - N_EXPERTS), constant_values=ROUTER_PAD).reshape(1, LANES)
    n1w = norm1_w[0].reshape(1, D_MODEL)
    n2w = norm2_w[0].reshape(1, D_MODEL)
    gnw = gla_norm_w[0].reshape(1, GLA_HV)
    fw = final_norm_w.reshape(1, D_MODEL)
    wc = w_conv[0]

    xp = x_prompt.reshape(t_p, D_MODEL)
    xs_ = x_sample.reshape(t_s, D_MODEL)

    proj_p, la_p = _in_proj(xp, mod_p, n1w, w_main, w_alow, w_gk, bgk, 1024, seq_p)
    proj_s, la_s = _in_proj(xs_, mod_s, n1w, w_main, w_alow, w_gk, bgk, t_s, seq_s)

    gla0 = jnp.zeros((n_p, GLA_HEADS, GLA_HK, GLA_HV), f32)
    conv0 = jnp.zeros((n_p, CONV_K - 1, CONV_CH), f32)
    oa_p, gla_p = _gla_scan(proj_p, la_p, gnw, gla0, n_p, seq_p)
    oa_s, gla_s = _gla_step(proj_s, la_s, gnw, state_gla[0])

    hn = jnp.zeros((HN_BUF_FACTOR * (t_p + t_s), D_MODEL), bf16)
    h_p, hn, gate_p, eid_p, ut_p = _post(xp, oa_p, proj_p, conv0, wc, mod_p, n2w, wo,
                                         wr_hi, wr_lo, br, hn, 0, True, seq_p)
    prev_s = (jnp.repeat(state_conv[0][:, 0], seq_s, axis=0), jnp.repeat(state_conv[0][:, 1], seq_s, axis=0))
    h_s, hn, gate_s, eid_s, u_s = _post(xs_, oa_s, proj_s, prev_s, wc, mod_s, n2w, wo,
                                        wr_hi, wr_lo, br, hn, t_p, False, seq_s)
    conv_p = ut_p.reshape(n_p, seq_p // POST_TM, U_TAIL, CONV_CH)[:, -1, U_TAIL - (CONV_K - 1):]
    conv_s = u_s.reshape(n_s, seq_s, CONV_CH)[:, seq_s - (CONV_K - 1):]

    gate = jnp.concatenate([gate_p, gate_s], axis=0)
    top_e = jnp.concatenate([eid_p, eid_s], axis=0)[:, :TOP_K]
    dest, src_tok, items = _routing(top_e)
    x_rows = hn[src_tok]
    out_rows = _moe(x_rows, items, w_up[0], b_up[0], w_down[0], b_down[0])
    og = out_rows[dest.T.reshape(-1)].reshape(TOP_K, t_p + t_s, D_MODEL)

    y_p = _final(h_p, og, gate, mod_p, fw, 0, seq_p)
    y_s = _final(h_s, og, gate, mod_s, fw, t_p, seq_s)

    return (y_p.reshape(n_p, seq_p, D_MODEL), y_s.reshape(n_s, seq_s, D_MODEL),
            gla_p[None], conv_p[None], gla_s[None], conv_s[None])
```

```python
import functools

import jax
import jax.numpy as jnp
from jax import lax
from jax.experimental import pallas as pl
from jax.experimental.pallas import tpu as pltpu

f32 = jnp.float32
bf16 = jnp.bfloat16

D_MODEL = 2048
N_MOD = 6
GLA_HEADS = 4
GLA_DK = 512
GLA_DV = 1024
GLA_HK = 128
GLA_HV = 256
GLA_LOW_RANK = 16
GLA_TAU = 16.0
GLA_CHUNK = 64
CONV_CH = 1024
CONV_K = 3
N_EXPERTS = 32
TOP_K = 4
D_FF = 2048
SWIGLU_LIMIT = 7.0
SWIGLU_ALPHA = 1.702
EPS = 1e-6

LANES = 128
SUBLANES = 8
BF16_ROWS = 2 * SUBLANES
VMEM_LIMIT = 56 * 1024 * 1024

OFF_Q = 0
OFF_K = OFF_Q + GLA_DK
OFF_V = OFF_K + GLA_DK
OFF_G = OFF_V + GLA_DV
OFF_CB = OFF_G + GLA_DV
OFF_CC = OFF_CB + CONV_CH
OFF_CH = OFF_CC + CONV_CH
OFF_GA = OFF_CH + CONV_CH
OFF_GB = OFF_GA + D_MODEL
D_MAIN = OFF_GB + D_MODEL

ROUTER_PAD = -1e30
HN_BUF_FACTOR = 2

HIGHEST = lax.Precision.HIGHEST


def _cparams(sem):
    return pltpu.CompilerParams(dimension_semantics=sem, vmem_limit_bytes=VMEM_LIMIT)


def _dot(a, b):
    return jnp.dot(a, b, preferred_element_type=f32)


def _dot_nt(a, b):
    return lax.dot_general(a, b, (((1,), (1,)), ((), ())), preferred_element_type=f32)


def _dot_exact(a, b):
    return jnp.dot(a, b, precision=HIGHEST, preferred_element_type=f32)


def _sigmoid_tanh(x):
    return 0.5 * jnp.tanh(0.5 * x) + 0.5


def _rms(x):
    return x * lax.rsqrt(jnp.mean(x * x, axis=-1, keepdims=True) + EPS)


ADA_TN = 2048


def _ada_kernel(c_ref, w_ref, b_ref, o_ref):
    c = c_ref[...]
    s = (c * jax.nn.sigmoid(c)).astype(bf16)
    o_ref[...] = _dot(s, w_ref[...].astype(bf16)) + b_ref[...]


def _ada(c, w, b):
    rows = c.shape[0]
    n = w.shape[1]
    return pl.pallas_call(
        _ada_kernel,
        grid=(n // ADA_TN,),
        in_specs=[pl.BlockSpec((rows, D_MODEL), lambda j: (0, 0)),
                  pl.BlockSpec((D_MODEL, ADA_TN), lambda j: (0, j)),
                  pl.BlockSpec((1, ADA_TN), lambda j: (0, j))],
        out_specs=pl.BlockSpec((rows, ADA_TN), lambda j: (0, j)),
        out_shape=jax.ShapeDtypeStruct((rows, n), f32),
        compiler_params=_cparams(("arbitrary",)),
        name="adaln",
    )(c, w, b.reshape(1, n))


PREP_TN = 1024


def _prep_kernel(a_ref, b_ref, low_ref, main_ref, alow_ref):
    j = pl.program_id(0)
    first_shifted = OFF_G // PREP_TN

    @pl.when(j < first_shifted)
    def _():
        main_ref[...] = a_ref[...].T.astype(bf16)

    @pl.when(j >= first_shifted)
    def _():
        wide = jnp.concatenate([a_ref[...], b_ref[...]], axis=0)
        main_ref[...] = wide[GLA_LOW_RANK:GLA_LOW_RANK + PREP_TN].T.astype(bf16)

    @pl.when(j == 0)
    def _():
        low = low_ref[...].T
        lane = lax.broadcasted_iota(jnp.int32, low.shape, 1)
        alow_ref[...] = jnp.where(lane < GLA_LOW_RANK, low, 0.0).astype(bf16)


def _prep_w_in(wt):
    d = wt.shape[1]
    return pl.pallas_call(
        _prep_kernel,
        grid=(D_MAIN // PREP_TN,),
        in_specs=[pl.BlockSpec((PREP_TN, d), lambda j: (j, 0)),
                  pl.BlockSpec((GLA_LOW_RANK, d), lambda j: ((j + 1) * (PREP_TN // GLA_LOW_RANK), 0)),
                  pl.BlockSpec((LANES, d), lambda j: (OFF_G // LANES, 0))],
        out_specs=[pl.BlockSpec((d, PREP_TN), lambda j: (0, j)),
                   pl.BlockSpec((d, LANES), lambda j: (0, 0))],
        out_shape=[jax.ShapeDtypeStruct((d, D_MAIN), bf16),
                   jax.ShapeDtypeStruct((d, LANES), bf16)],
        compiler_params=_cparams(("arbitrary",)),
        name="prep_w_in",
    )(wt, wt, wt)


IN_TN = 1024


def _in_kernel(x_ref, sc_ref, sh_ref, nw_ref, w_ref, wa_ref, wgk_ref, bgk_ref,
               proj_ref, la_ref, xn_ref):
    @pl.when(pl.program_id(1) == 0)
    def _():
        xn = _rms(x_ref[...]) * nw_ref[...] * (1.0 + sc_ref[...]) + sh_ref[...]
        xnb = xn.astype(bf16)
        xn_ref[...] = xnb
        a_low = _dot(xnb, wa_ref[...])
        z = _dot(a_low.astype(bf16), wgk_ref[...]) + bgk_ref[...]
        la_ref[...] = (jnp.minimum(z, 0.0) - jnp.log(1.0 + jnp.exp(-jnp.abs(z)))) * (1.0 / GLA_TAU)

    proj_ref[...] = _dot(xn_ref[...], w_ref[...]).astype(bf16)


MOD_SH1, MOD_SC1, MOD_G1, MOD_SH2, MOD_SC2, MOD_G2 = range(N_MOD)


def _mod_spec(mod, which, tm, seq_len):
    if mod.ndim == 3:
        tiles_per_seq = seq_len // tm
        return pl.BlockSpec((None, 1, D_MODEL), lambda *g: (g[0] // tiles_per_seq, 0, which))
    return pl.BlockSpec((tm, D_MODEL), lambda *g: (g[0], which))


def _in_proj(x2, mod, nw, w_main, w_alow, w_gk, b_gk, tm, seq_len):
    t = x2.shape[0]
    return pl.pallas_call(
        _in_kernel,
        grid=(t // tm, D_MAIN // IN_TN),
        in_specs=[pl.BlockSpec((tm, D_MODEL), lambda m, n: (m, 0)),
                  _mod_spec(mod, MOD_SC1, tm, seq_len), _mod_spec(mod, MOD_SH1, tm, seq_len),
                  pl.BlockSpec((1, D_MODEL), lambda m, n: (0, 0)),
                  pl.BlockSpec((D_MODEL, IN_TN), lambda m, n: (0, n)),
                  pl.BlockSpec((D_MODEL, LANES), lambda m, n: (0, 0)),
                  pl.BlockSpec((LANES, GLA_DK), lambda m, n: (0, 0)),
                  pl.BlockSpec((1, GLA_DK), lambda m, n: (0, 0))],
        out_specs=[pl.BlockSpec((tm, IN_TN), lambda m, n: (m, n)),
                   pl.BlockSpec((tm, GLA_DK), lambda m, n: (m, 0))],
        out_shape=[jax.ShapeDtypeStruct((t, D_MAIN), bf16),
                   jax.ShapeDtypeStruct((t, GLA_DK), f32)],
        scratch_shapes=[pltpu.VMEM((tm, D_MODEL), bf16)],
        compiler_params=_cparams(("arbitrary", "arbitrary")),
        name="in_proj",
    )(x2, mod, mod, nw, w_main, w_alow, w_gk, b_gk)


GLA_R = 256


def _gla_norm_gate(o, nw, g):
    gf = g.astype(f32)
    return _rms(o) * nw * (gf * jax.nn.sigmoid(gf))


def _gla_scan_kernel(q_ref, k_ref, v_ref, g_ref, la_ref, nw_ref, s0_ref, o_ref, sout_ref, s_ref):
    step = pl.program_id(1)

    @pl.when(step == 0)
    def _():
        s_ref[...] = s0_ref[...]

    r, c = GLA_R, GLA_CHUNK
    n_chunks = r // c
    row = lax.broadcasted_iota(jnp.int32, (r, r), 0)
    col = lax.broadcasted_iota(jnp.int32, (r, r), 1)
    same_chunk = (row // c) == (col // c)
    causal = same_chunk & (col <= row)

    la = la_ref[...]
    b = _dot_exact(causal.astype(f32), la)
    b_mid = jnp.concatenate(
        [jnp.broadcast_to(b[i * c + c // 2:i * c + c // 2 + 1], (c, GLA_DK)) for i in range(n_chunks)], axis=0)
    b_last = jnp.concatenate(
        [jnp.broadcast_to(b[i * c + c - 1:i * c + c], (c, GLA_DK)) for i in range(n_chunks)], axis=0)
    q = q_ref[...].astype(f32) * (GLA_HK ** -0.5)
    k = k_ref[...].astype(f32)
    qs = (q * jnp.exp(b - b_mid)).astype(bf16)
    ks = (k * jnp.exp(b_mid - b)).astype(bf16)
    qd = (q * jnp.exp(b)).astype(bf16)
    kd_t = (k * jnp.exp(b_last - b)).T.astype(bf16)
    la_t = la.T
    lane = lax.broadcasted_iota(jnp.int32, (GLA_HK, r), 1)
    nw = nw_ref[...]

    for h in range(GLA_HEADS):
        hk = slice(h * GLA_HK, (h + 1) * GLA_HK)
        hv = slice(h * GLA_HV, (h + 1) * GLA_HV)
        v_h = v_ref[:, hv]
        scores = _dot_nt(qs[:, hk], ks[:, hk])
        o_intra = _dot(jnp.where(causal, scores, 0.0).astype(bf16), v_h)
        s = s_ref[h]
        for i in range(n_chunks):
            rows = slice(i * c, (i + 1) * c)
            in_chunk = (lane // c) == i
            o = o_intra[rows] + _dot(qd[rows, hk], s.astype(bf16))
            o_ref[rows, hv] = _gla_norm_gate(o, nw, g_ref[rows, hv]).astype(bf16)
            decay = jnp.exp(jnp.sum(jnp.where(in_chunk, la_t[hk], 0.0), axis=1, keepdims=True))
            s = decay * s + _dot(jnp.where(in_chunk, kd_t[hk], jnp.zeros_like(kd_t[hk])), v_h)
        s_ref[h] = s

    @pl.when(step == pl.num_programs(1) - 1)
    def _():
        sout_ref[...] = s_ref[...]


def _gla_scan(proj, la, nw, s0, n_seq, seq_len):
    steps = seq_len // GLA_R
    t = n_seq * seq_len

    def rows(b, s):
        return b * steps + s

    return pl.pallas_call(
        _gla_scan_kernel,
        grid=(n_seq, steps),
        in_specs=[pl.BlockSpec((GLA_R, GLA_DK), lambda b, s: (rows(b, s), OFF_Q // GLA_DK)),
                  pl.BlockSpec((GLA_R, GLA_DK), lambda b, s: (rows(b, s), OFF_K // GLA_DK)),
                  pl.BlockSpec((GLA_R, GLA_DV), lambda b, s: (rows(b, s), OFF_V // GLA_DV)),
                  pl.BlockSpec((GLA_R, GLA_DV), lambda b, s: (rows(b, s), OFF_G // GLA_DV)),
                  pl.BlockSpec((GLA_R, GLA_DK), lambda b, s: (rows(b, s), 0)),
                  pl.BlockSpec((1, GLA_HV), lambda b, s: (0, 0)),
                  pl.BlockSpec((None, GLA_HEADS, GLA_HK, GLA_HV), lambda b, s: (b, 0, 0, 0))],
        out_specs=[pl.BlockSpec((GLA_R, GLA_DV), lambda b, s: (rows(b, s), 0)),
                   pl.BlockSpec((None, GLA_HEADS, GLA_HK, GLA_HV), lambda b, s: (b, 0, 0, 0))],
        out_shape=[jax.ShapeDtypeStruct((t, GLA_DV), bf16),
                   jax.ShapeDtypeStruct((n_seq, GLA_HEADS, GLA_HK, GLA_HV), f32)],
        scratch_shapes=[pltpu.VMEM((GLA_HEADS, GLA_HK, GLA_HV), f32)],
        compiler_params=_cparams(("arbitrary", "arbitrary")),
        name="gla_scan",
    )(proj, proj, proj, proj, la, nw, s0)


GS_SEQ = 4
GS_ROWS = 128
GS_B = GS_ROWS // GS_SEQ
GS_STEP_B = 4


def _gla_step_kernel(q_ref, k_ref, v_ref, g_ref, la_ref, nw_ref, s0_ref, o_ref, sout_ref,
                     oacc_ref, qd_ref, kdt_ref, lat_ref):
    j = pl.program_id(1)
    r = GS_ROWS
    row = lax.broadcasted_iota(jnp.int32, (r, r), 0)
    col = lax.broadcasted_iota(jnp.int32, (r, r), 1)

    @pl.when(j == 0)
    def _():
        same_seq = (row // GS_SEQ) == (col // GS_SEQ)
        causal = same_seq & (col <= row)
        la = la_ref[...]
        b = _dot_exact(causal.astype(f32), la)
        b_mid = _dot_exact((same_seq & (col % GS_SEQ <= GS_SEQ // 2)).astype(f32), la)
        b_last = _dot_exact(same_seq.astype(f32), la)
        q = q_ref[...].astype(f32) * (GLA_HK ** -0.5)
        k = k_ref[...].astype(f32)
        qs = (q * jnp.exp(b - b_mid)).astype(bf16)
        ks = (k * jnp.exp(b_mid - b)).astype(bf16)
        qd_ref[...] = (q * jnp.exp(b)).astype(bf16)
        kdt_ref[...] = (k * jnp.exp(b_last - b)).T.astype(bf16)
        lat_ref[...] = la.T
        for h in range(GLA_HEADS):
            hk = slice(h * GLA_HK, (h + 1) * GLA_HK)
            hv = slice(h * GLA_HV, (h + 1) * GLA_HV)
            scores = _dot_nt(qs[:, hk], ks[:, hk])
            oacc_ref[:, hv] = _dot(jnp.where(causal, scores, 0.0).astype(bf16), v_ref[:, hv])

    for bb in range(GS_STEP_B):
        seq = j * GS_STEP_B + bb
        in_seq_lane = (col // GS_SEQ) == seq
        in_seq_row = (row[:, :GLA_HK] // GS_SEQ) == seq
        for h in range(GLA_HEADS):
            hk = slice(h * GLA_HK, (h + 1) * GLA_HK)
            hv = slice(h * GLA_HV, (h + 1) * GLA_HV)
            s0 = s0_ref[bb, h]
            kd_t = kdt_ref[hk, :]
            ds = _dot(jnp.where(in_seq_lane, kd_t, jnp.zeros_like(kd_t)), v_ref[:, hv])
            decay = jnp.exp(jnp.sum(jnp.where(in_seq_lane, lat_ref[hk, :], 0.0), axis=1, keepdims=True))
            sout_ref[bb, h] = decay * s0 + ds
            qd = qd_ref[:, hk]
            oacc_ref[:, hv] += _dot(jnp.where(in_seq_row, qd, jnp.zeros_like(qd)), s0.astype(bf16))

    @pl.when(j == pl.num_programs(1) - 1)
    def _():
        nw = nw_ref[...]
        for h in range(GLA_HEADS):
            hv = slice(h * GLA_HV, (h + 1) * GLA_HV)
            o_ref[:, hv] = _gla_norm_gate(oacc_ref[:, hv], nw, g_ref[:, hv]).astype(bf16)


def _gla_step(proj, la, nw, s0):
    n_seq = s0.shape[0]
    t = n_seq * GS_SEQ
    groups = t // GS_ROWS
    steps = GS_B // GS_STEP_B
    return pl.pallas_call(
        _gla_step_kernel,
        grid=(groups, steps),
        in_specs=[pl.BlockSpec((GS_ROWS, GLA_DK), lambda g, j: (g, OFF_Q // GLA_DK)),
                  pl.BlockSpec((GS_ROWS, GLA_DK), lambda g, j: (g, OFF_K // GLA_DK)),
                  pl.BlockSpec((GS_ROWS, GLA_DV), lambda g, j: (g, OFF_V // GLA_DV)),
                  pl.BlockSpec((GS_ROWS, GLA_DV), lambda g, j: (g, OFF_G // GLA_DV)),
                  pl.BlockSpec((GS_ROWS, GLA_DK), lambda g, j: (g, 0)),
                  pl.BlockSpec((1, GLA_HV), lambda g, j: (0, 0)),
                  pl.BlockSpec((GS_STEP_B, GLA_HEADS, GLA_HK, GLA_HV), lambda g, j: (g * steps + j, 0, 0, 0))],
        out_specs=[pl.BlockSpec((GS_ROWS, GLA_DV), lambda g, j: (g, 0)),
                   pl.BlockSpec((GS_STEP_B, GLA_HEADS, GLA_HK, GLA_HV), lambda g, j: (g * steps + j, 0, 0, 0))],
        out_shape=[jax.ShapeDtypeStruct((t, GLA_DV), bf16),
                   jax.ShapeDtypeStruct((n_seq, GLA_HEADS, GLA_HK, GLA_HV), f32)],
        scratch_shapes=[pltpu.VMEM((GS_ROWS, GLA_DV), f32),
                        pltpu.VMEM((GS_ROWS, GLA_DK), bf16),
                        pltpu.VMEM((GLA_DK, GS_ROWS), bf16),
                        pltpu.VMEM((GLA_DK, GS_ROWS), f32)],
        compiler_params=_cparams(("arbitrary", "arbitrary")),
        name="gla_step",
    )(proj, proj, proj, proj, la, nw, s0)


POST_TM = 256
HALO = BF16_ROWS
U_TAIL = SUBLANES


def _post_kernel(long_seq, tiles_per_seq, *refs):
    if long_seq:
        (x_ref, oa_ref, cb_ref, cc_ref, ch_ref, ga_ref, gb_ref, hcc_ref, hch_ref, cbuf_ref,
         wc_ref, g1_ref, sc_ref, sh_ref, nw_ref, wo_ref, wrh_ref, wrl_ref, br_ref, _hn_alias,
         h_ref, hn_ref, gate_ref, eid_ref, u_ref) = refs
    else:
        (x_ref, oa_ref, cb_ref, cc_ref, ch_ref, ga_ref, gb_ref, p0_ref, p1_ref,
         wc_ref, g1_ref, sc_ref, sh_ref, nw_ref, wo_ref, wrh_ref, wrl_ref, br_ref, _hn_alias,
         h_ref, hn_ref, gate_ref, eid_ref, u_ref) = refs
    tm = x_ref.shape[0]
    u = cc_ref[...].astype(f32) * ch_ref[...].astype(f32)
    row = lax.broadcasted_iota(jnp.int32, (tm, CONV_CH), 0)
    if long_seq:
        pos = row
        first = (pl.program_id(0) % tiles_per_seq) == 0
        halo = hcc_ref[HALO - 2:HALO, :].astype(f32) * hch_ref[HALO - 2:HALO, :].astype(f32)
        cbuf = cbuf_ref[...]
        p0 = jnp.where(first, cbuf[0:1], halo[0:1])
        p1 = jnp.where(first, cbuf[1:2], halo[1:2])
        u_ref[...] = u[tm - U_TAIL:tm]
    else:
        pos = row % GS_SEQ
        p0 = p0_ref[...]
        p1 = p1_ref[...]
        u_ref[...] = u
    u1 = jnp.where(pos == 0, p1, pltpu.roll(u, 1, 0))
    u2 = jnp.where(pos == 0, p0, jnp.where(pos == 1, p1, pltpu.roll(u, 2, 0)))
    wc = wc_ref[...]
    conv = wc[0:1] * u2 + wc[1:2] * u1 + wc[2:3] * u
    ob = (cb_ref[...].astype(f32) * conv).astype(bf16)

    ya = _dot(oa_ref[...], wo_ref[0:GLA_DV, :])
    yb = _dot(ob, wo_ref[GLA_DV:GLA_DV + CONV_CH, :])
    y = _sigmoid_tanh(ga_ref[...].astype(f32)) * ya + _sigmoid_tanh(gb_ref[...].astype(f32)) * yb
    h = x_ref[...] + g1_ref[...] * y
    h_ref[...] = h
    hn = _rms(h) * nw_ref[...] * (1.0 + sc_ref[...]) + sh_ref[...]
    hn_hi = hn.astype(bf16)
    hn_ref[...] = hn_hi
    hn_lo = (hn - hn_hi.astype(f32)).astype(bf16)
    lg = (_dot(hn_hi, wrh_ref[...]) + _dot(hn_hi, wrl_ref[...]) + _dot(hn_lo, wrh_ref[...])
          + br_ref[...])

    lane = lax.broadcasted_iota(jnp.int32, (tm, LANES), 1)
    lane_f = lane.astype(f32)
    vals, idxs = [], []
    for _ in range(TOP_K):
        m = jnp.max(lg, axis=-1, keepdims=True)
        idx = jnp.min(jnp.where(lg == m, lane_f, float(LANES)), axis=-1, keepdims=True)
        vals.append(m)
        idxs.append(idx)
        lg = jnp.where(lane_f == idx, -jnp.inf, lg)
    exps = [jnp.exp(v - vals[0]) for v in vals]
    den = exps[0] + exps[1] + exps[2] + exps[3]
    gate = jnp.zeros((tm, LANES), f32)
    eid = jnp.zeros((tm, LANES), f32)
    for j in range(TOP_K):
        gate = jnp.where(lane == j, exps[j] / den, gate)
        eid = jnp.where(lane == j, idxs[j], eid)
    gate_ref[...] = gate
    eid_ref[...] = eid.astype(jnp.int32)


def _post(x2, oa, proj, prev, wc, mod, nw, wo, wrh, wrl, br, hn_buf, row0, long_seq, seq_len):
    t = x2.shape[0]
    tm = POST_TM
    n_tiles = t // tm
    blk0 = row0 // tm
    tiles_per_seq = seq_len // tm if long_seq else 1

    def colblk(off, width):
        return pl.BlockSpec((tm, width), lambda m: (m, off // width))

    const = lambda shape: pl.BlockSpec(shape, lambda m: tuple(0 for _ in shape))
    if long_seq:
        halo_rows = lambda m: jnp.maximum(m * (tm // HALO) - 1, 0)
        prev_specs = [pl.BlockSpec((HALO, CONV_CH), lambda m: (halo_rows(m), OFF_CC // CONV_CH)),
                      pl.BlockSpec((HALO, CONV_CH), lambda m: (halo_rows(m), OFF_CH // CONV_CH)),
                      pl.BlockSpec((None, CONV_K - 1, CONV_CH), lambda m: (m // tiles_per_seq, 0, 0))]
        prev_args = [proj, proj, prev]
        u_spec = pl.BlockSpec((None, U_TAIL, CONV_CH), lambda m: (m, 0, 0))
        u_shape = jax.ShapeDtypeStruct((n_tiles, U_TAIL, CONV_CH), f32)
    else:
        prev_specs = [pl.BlockSpec((tm, CONV_CH), lambda m: (m, 0))] * 2
        prev_args = list(prev)
        u_spec = pl.BlockSpec((tm, CONV_CH), lambda m: (m, 0))
        u_shape = jax.ShapeDtypeStruct((t, CONV_CH), f32)
    n_in = 7 + len(prev_specs) + 10
    return pl.pallas_call(
        functools.partial(_post_kernel, long_seq, tiles_per_seq),
        grid=(n_tiles,),
        in_specs=[pl.BlockSpec((tm, D_MODEL), lambda m: (m, 0)),
                  pl.BlockSpec((tm, GLA_DV), lambda m: (m, 0)),
                  colblk(OFF_CB, CONV_CH), colblk(OFF_CC, CONV_CH), colblk(OFF_CH, CONV_CH),
                  colblk(OFF_GA, D_MODEL), colblk(OFF_GB, D_MODEL)]
                 + prev_specs
                 + [const((CONV_K, CONV_CH)), _mod_spec(mod, MOD_G1, tm, seq_len),
                    _mod_spec(mod, MOD_SC2, tm, seq_len), _mod_spec(mod, MOD_SH2, tm, seq_len), const((1, D_MODEL)),
                    const((GLA_DV + CONV_CH, D_MODEL)), const((D_MODEL, LANES)), const((D_MODEL, LANES)),
                    const((1, LANES)), pl.BlockSpec(memory_space=pl.ANY)],
        out_specs=[pl.BlockSpec((tm, D_MODEL), lambda m: (m, 0)),
                   pl.BlockSpec((tm, D_MODEL), lambda m: (blk0 + m, 0)),
                   pl.BlockSpec((tm, LANES), lambda m: (m, 0)),
                   pl.BlockSpec((tm, LANES), lambda m: (m, 0)),
                   u_spec],
        out_shape=[jax.ShapeDtypeStruct((t, D_MODEL), f32),
                   jax.ShapeDtypeStruct(hn_buf.shape, bf16),
                   jax.ShapeDtypeStruct((t, LANES), f32),
                   jax.ShapeDtypeStruct((t, LANES), jnp.int32),
                   u_shape],
        input_output_aliases={n_in - 1: 1},
        compiler_params=_cparams(("arbitrary",)),
        name="post_mix",
    )(x2, oa, proj, proj, proj, proj, proj, *prev_args, wc, mod, mod, mod, nw, wo, wrh, wrl, br, hn_buf)


MOE_RB = 128
MOE_TM = 1536
MOE_TN = 512
MOE_KB = MOE_TM // MOE_RB
MOE_NT = D_FF // MOE_TN
MOE_BIG = 8


def _moe_kernel(ie_ref, ib_ref, in_ref, iv_ref, iz_ref,
                x_hbm, wg_ref, wl_ref, wd_ref, bg_ref, bl_ref, bd_ref, out_hbm,
                xbuf, hbuf, obuf, wcast, sem_in, sem_out):
    i = pl.program_id(0)
    s = pl.program_id(1)
    n_items = pl.num_programs(0)
    nb = in_ref[i]

    def x_copy(item, jb, k):
        return pltpu.make_async_copy(
            x_hbm.at[pl.ds((ib_ref[item] + jb) * MOE_RB, MOE_RB), pl.ds(k * MOE_TN, MOE_TN)],
            xbuf.at[k, pl.ds(jb * MOE_RB, MOE_RB)], sem_in)

    def out_copy(item, jb, k):
        return pltpu.make_async_copy(
            obuf.at[k, pl.ds(jb * MOE_RB, MOE_RB)],
            out_hbm.at[pl.ds((ib_ref[item] + jb) * MOE_RB, MOE_RB), pl.ds(k * MOE_TN, MOE_TN)], sem_out)

    def for_blocks(n, fn):
        def body(jb, carry):
            for k in range(MOE_NT):
                fn(jb, k)
            return carry

        lax.fori_loop(0, n, body, 0)

    def for_rows(fn):
        def body(j, carry):
            fn(pl.multiple_of(j * (MOE_BIG * MOE_RB), MOE_BIG * MOE_RB), MOE_BIG * MOE_RB)
            return carry

        lax.fori_loop(0, nb // MOE_BIG, body, 0)
        done = (nb // MOE_BIG) * MOE_BIG
        size = MOE_BIG // 2
        while size >= 1:
            @pl.when((nb & size) != 0)
            def _(done=done, size=size):
                fn(pl.multiple_of(done * MOE_RB, MOE_RB), size * MOE_RB)

            done = done + (nb & size)
            size //= 2

    @pl.when(s == 0)
    def _():
        @pl.when(i == 0)
        def _():
            for_blocks(nb, lambda jb, k: x_copy(i, jb, k).start())

        for_blocks(nb, lambda jb, k: x_copy(i, jb, k).wait())

    @pl.when(s < MOE_NT)
    def _():
        wcast[:, 0:MOE_TN] = wg_ref[...].astype(bf16)
        wcast[:, MOE_TN:2 * MOE_TN] = wl_ref[...].astype(bf16)
        bg = bg_ref[...]
        bl = bl_ref[...]

        def up(r0, size):
            rows = pl.ds(r0, size)
            xb = jnp.concatenate([xbuf[k, rows, :] for k in range(MOE_NT)], axis=1)
            h = _dot(xb, wcast[...])
            hg = jnp.minimum(h[:, 0:MOE_TN] + bg, SWIGLU_LIMIT)
            hl = jnp.clip(h[:, MOE_TN:2 * MOE_TN] + bl, -SWIGLU_LIMIT, SWIGLU_LIMIT)
            hbuf[s, rows, :] = ((hl + 1.0) * hg * jax.nn.sigmoid(SWIGLU_ALPHA * hg)).astype(bf16)

        for_rows(up)

    @pl.when(s == MOE_NT)
    def _():
        @pl.when(i > 0)
        def _():
            prev = jnp.maximum(i - 1, 0)
            for_blocks(in_ref[prev], lambda jb, k: out_copy(prev, jb, k).wait())

        def fill(jb, k):
            obuf[k, pl.ds(pl.multiple_of(jb * MOE_RB, MOE_RB), MOE_RB), :] = jnp.zeros((MOE_RB, MOE_TN), bf16)
            out_copy(i, jb, k).start()

        nz = iz_ref[i]
        for_blocks(nz, fill)
        for_blocks(nz, lambda jb, k: out_copy(i, jb, k).wait())

        @pl.when(i + 1 < n_items)
        def _():
            nxt = jnp.minimum(i + 1, n_items - 1)
            for_blocks(in_ref[nxt], lambda jb, k: x_copy(nxt, jb, k).start())

    @pl.when(s >= MOE_NT)
    def _():
        wcast[:, 0:MOE_TN] = wd_ref[...].astype(bf16)
        bd = bd_ref[...]

        def down(r0, size):
            rows = pl.ds(r0, size)
            hb = jnp.concatenate([hbuf[k, rows, :] for k in range(MOE_NT)], axis=1)
            obuf[s - MOE_NT, rows, :] = (_dot(hb, wcast[:, 0:MOE_TN]) + bd).astype(bf16)

        for_rows(down)

    @pl.when(s == 2 * MOE_NT - 1)
    def _():
        for_blocks(nb, lambda jb, k: out_copy(i, jb, k).start())

        @pl.when(i == n_items - 1)
        def _():
            for_blocks(nb, lambda jb, k: out_copy(i, jb, k).wait())


def _moe(xs, items, w_up, b_up, w_down, b_down):
    rows = xs.shape[0]
    n_items = items[0].shape[0]

    last = MOE_NT - 1

    def up_tile(i, s, tabs):
        valid = tabs[3][i]
        return jnp.minimum(s, last) * valid + last * (1 - valid)

    def down_tile(i, s, tabs):
        valid = tabs[3][i]
        return jnp.maximum(s - MOE_NT, 0) * valid + last * (1 - valid)

    def expert(i, tabs):
        return tabs[0][i]

    grid_spec = pltpu.PrefetchScalarGridSpec(
        num_scalar_prefetch=len(items),
        grid=(n_items, 2 * MOE_NT),
        in_specs=[pl.BlockSpec(memory_space=pl.ANY),
                  pl.BlockSpec((None, D_MODEL, MOE_TN), lambda i, s, *t: (expert(i, t), 0, up_tile(i, s, t))),
                  pl.BlockSpec((None, D_MODEL, MOE_TN), lambda i, s, *t: (expert(i, t), 0, MOE_NT + up_tile(i, s, t))),
                  pl.BlockSpec((None, D_FF, MOE_TN), lambda i, s, *t: (expert(i, t), 0, down_tile(i, s, t))),
                  pl.BlockSpec((None, 1, MOE_TN), lambda i, s, *t: (expert(i, t), 0, up_tile(i, s, t))),
                  pl.BlockSpec((None, 1, MOE_TN), lambda i, s, *t: (expert(i, t), 0, MOE_NT + up_tile(i, s, t))),
                  pl.BlockSpec((None, 1, MOE_TN), lambda i, s, *t: (expert(i, t), 0, down_tile(i, s, t)))],
        out_specs=pl.BlockSpec(memory_space=pl.ANY),
        scratch_shapes=[pltpu.VMEM((MOE_NT, MOE_TM, MOE_TN), bf16),
                        pltpu.VMEM((MOE_NT, MOE_TM, MOE_TN), bf16),
                        pltpu.VMEM((MOE_NT, MOE_TM, MOE_TN), bf16),
                        pltpu.VMEM((D_MODEL, 2 * MOE_TN), bf16),
                        pltpu.SemaphoreType.DMA(()),
                        pltpu.SemaphoreType.DMA(())],
    )
    return pl.pallas_call(
        _moe_kernel,
        grid_spec=grid_spec,
        out_shape=jax.ShapeDtypeStruct((rows, D_MODEL), bf16),
        compiler_params=_cparams(("arbitrary", "arbitrary")),
        name="moe_experts",
    )(*items,
      xs, w_up, w_up, w_down,
      b_up.reshape(N_EXPERTS, 1, 2 * D_FF), b_up.reshape(N_EXPERTS, 1, 2 * D_FF),
      b_down.reshape(N_EXPERTS, 1, D_MODEL))


FIN_TM = 512


def _final_kernel(h_ref, og_ref, gate_ref, g2_ref, fw_ref, y_ref):
    gate = gate_ref[...]
    ff = og_ref[0].astype(f32) * gate[:, 0:1]
    for j in range(1, TOP_K):
        ff = ff + og_ref[j].astype(f32) * gate[:, j:j + 1]
    h = h_ref[...] + g2_ref[...] * ff
    y_ref[...] = _rms(h) * fw_ref[...]


def _final(h, og, gate, mod, fw, row0, seq_len):
    t = h.shape[0]
    tm = FIN_TM
    blk0 = row0 // tm
    return pl.pallas_call(
        _final_kernel,
        grid=(t // tm,),
        in_specs=[pl.BlockSpec((tm, D_MODEL), lambda m: (m, 0)),
                  pl.BlockSpec((TOP_K, tm, D_MODEL), lambda m: (0, blk0 + m, 0)),
                  pl.BlockSpec((tm, LANES), lambda m: (blk0 + m, 0)),
                  _mod_spec(mod, MOD_G2, tm, seq_len),
                  pl.BlockSpec((1, D_MODEL), lambda m: (0, 0))],
        out_specs=pl.BlockSpec((tm, D_MODEL), lambda m: (m, 0)),
        out_shape=jax.ShapeDtypeStruct((t, D_MODEL), f32),
        compiler_params=_cparams(("arbitrary",)),
        name="final_norm",
    )(h, og, gate, mod, fw)


def _count_le(bounds, x):
    return jnp.sum((bounds[None, :] <= x[:, None]).astype(jnp.int32), axis=1)


def _routing(top_e):
    t = top_e.shape[0]
    n_pairs = t * TOP_K
    e_flat = top_e.reshape(-1)
    onehot = (e_flat[:, None] == jnp.arange(N_EXPERTS, dtype=e_flat.dtype)[None, :]).astype(jnp.int32)
    csum = jnp.cumsum(onehot, axis=0)
    counts = csum[-1]
    nblk = (counts + MOE_RB - 1) // MOE_RB
    blk_end = jnp.cumsum(nblk)
    blk_start = blk_end - nblk
    row_start = blk_start * MOE_RB
    dest = jnp.sum(onehot * (csum + row_start[None, :]), axis=1) - 1
    n_rows = (n_pairs // MOE_RB + N_EXPERTS) * MOE_RB
    pair_bits = (n_pairs - 1).bit_length()
    pair_ids = jnp.arange(n_pairs, dtype=jnp.int32)
    order = jnp.sort(e_flat * (1 << pair_bits) + pair_ids) & ((1 << pair_bits) - 1)
    raw_start = jnp.cumsum(counts) - counts
    r = jnp.arange(n_rows, dtype=jnp.int32)
    begun = row_start[None, :] <= r[:, None]

    def at_row(table):
        steps = jnp.diff(table, prepend=0)
        return jnp.sum(jnp.where(begun, steps[None, :], 0), axis=1)

    src_pos = r - at_row(row_start - raw_start)
    is_real = r < at_row(row_start + counts)
    src_pair = order[jnp.clip(src_pos, 0, n_pairs - 1)]
    src_tok = jnp.where(is_real, src_pair // TOP_K, r % t).astype(jnp.int32)

    n_items = N_EXPERTS + n_rows // MOE_TM
    nit = (nblk + MOE_KB - 1) // MOE_KB
    it_end = jnp.cumsum(nit)
    it_start = it_end - nit
    idx = jnp.arange(n_items, dtype=jnp.int32)
    valid = idx < it_end[-1]
    last_valid = jnp.maximum(it_end[-1] - 1, 0)
    idx_c = jnp.minimum(idx, last_valid)
    e_of = jnp.minimum(_count_le(it_end, idx_c), N_EXPERTS - 1).astype(jnp.int32)
    k = idx_c - it_start[e_of]
    item_nb = jnp.where(valid, jnp.clip(nblk[e_of] - k * MOE_KB, 0, MOE_KB), 0).astype(jnp.int32)
    spare = idx - it_end[-1]
    tail_blk = blk_end[-1] + spare * MOE_KB
    item_nz = jnp.where(valid, 0, jnp.clip(n_rows // MOE_RB - tail_blk, 0, MOE_KB)).astype(jnp.int32)
    item_blk = jnp.where(valid, blk_start[e_of] + k * MOE_KB, tail_blk).astype(jnp.int32)
    items = (e_of, item_blk, item_nb, valid.astype(jnp.int32), item_nz)
    return dest.reshape(t, TOP_K), src_tok, items


def kernel(x_prompt, x_sample, state_gla, state_conv, c_prompt, c_sample, w_ada, b_ada, norm1_w,
           w_in, w_gk_up, b_gk, gla_norm_w, w_conv, w_out, norm2_w, w_router, b_router, w_up, b_up,
           w_down, b_down, final_norm_w):
    n_p, seq_p, _ = x_prompt.shape
    n_s, seq_s, _ = x_sample.shape
    t_p, t_s = n_p * seq_p, n_s * seq_s
    assert seq_s == GS_SEQ and w_ada.shape[0] == 1

    c_all = jnp.concatenate([jnp.repeat(c_sample, seq_s, axis=0), c_prompt], axis=0)
    pad = (-c_all.shape[0]) % BF16_ROWS
    c_all = jnp.pad(c_all, ((0, pad), (0, 0)))
    mod = _ada(c_all, w_ada[0], b_ada[0])
    mod_s = mod
    mod_p = mod[t_s:t_s + n_p].reshape(n_p, 1, N_MOD * D_MODEL)

    w_main, w_alow = _prep_w_in(w_in[0].T)
    w_gk = jnp.pad(w_gk_up[0], ((0, LANES - GLA_LOW_RANK), (0, 0))).astype(bf16)
    bgk = b_gk[0].reshape(1, GLA_DK)
    wo = w_out[0].astype(bf16)
    wr = jnp.pad(w_router[0], ((0, 0), (0, LANES - N_EXPERTS)))
    wr_hi = wr.astype(bf16)
    wr_lo = (wr - wr_hi.astype(f32)).astype(bf16)
    br = jnp.pad(b_router[0], (0, LANES - N_EXPERTS), constant_values=ROUTER_PAD).reshape(1, LANES)
    n1w = norm1_w[0].reshape(1, D_MODEL)
    n2w = norm2_w[0].reshape(1, D_MODEL)
    gnw = gla_norm_w[0].reshape(1, GLA_HV)
    fw = final_norm_w.reshape(1, D_MODEL)
    wc = w_conv[0]

    xp = x_prompt.reshape(t_p, D_MODEL)
    xs_ = x_sample.reshape(t_s, D_MODEL)

    proj_p, la_p = _in_proj(xp, mod_p, n1w, w_main, w_alow, w_gk, bgk, 1024, seq_p)
    proj_s, la_s = _in_proj(xs_, mod_s, n1w, w_main, w_alow, w_gk, bgk, t_s, seq_s)

    gla0 = jnp.zeros((n_p, GLA_HEADS, GLA_HK, GLA_HV), f32)
    conv0 = jnp.zeros((n_p, CONV_K - 1, CONV_CH), f32)
    oa_p, gla_p = _gla_scan(proj_p, la_p, gnw, gla0, n_p, seq_p)
    oa_s, gla_s = _gla_step(proj_s, la_s, gnw, state_gla[0])

    hn = jnp.zeros((HN_BUF_FACTOR * (t_p + t_s), D_MODEL), bf16)
    h_p, hn, gate_p, eid_p, ut_p = _post(xp, oa_p, proj_p, conv0, wc, mod_p, n2w, wo,
                                         wr_hi, wr_lo, br, hn, 0, True, seq_p)
    prev_s = (jnp.repeat(state_conv[0][:, 0], seq_s, axis=0), jnp.repeat(state_conv[0][:, 1], seq_s, axis=0))
    h_s, hn, gate_s, eid_s, u_s = _post(xs_, oa_s, proj_s, prev_s, wc, mod_s, n2w, wo,
                                        wr_hi, wr_lo, br, hn, t_p, False, seq_s)
    conv_p = ut_p.reshape(n_p, seq_p // POST_TM, U_TAIL, CONV_CH)[:, -1, U_TAIL - (CONV_K - 1):]
    conv_s = u_s.reshape(n_s, seq_s, CONV_CH)[:, seq_s - (CONV_K - 1):]

    gate = jnp.concatenate([gate_p, gate_s], axis=0)
    top_e = jnp.concatenate([eid_p, eid_s], axis=0)[:, :TOP_K]
    dest, src_tok, items = _routing(top_e)
    x_rows = hn[src_tok]
    out_rows = _moe(x_rows, items, w_up[0], b_up[0], w_down[0], b_down[0])
    og = out_rows[dest.T.reshape(-1)].reshape(TOP_K, t_p + t_s, D_MODEL)

    y_p = _final(h_p, og, gate, mod_p, fw, 0, seq_p)
    y_s = _final(h_s, og, gate, mod_s, fw, t_p, seq_s)

    return (y_p.reshape(n_p, seq_p, D_MODEL), y_s.reshape(n_s, seq_s, D_MODEL),
            gla_p[None], conv_p[None], gla_s[None], conv_s[None])
```

```python
import functools

import jax
import jax.numpy as jnp
from jax import lax
from jax.experimental import pallas as pl
from jax.experimental.pallas import tpu as pltpu

f32 = jnp.float32
bf16 = jnp.bfloat16

D_MODEL = 2048
N_MOD = 6
GLA_HEADS = 4
GLA_DK = 512
GLA_DV = 1024
GLA_HK = 128
GLA_HV = 256
GLA_LOW_RANK = 16
GLA_TAU = 16.0
GLA_CHUNK = 64
CONV_CH = 1024
CONV_K = 3
N_EXPERTS = 32
TOP_K = 4
D_FF = 2048
SWIGLU_LIMIT = 7.0
SWIGLU_ALPHA = 1.702
EPS = 1e-6

LANES = 128
SUBLANES = 8
BF16_ROWS = 2 * SUBLANES
VMEM_LIMIT = 56 * 1024 * 1024

OFF_Q = 0
OFF_K = OFF_Q + GLA_DK
OFF_V = OFF_K + GLA_DK
OFF_G = OFF_V + GLA_DV
OFF_CB = OFF_G + GLA_DV
OFF_CC = OFF_CB + CONV_CH
OFF_CH = OFF_CC + CONV_CH
OFF_GA = OFF_CH + CONV_CH
OFF_GB = OFF_GA + D_MODEL
D_MAIN = OFF_GB + D_MODEL

ROUTER_PAD = -1e30
HN_BUF_FACTOR = 2

HIGHEST = lax.Precision.HIGHEST


def _cparams(sem):
    return pltpu.CompilerParams(dimension_semantics=sem, vmem_limit_bytes=VMEM_LIMIT)


def _dot(a, b):
    return jnp.dot(a, b, preferred_element_type=f32)


def _dot_nt(a, b):
    return lax.dot_general(a, b, (((1,), (1,)), ((), ())), preferred_element_type=f32)


def _dot_exact(a, b):
    return jnp.dot(a, b, precision=HIGHEST, preferred_element_type=f32)


def _sigmoid_tanh(x):
    return 0.5 * jnp.tanh(0.5 * x) + 0.5


def _rms(x):
    return x * lax.rsqrt(jnp.mean(x * x, axis=-1, keepdims=True) + EPS)


ADA_TN = 2048


def _ada_kernel(c_ref, w_ref, b_ref, o_ref):
    c = c_ref[...]
    s = (c * jax.nn.sigmoid(c)).astype(bf16)
    o_ref[...] = _dot(s, w_ref[...].astype(bf16)) + b_ref[...]


def _ada(c, w, b):
    rows = c.shape[0]
    n = w.shape[1]
    return pl.pallas_call(
        _ada_kernel,
        grid=(n // ADA_TN,),
        in_specs=[pl.BlockSpec((rows, D_MODEL), lambda j: (0, 0)),
                  pl.BlockSpec((D_MODEL, ADA_TN), lambda j: (0, j)),
                  pl.BlockSpec((1, ADA_TN), lambda j: (0, j))],
        out_specs=pl.BlockSpec((rows, ADA_TN), lambda j: (0, j)),
        out_shape=jax.ShapeDtypeStruct((rows, n), f32),
        compiler_params=_cparams(("arbitrary",)),
        name="adaln",
    )(c, w, b.reshape(1, n))


PREP_TN = 1024


def _prep_kernel(a_ref, b_ref, low_ref, main_ref, alow_ref):
    j = pl.program_id(0)
    first_shifted = OFF_G // PREP_TN

    @pl.when(j < first_shifted)
    def _():
        main_ref[...] = a_ref[...].T.astype(bf16)

    @pl.when(j >= first_shifted)
    def _():
        wide = jnp.concatenate([a_ref[...], b_ref[...]], axis=0)
        main_ref[...] = wide[GLA_LOW_RANK:GLA_LOW_RANK + PREP_TN].T.astype(bf16)

    @pl.when(j == 0)
    def _():
        low = low_ref[...].T
        lane = lax.broadcasted_iota(jnp.int32, low.shape, 1)
        alow_ref[...] = jnp.where(lane < GLA_LOW_RANK, low, 0.0).astype(bf16)


def _prep_w_in(wt):
    d = wt.shape[1]
    return pl.pallas_call(
        _prep_kernel,
        grid=(D_MAIN // PREP_TN,),
        in_specs=[pl.BlockSpec((PREP_TN, d), lambda j: (j, 0)),
                  pl.BlockSpec((GLA_LOW_RANK, d), lambda j: ((j + 1) * (PREP_TN // GLA_LOW_RANK), 0)),
                  pl.BlockSpec((LANES, d), lambda j: (OFF_G // LANES, 0))],
        out_specs=[pl.BlockSpec((d, PREP_TN), lambda j: (0, j)),
                   pl.BlockSpec((d, LANES), lambda j: (0, 0))],
        out_shape=[jax.ShapeDtypeStruct((d, D_MAIN), bf16),
                   jax.ShapeDtypeStruct((d, LANES), bf16)],
        compiler_params=_cparams(("arbitrary",)),
        name="prep_w_in",
    )(wt, wt, wt)


IN_TN = 1024


def _in_kernel(x_ref, sc_ref, sh_ref, nw_ref, w_ref, wa_ref, wgk_ref, bgk_ref,
               proj_ref, la_ref, xn_ref):
    @pl.when(pl.program_id(1) == 0)
    def _():
        xn = _rms(x_ref[...]) * nw_ref[...] * (1.0 + sc_ref[...]) + sh_ref[...]
        xnb = xn.astype(bf16)
        xn_ref[...] = xnb
        a_low = _dot(xnb, wa_ref[...])
        z = _dot(a_low.astype(bf16), wgk_ref[...]) + bgk_ref[...]
        la_ref[...] = (jnp.minimum(z, 0.0) - jnp.log(1.0 + jnp.exp(-jnp.abs(z)))) * (1.0 / GLA_TAU)

    proj_ref[...] = _dot(xn_ref[...], w_ref[...]).astype(bf16)


MOD_SH1, MOD_SC1, MOD_G1, MOD_SH2, MOD_SC2, MOD_G2 = range(N_MOD)


def _mod_spec(mod, which, tm, seq_len):
    if mod.ndim == 3:
        tiles_per_seq = seq_len // tm
        return pl.BlockSpec((None, 1, D_MODEL), lambda *g: (g[0] // tiles_per_seq, 0, which))
    return pl.BlockSpec((tm, D_MODEL), lambda *g: (g[0], which))


def _in_proj(x2, mod, nw, w_main, w_alow, w_gk, b_gk, tm, seq_len):
    t = x2.shape[0]
    return pl.pallas_call(
        _in_kernel,
        grid=(t // tm, D_MAIN // IN_TN),
        in_specs=[pl.BlockSpec((tm, D_MODEL), lambda m, n: (m, 0)),
                  _mod_spec(mod, MOD_SC1, tm, seq_len), _mod_spec(mod, MOD_SH1, tm, seq_len),
                  pl.BlockSpec((1, D_MODEL), lambda m, n: (0, 0)),
                  pl.BlockSpec((D_MODEL, IN_TN), lambda m, n: (0, n)),
                  pl.BlockSpec((D_MODEL, LANES), lambda m, n: (0, 0)),
                  pl.BlockSpec((LANES, GLA_DK), lambda m, n: (0, 0)),
                  pl.BlockSpec((1, GLA_DK), lambda m, n: (0, 0))],
        out_specs=[pl.BlockSpec((tm, IN_TN), lambda m, n: (m, n)),
                   pl.BlockSpec((tm, GLA_DK), lambda m, n: (m, 0))],
        out_shape=[jax.ShapeDtypeStruct((t, D_MAIN), bf16),
                   jax.ShapeDtypeStruct((t, GLA_DK), f32)],
        scratch_shapes=[pltpu.VMEM((tm, D_MODEL), bf16)],
        compiler_params=_cparams(("arbitrary", "arbitrary")),
        name="in_proj",
    )(x2, mod, mod, nw, w_main, w_alow, w_gk, b_gk)


GLA_R = 256


def _gla_norm_gate(o, nw, g):
    gf = g.astype(f32)
    return _rms(o) * nw * (gf * jax.nn.sigmoid(gf))


def _gla_scan_kernel(q_ref, k_ref, v_ref, g_ref, la_ref, nw_ref, s0_ref, o_ref, sout_ref, s_ref):
    step = pl.program_id(1)

    @pl.when(step == 0)
    def _():
        s_ref[...] = s0_ref[...]

    r, c = GLA_R, GLA_CHUNK
    n_chunks = r // c
    row = lax.broadcasted_iota(jnp.int32, (r, r), 0)
    col = lax.broadcasted_iota(jnp.int32, (r, r), 1)
    same_chunk = (row // c) == (col // c)
    causal = same_chunk & (col <= row)

    la = la_ref[...]
    b = _dot_exact(causal.astype(f32), la)
    b_mid = jnp.concatenate(
        [jnp.broadcast_to(b[i * c + c // 2:i * c + c // 2 + 1], (c, GLA_DK)) for i in range(n_chunks)], axis=0)
    b_last = jnp.concatenate(
        [jnp.broadcast_to(b[i * c + c - 1:i * c + c], (c, GLA_DK)) for i in range(n_chunks)], axis=0)
    q = q_ref[...].astype(f32) * (GLA_HK ** -0.5)
    k = k_ref[...].astype(f32)
    qs = (q * jnp.exp(b - b_mid)).astype(bf16)
    ks = (k * jnp.exp(b_mid - b)).astype(bf16)
    qd = (q * jnp.exp(b)).astype(bf16)
    kd_t = (k * jnp.exp(b_last - b)).T.astype(bf16)
    la_t = la.T
    lane = lax.broadcasted_iota(jnp.int32, (GLA_HK, r), 1)
    nw = nw_ref[...]

    for h in range(GLA_HEADS):
        hk = slice(h * GLA_HK, (h + 1) * GLA_HK)
        hv = slice(h * GLA_HV, (h + 1) * GLA_HV)
        v_h = v_ref[:, hv]
        scores = _dot_nt(qs[:, hk], ks[:, hk])
        o_intra = _dot(jnp.where(causal, scores, 0.0).astype(bf16), v_h)
        s = s_ref[h]
        for i in range(n_chunks):
            rows = slice(i * c, (i + 1) * c)
            in_chunk = (lane // c) == i
            o = o_intra[rows] + _dot(qd[rows, hk], s.astype(bf16))
            o_ref[rows, hv] = _gla_norm_gate(o, nw, g_ref[rows, hv]).astype(bf16)
            decay = jnp.exp(jnp.sum(jnp.where(in_chunk, la_t[hk], 0.0), axis=1, keepdims=True))
            s = decay * s + _dot(jnp.where(in_chunk, kd_t[hk], jnp.zeros_like(kd_t[hk])), v_h)
        s_ref[h] = s

    @pl.when(step == pl.num_programs(1) - 1)
    def _():
        sout_ref[...] = s_ref[...]


def _gla_scan(proj, la, nw, s0, n_seq, seq_len):
    steps = seq_len // GLA_R
    t = n_seq * seq_len

    def rows(b, s):
        return b * steps + s

    return pl.pallas_call(
        _gla_scan_kernel,
        grid=(n_seq, steps),
        in_specs=[pl.BlockSpec((GLA_R, GLA_DK), lambda b, s: (rows(b, s), OFF_Q // GLA_DK)),
                  pl.BlockSpec((GLA_R, GLA_DK), lambda b, s: (rows(b, s), OFF_K // GLA_DK)),
                  pl.BlockSpec((GLA_R, GLA_DV), lambda b, s: (rows(b, s), OFF_V // GLA_DV)),
                  pl.BlockSpec((GLA_R, GLA_DV), lambda b, s: (rows(b, s), OFF_G // GLA_DV)),
                  pl.BlockSpec((GLA_R, GLA_DK), lambda b, s: (rows(b, s), 0)),
                  pl.BlockSpec((1, GLA_HV), lambda b, s: (0, 0)),
                  pl.BlockSpec((None, GLA_HEADS, GLA_HK, GLA_HV), lambda b, s: (b, 0, 0, 0))],
        out_specs=[pl.BlockSpec((GLA_R, GLA_DV), lambda b, s: (rows(b, s), 0)),
                   pl.BlockSpec((None, GLA_HEADS, GLA_HK, GLA_HV), lambda b, s: (b, 0, 0, 0))],
        out_shape=[jax.ShapeDtypeStruct((t, GLA_DV), bf16),
                   jax.ShapeDtypeStruct((n_seq, GLA_HEADS, GLA_HK, GLA_HV), f32)],
        scratch_shapes=[pltpu.VMEM((GLA_HEADS, GLA_HK, GLA_HV), f32)],
        compiler_params=_cparams(("arbitrary", "arbitrary")),
        name="gla_scan",
    )(proj, proj, proj, proj, la, nw, s0)


GS_SEQ = 4
GS_ROWS = 128
GS_B = GS_ROWS // GS_SEQ
GS_STEP_B = 4


def _gla_step_kernel(q_ref, k_ref, v_ref, g_ref, la_ref, nw_ref, s0_ref, o_ref, sout_ref,
                     oacc_ref, qd_ref, kdt_ref, lat_ref):
    j = pl.program_id(1)
    r = GS_ROWS
    row = lax.broadcasted_iota(jnp.int32, (r, r), 0)
    col = lax.broadcasted_iota(jnp.int32, (r, r), 1)

    @pl.when(j == 0)
    def _():
        same_seq = (row // GS_SEQ) == (col // GS_SEQ)
        causal = same_seq & (col <= row)
        la = la_ref[...]
        b = _dot_exact(causal.astype(f32), la)
        b_mid = _dot_exact((same_seq & (col % GS_SEQ <= GS_SEQ // 2)).astype(f32), la)
        b_last = _dot_exact(same_seq.astype(f32), la)
        q = q_ref[...].astype(f32) * (GLA_HK ** -0.5)
        k = k_ref[...].astype(f32)
        qs = (q * jnp.exp(b - b_mid)).astype(bf16)
        ks = (k * jnp.exp(b_mid - b)).astype(bf16)
        qd_ref[...] = (q * jnp.exp(b)).astype(bf16)
        kdt_ref[...] = (k * jnp.exp(b_last - b)).T.astype(bf16)
        lat_ref[...] = la.T
        for h in range(GLA_HEADS):
            hk = slice(h * GLA_HK, (h + 1) * GLA_HK)
            hv = slice(h * GLA_HV, (h + 1) * GLA_HV)
            scores = _dot_nt(qs[:, hk], ks[:, hk])
            oacc_ref[:, hv] = _dot(jnp.where(causal, scores, 0.0).astype(bf16), v_ref[:, hv])

    for bb in range(GS_STEP_B):
        seq = j * GS_STEP_B + bb
        in_seq_lane = (col // GS_SEQ) == seq
        in_seq_row = (row[:, :GLA_HK] // GS_SEQ) == seq
        for h in range(GLA_HEADS):
            hk = slice(h * GLA_HK, (h + 1) * GLA_HK)
            hv = slice(h * GLA_HV, (h + 1) * GLA_HV)
            s0 = s0_ref[bb, h]
            kd_t = kdt_ref[hk, :]
            ds = _dot(jnp.where(in_seq_lane, kd_t, jnp.zeros_like(kd_t)), v_ref[:, hv])
            decay = jnp.exp(jnp.sum(jnp.where(in_seq_lane, lat_ref[hk, :], 0.0), axis=1, keepdims=True))
            sout_ref[bb, h] = decay * s0 + ds
            qd = qd_ref[:, hk]
            oacc_ref[:, hv] += _dot(jnp.where(in_seq_row, qd, jnp.zeros_like(qd)), s0.astype(bf16))

    @pl.when(j == pl.num_programs(1) - 1)
    def _():
        nw = nw_ref[...]
        for h in range(GLA_HEADS):
            hv = slice(h * GLA_HV, (h + 1) * GLA_HV)
            o_ref[:, hv] = _gla_norm_gate(oacc_ref[:, hv], nw, g_ref[:, hv]).astype(bf16)


def _gla_step(proj, la, nw, s0):
    n_seq = s0.shape[0]
    t = n_seq * GS_SEQ
    groups = t // GS_ROWS
    steps = GS_B // GS_STEP_B
    return pl.pallas_call(
        _gla_step_kernel,
        grid=(groups, steps),
        in_specs=[pl.BlockSpec((GS_ROWS, GLA_DK), lambda g, j: (g, OFF_Q // GLA_DK)),
                  pl.BlockSpec((GS_ROWS, GLA_DK), lambda g, j: (g, OFF_K // GLA_DK)),
                  pl.BlockSpec((GS_ROWS, GLA_DV), lambda g, j: (g, OFF_V // GLA_DV)),
                  pl.BlockSpec((GS_ROWS, GLA_DV), lambda g, j: (g, OFF_G // GLA_DV)),
                  pl.BlockSpec((GS_ROWS, GLA_DK), lambda g, j: (g, 0)),
                  pl.BlockSpec((1, GLA_HV), lambda g, j: (0, 0)),
                  pl.BlockSpec((GS_STEP_B, GLA_HEADS, GLA_HK, GLA_HV), lambda g, j: (g * steps + j, 0, 0, 0))],
        out_specs=[pl.BlockSpec((GS_ROWS, GLA_DV), lambda g, j: (g, 0)),
                   pl.BlockSpec((GS_STEP_B, GLA_HEADS, GLA_HK, GLA_HV), lambda g, j: (g * steps + j, 0, 0, 0))],
        out_shape=[jax.ShapeDtypeStruct((t, GLA_DV), bf16),
                   jax.ShapeDtypeStruct((n_seq, GLA_HEADS, GLA_HK, GLA_HV), f32)],
        scratch_shapes=[pltpu.VMEM((GS_ROWS, GLA_DV), f32),
                        pltpu.VMEM((GS_ROWS, GLA_DK), bf16),
                        pltpu.VMEM((GLA_DK, GS_ROWS), bf16),
                        pltpu.VMEM((GLA_DK, GS_ROWS), f32)],
        compiler_params=_cparams(("arbitrary", "arbitrary")),
        name="gla_step",
    )(proj, proj, proj, proj, la, nw, s0)


POST_TM = 512
HALO = BF16_ROWS
U_TAIL = SUBLANES


def _post_kernel(long_seq, tiles_per_seq, *refs):
    if long_seq:
        (x_ref, oa_ref, cb_ref, cc_ref, ch_ref, ga_ref, gb_ref, hcc_ref, hch_ref, cbuf_ref,
         wc_ref, g1_ref, sc_ref, sh_ref, nw_ref, wo_ref, wrh_ref, wrl_ref, br_ref, _hn_alias,
         h_ref, hn_ref, gate_ref, eid_ref, u_ref) = refs
    else:
        (x_ref, oa_ref, cb_ref, cc_ref, ch_ref, ga_ref, gb_ref, p0_ref, p1_ref,
         wc_ref, g1_ref, sc_ref, sh_ref, nw_ref, wo_ref, wrh_ref, wrl_ref, br_ref, _hn_alias,
         h_ref, hn_ref, gate_ref, eid_ref, u_ref) = refs
    tm = x_ref.shape[0]
    u = cc_ref[...].astype(f32) * ch_ref[...].astype(f32)
    row = lax.broadcasted_iota(jnp.int32, (tm, CONV_CH), 0)
    if long_seq:
        pos = row
        first = (pl.program_id(0) % tiles_per_seq) == 0
        halo = hcc_ref[HALO - 2:HALO, :].astype(f32) * hch_ref[HALO - 2:HALO, :].astype(f32)
        cbuf = cbuf_ref[...]
        p0 = jnp.where(first, cbuf[0:1], halo[0:1])
        p1 = jnp.where(first, cbuf[1:2], halo[1:2])
        u_ref[...] = u[tm - U_TAIL:tm]
    else:
        pos = row % GS_SEQ
        p0 = p0_ref[...]
        p1 = p1_ref[...]
        u_ref[...] = u
    u1 = jnp.where(pos == 0, p1, pltpu.roll(u, 1, 0))
    u2 = jnp.where(pos == 0, p0, jnp.where(pos == 1, p1, pltpu.roll(u, 2, 0)))
    wc = wc_ref[...]
    conv = wc[0:1] * u2 + wc[1:2] * u1 + wc[2:3] * u
    ob = (cb_ref[...].astype(f32) * conv).astype(bf16)

    ya = _dot(oa_ref[...], wo_ref[0:GLA_DV, :])
    yb = _dot(ob, wo_ref[GLA_DV:GLA_DV + CONV_CH, :])
    y = _sigmoid_tanh(ga_ref[...].astype(f32)) * ya + _sigmoid_tanh(gb_ref[...].astype(f32)) * yb
    h = x_ref[...] + g1_ref[...] * y
    h_ref[...] = h
    hn = _rms(h) * nw_ref[...] * (1.0 + sc_ref[...]) + sh_ref[...]
    hn_hi = hn.astype(bf16)
    hn_ref[...] = hn_hi
    hn_lo = (hn - hn_hi.astype(f32)).astype(bf16)
    lg = (_dot(hn_hi, wrh_ref[...]) + _dot(hn_hi, wrl_ref[...]) + _dot(hn_lo, wrh_ref[...])
          + br_ref[...])

    lane = lax.broadcasted_iota(jnp.int32, (tm, LANES), 1)
    lane_f = lane.astype(f32)
    vals, idxs = [], []
    for _ in range(TOP_K):
        m = jnp.max(lg, axis=-1, keepdims=True)
        idx = jnp.min(jnp.where(lg == m, lane_f, float(LANES)), axis=-1, keepdims=True)
        vals.append(m)
        idxs.append(idx)
        lg = jnp.where(lane_f == idx, -jnp.inf, lg)
    exps = [jnp.exp(v - vals[0]) for v in vals]
    den = exps[0] + exps[1] + exps[2] + exps[3]
    gate = jnp.zeros((tm, LANES), f32)
    eid = jnp.zeros((tm, LANES), f32)
    for j in range(TOP_K):
        gate = jnp.where(lane == j, exps[j] / den, gate)
        eid = jnp.where(lane == j, idxs[j], eid)
    gate_ref[...] = gate
    eid_ref[...] = eid.astype(jnp.int32)


def _post(x2, oa, proj, prev, wc, mod, nw, wo, wrh, wrl, br, hn_buf, row0, long_seq, seq_len):
    t = x2.shape[0]
    tm = POST_TM
    n_tiles = t // tm
    blk0 = row0 // tm
    tiles_per_seq = seq_len // tm if long_seq else 1

    def colblk(off, width):
        return pl.BlockSpec((tm, width), lambda m: (m, off // width))

    const = lambda shape: pl.BlockSpec(shape, lambda m: tuple(0 for _ in shape))
    if long_seq:
        halo_rows = lambda m: jnp.maximum(m * (tm // HALO) - 1, 0)
        prev_specs = [pl.BlockSpec((HALO, CONV_CH), lambda m: (halo_rows(m), OFF_CC // CONV_CH)),
                      pl.BlockSpec((HALO, CONV_CH), lambda m: (halo_rows(m), OFF_CH // CONV_CH)),
                      pl.BlockSpec((None, CONV_K - 1, CONV_CH), lambda m: (m // tiles_per_seq, 0, 0))]
        prev_args = [proj, proj, prev]
        u_spec = pl.BlockSpec((None, U_TAIL, CONV_CH), lambda m: (m, 0, 0))
        u_shape = jax.ShapeDtypeStruct((n_tiles, U_TAIL, CONV_CH), f32)
    else:
        prev_specs = [pl.BlockSpec((tm, CONV_CH), lambda m: (m, 0))] * 2
        prev_args = list(prev)
        u_spec = pl.BlockSpec((tm, CONV_CH), lambda m: (m, 0))
        u_shape = jax.ShapeDtypeStruct((t, CONV_CH), f32)
    n_in = 7 + len(prev_specs) + 10
    return pl.pallas_call(
        functools.partial(_post_kernel, long_seq, tiles_per_seq),
        grid=(n_tiles,),
        in_specs=[pl.BlockSpec((tm, D_MODEL), lambda m: (m, 0)),
                  pl.BlockSpec((tm, GLA_DV), lambda m: (m, 0)),
                  colblk(OFF_CB, CONV_CH), colblk(OFF_CC, CONV_CH), colblk(OFF_CH, CONV_CH),
                  colblk(OFF_GA, D_MODEL), colblk(OFF_GB, D_MODEL)]
                 + prev_specs
                 + [const((CONV_K, CONV_CH)), _mod_spec(mod, MOD_G1, tm, seq_len),
                    _mod_spec(mod, MOD_SC2, tm, seq_len), _mod_spec(mod, MOD_SH2, tm, seq_len), const((1, D_MODEL)),
                    const((GLA_DV + CONV_CH, D_MODEL)), const((D_MODEL, LANES)), const((D_MODEL, LANES)),
                    const((1, LANES)), pl.BlockSpec(memory_space=pl.ANY)],
        out_specs=[pl.BlockSpec((tm, D_MODEL), lambda m: (m, 0)),
                   pl.BlockSpec((tm, D_MODEL), lambda m: (blk0 + m, 0)),
                   pl.BlockSpec((tm, LANES), lambda m: (m, 0)),
                   pl.BlockSpec((tm, LANES), lambda m: (m, 0)),
                   u_spec],
        out_shape=[jax.ShapeDtypeStruct((t, D_MODEL), f32),
                   jax.ShapeDtypeStruct(hn_buf.shape, bf16),
                   jax.ShapeDtypeStruct((t, LANES), f32),
                   jax.ShapeDtypeStruct((t, LANES), jnp.int32),
                   u_shape],
        input_output_aliases={n_in - 1: 1},
        compiler_params=_cparams(("arbitrary",)),
        name="post_mix",
    )(x2, oa, proj, proj, proj, proj, proj, *prev_args, wc, mod, mod, mod, nw, wo, wrh, wrl, br, hn_buf)


MOE_RB = 128
MOE_TM = 1536
MOE_TN = 512
MOE_KB = MOE_TM // MOE_RB
MOE_NT = D_FF // MOE_TN
MOE_BIG = 8


def _moe_kernel(ie_ref, ib_ref, in_ref, iv_ref, iz_ref,
                x_hbm, wg_ref, wl_ref, wd_ref, bg_ref, bl_ref, bd_ref, out_hbm,
                xbuf, hbuf, obuf, wcast, sem_in, sem_out):
    i = pl.program_id(0)
    s = pl.program_id(1)
    n_items = pl.num_programs(0)
    nb = in_ref[i]

    def x_copy(item, jb, k):
        return pltpu.make_async_copy(
            x_hbm.at[pl.ds((ib_ref[item] + jb) * MOE_RB, MOE_RB), pl.ds(k * MOE_TN, MOE_TN)],
            xbuf.at[k, pl.ds(jb * MOE_RB, MOE_RB)], sem_in)

    def out_copy(item, jb, k):
        return pltpu.make_async_copy(
            obuf.at[k, pl.ds(jb * MOE_RB, MOE_RB)],
            out_hbm.at[pl.ds((ib_ref[item] + jb) * MOE_RB, MOE_RB), pl.ds(k * MOE_TN, MOE_TN)], sem_out)

    def for_blocks(n, fn):
        def body(jb, carry):
            for k in range(MOE_NT):
                fn(jb, k)
            return carry

        lax.fori_loop(0, n, body, 0)

    def for_rows(fn):
        def body(j, carry):
            fn(pl.multiple_of(j * (MOE_BIG * MOE_RB), MOE_BIG * MOE_RB), MOE_BIG * MOE_RB)
            return carry

        lax.fori_loop(0, nb // MOE_BIG, body, 0)
        done = (nb // MOE_BIG) * MOE_BIG
        size = MOE_BIG // 2
        while size >= 1:
            @pl.when((nb & size) != 0)
            def _(done=done, size=size):
                fn(pl.multiple_of(done * MOE_RB, MOE_RB), size * MOE_RB)

            done = done + (nb & size)
            size //= 2

    @pl.when(s == 0)
    def _():
        @pl.when(i == 0)
        def _():
            for_blocks(nb, lambda jb, k: x_copy(i, jb, k).start())

        for_blocks(nb, lambda jb, k: x_copy(i, jb, k).wait())

    @pl.when(s < MOE_NT)
    def _():
        wcast[:, 0:MOE_TN] = wg_ref[...].astype(bf16)
        wcast[:, MOE_TN:2 * MOE_TN] = wl_ref[...].astype(bf16)
        bg = bg_ref[...]
        bl = bl_ref[...]

        def up(r0, size):
            rows = pl.ds(r0, size)
            xb = jnp.concatenate([xbuf[k, rows, :] for k in range(MOE_NT)], axis=1)
            h = _dot(xb, wcast[...])
            hg = jnp.minimum(h[:, 0:MOE_TN] + bg, SWIGLU_LIMIT)
            hl = jnp.clip(h[:, MOE_TN:2 * MOE_TN] + bl, -SWIGLU_LIMIT, SWIGLU_LIMIT)
            hbuf[s, rows, :] = ((hl + 1.0) * hg * jax.nn.sigmoid(SWIGLU_ALPHA * hg)).astype(bf16)

        for_rows(up)

    @pl.when(s == MOE_NT)
    def _():
        @pl.when(i > 0)
        def _():
            prev = jnp.maximum(i - 1, 0)
            for_blocks(in_ref[prev], lambda jb, k: out_copy(prev, jb, k).wait())

        def fill(jb, k):
            obuf[k, pl.ds(pl.multiple_of(jb * MOE_RB, MOE_RB), MOE_RB), :] = jnp.zeros((MOE_RB, MOE_TN), bf16)
            out_copy(i, jb, k).start()

        nz = iz_ref[i]
        for_blocks(nz, fill)
        for_blocks(nz, lambda jb, k: out_copy(i, jb, k).wait())

        @pl.when(i + 1 < n_items)
        def _():
            nxt = jnp.minimum(i + 1, n_items - 1)
            for_blocks(in_ref[nxt], lambda jb, k: x_copy(nxt, jb, k).start())

    @pl.when(s >= MOE_NT)
    def _():
        wcast[:, 0:MOE_TN] = wd_ref[...].astype(bf16)
        bd = bd_ref[...]

        def down(r0, size):
            rows = pl.ds(r0, size)
            hb = jnp.concatenate([hbuf[k, rows, :] for k in range(MOE_NT)], axis=1)
            obuf[s - MOE_NT, rows, :] = (_dot(hb, wcast[:, 0:MOE_TN]) + bd).astype(bf16)

        for_rows(down)

    @pl.when(s == 2 * MOE_NT - 1)
    def _():
        for_blocks(nb, lambda jb, k: out_copy(i, jb, k).start())

        @pl.when(i == n_items - 1)
        def _():
            for_blocks(nb, lambda jb, k: out_copy(i, jb, k).wait())


def _moe(xs, items, w_up, b_up, w_down, b_down):
    rows = xs.shape[0]
    n_items = items[0].shape[0]

    last = MOE_NT - 1

    def up_tile(i, s, tabs):
        valid = tabs[3][i]
        return jnp.minimum(s, last) * valid + last * (1 - valid)

    def down_tile(i, s, tabs):
        valid = tabs[3][i]
        return jnp.maximum(s - MOE_NT, 0) * valid + last * (1 - valid)

    def expert(i, tabs):
        return tabs[0][i]

    grid_spec = pltpu.PrefetchScalarGridSpec(
        num_scalar_prefetch=len(items),
        grid=(n_items, 2 * MOE_NT),
        in_specs=[pl.BlockSpec(memory_space=pl.ANY),
                  pl.BlockSpec((None, D_MODEL, MOE_TN), lambda i, s, *t: (expert(i, t), 0, up_tile(i, s, t))),
                  pl.BlockSpec((None, D_MODEL, MOE_TN), lambda i, s, *t: (expert(i, t), 0, MOE_NT + up_tile(i, s, t))),
                  pl.BlockSpec((None, D_FF, MOE_TN), lambda i, s, *t: (expert(i, t), 0, down_tile(i, s, t))),
                  pl.BlockSpec((None, 1, MOE_TN), lambda i, s, *t: (expert(i, t), 0, up_tile(i, s, t))),
                  pl.BlockSpec((None, 1, MOE_TN), lambda i, s, *t: (expert(i, t), 0, MOE_NT + up_tile(i, s, t))),
                  pl.BlockSpec((None, 1, MOE_TN), lambda i, s, *t: (expert(i, t), 0, down_tile(i, s, t)))],
        out_specs=pl.BlockSpec(memory_space=pl.ANY),
        scratch_shapes=[pltpu.VMEM((MOE_NT, MOE_TM, MOE_TN), bf16),
                        pltpu.VMEM((MOE_NT, MOE_TM, MOE_TN), bf16),
                        pltpu.VMEM((MOE_NT, MOE_TM, MOE_TN), bf16),
                        pltpu.VMEM((D_MODEL, 2 * MOE_TN), bf16),
                        pltpu.SemaphoreType.DMA(()),
                        pltpu.SemaphoreType.DMA(())],
    )
    return pl.pallas_call(
        _moe_kernel,
        grid_spec=grid_spec,
        out_shape=jax.ShapeDtypeStruct((rows, D_MODEL), bf16),
        compiler_params=_cparams(("arbitrary", "arbitrary")),
        name="moe_experts",
    )(*items,
      xs, w_up, w_up, w_down,
      b_up.reshape(N_EXPERTS, 1, 2 * D_FF), b_up.reshape(N_EXPERTS, 1, 2 * D_FF),
      b_down.reshape(N_EXPERTS, 1, D_MODEL))


FIN_TM = 512


def _final_kernel(h_ref, og_ref, gate_ref, g2_ref, fw_ref, y_ref):
    gate = gate_ref[...]
    ff = og_ref[0].astype(f32) * gate[:, 0:1]
    for j in range(1, TOP_K):
        ff = ff + og_ref[j].astype(f32) * gate[:, j:j + 1]
    h = h_ref[...] + g2_ref[...] * ff
    y_ref[...] = _rms(h) * fw_ref[...]


def _final(h, og, gate, mod, fw, row0, seq_len):
    t = h.shape[0]
    tm = FIN_TM
    blk0 = row0 // tm
    return pl.pallas_call(
        _final_kernel,
        grid=(t // tm,),
        in_specs=[pl.BlockSpec((tm, D_MODEL), lambda m: (m, 0)),
                  pl.BlockSpec((TOP_K, tm, D_MODEL), lambda m: (0, blk0 + m, 0)),
                  pl.BlockSpec((tm, LANES), lambda m: (blk0 + m, 0)),
                  _mod_spec(mod, MOD_G2, tm, seq_len),
                  pl.BlockSpec((1, D_MODEL), lambda m: (0, 0))],
        out_specs=pl.BlockSpec((tm, D_MODEL), lambda m: (m, 0)),
        out_shape=jax.ShapeDtypeStruct((t, D_MODEL), f32),
        compiler_params=_cparams(("arbitrary",)),
        name="final_norm",
    )(h, og, gate, mod, fw)


def _count_le(bounds, x):
    return jnp.sum((bounds[None, :] <= x[:, None]).astype(jnp.int32), axis=1)


def _routing(top_e):
    t = top_e.shape[0]
    n_pairs = t * TOP_K
    e_flat = top_e.reshape(-1)
    onehot = (e_flat[:, None] == jnp.arange(N_EXPERTS, dtype=e_flat.dtype)[None, :]).astype(jnp.int32)
    csum = jnp.cumsum(onehot, axis=0)
    counts = csum[-1]
    nblk = (counts + MOE_RB - 1) // MOE_RB
    blk_end = jnp.cumsum(nblk)
    blk_start = blk_end - nblk
    row_start = blk_start * MOE_RB
    dest = jnp.sum(onehot * (csum + row_start[None, :]), axis=1) - 1
    n_rows = (n_pairs // MOE_RB + N_EXPERTS) * MOE_RB
    pair_bits = (n_pairs - 1).bit_length()
    pair_ids = jnp.arange(n_pairs, dtype=jnp.int32)
    order = jnp.sort(e_flat * (1 << pair_bits) + pair_ids) & ((1 << pair_bits) - 1)
    raw_start = jnp.cumsum(counts) - counts
    r = jnp.arange(n_rows, dtype=jnp.int32)
    begun = row_start[None, :] <= r[:, None]

    def at_row(table):
        steps = jnp.diff(table, prepend=0)
        return jnp.sum(jnp.where(begun, steps[None, :], 0), axis=1)

    src_pos = r - at_row(row_start - raw_start)
    is_real = r < at_row(row_start + counts)
    src_pair = order[jnp.clip(src_pos, 0, n_pairs - 1)]
    src_tok = jnp.where(is_real, src_pair // TOP_K, r % t).astype(jnp.int32)

    n_items = N_EXPERTS + n_rows // MOE_TM
    nit = (nblk + MOE_KB - 1) // MOE_KB
    it_end = jnp.cumsum(nit)
    it_start = it_end - nit
    idx = jnp.arange(n_items, dtype=jnp.int32)
    valid = idx < it_end[-1]
    last_valid = jnp.maximum(it_end[-1] - 1, 0)
    idx_c = jnp.minimum(idx, last_valid)
    e_of = jnp.minimum(_count_le(it_end, idx_c), N_EXPERTS - 1).astype(jnp.int32)
    k = idx_c - it_start[e_of]
    item_nb = jnp.where(valid, jnp.clip(nblk[e_of] - k * MOE_KB, 0, MOE_KB), 0).astype(jnp.int32)
    spare = idx - it_end[-1]
    tail_blk = blk_end[-1] + spare * MOE_KB
    item_nz = jnp.where(valid, 0, jnp.clip(n_rows // MOE_RB - tail_blk, 0, MOE_KB)).astype(jnp.int32)
    item_blk = jnp.where(valid, blk_start[e_of] + k * MOE_KB, tail_blk).astype(jnp.int32)
    items = (e_of, item_blk, item_nb, valid.astype(jnp.int32), item_nz)
    return dest.reshape(t, TOP_K), src_tok, items


def kernel(x_prompt, x_sample, state_gla, state_conv, c_prompt, c_sample, w_ada, b_ada, norm1_w,
           w_in, w_gk_up, b_gk, gla_norm_w, w_conv, w_out, norm2_w, w_router, b_router, w_up, b_up,
           w_down, b_down, final_norm_w):
    n_p, seq_p, _ = x_prompt.shape
    n_s, seq_s, _ = x_sample.shape
    t_p, t_s = n_p * seq_p, n_s * seq_s
    assert seq_s == GS_SEQ and w_ada.shape[0] == 1

    c_all = jnp.concatenate([jnp.repeat(c_sample, seq_s, axis=0), c_prompt], axis=0)
    pad = (-c_all.shape[0]) % BF16_ROWS
    c_all = jnp.pad(c_all, ((0, pad), (0, 0)))
    mod = _ada(c_all, w_ada[0], b_ada[0])
    mod_s = mod
    mod_p = mod[t_s:t_s + n_p].reshape(n_p, 1, N_MOD * D_MODEL)

    w_main, w_alow = _prep_w_in(w_in[0].T)
    w_gk = jnp.pad(w_gk_up[0], ((0, LANES - GLA_LOW_RANK), (0, 0))).astype(bf16)
    bgk = b_gk[0].reshape(1, GLA_DK)
    wo = w_out[0].astype(bf16)
    wr = jnp.pad(w_router[0], ((0, 0), (0, LANES - N_EXPERTS)))
    wr_hi = wr.astype(bf16)
    wr_lo = (wr - wr_hi.astype(f32)).astype(bf16)
    br = jnp.pad(b_router[0], (0, LANES - N_EXPERTS), constant_values=ROUTER_PAD).reshape(1, LANES)
    n1w = norm1_w[0].reshape(1, D_MODEL)
    n2w = norm2_w[0].reshape(1, D_MODEL)
    gnw = gla_norm_w[0].reshape(1, GLA_HV)
    fw = final_norm_w.reshape(1, D_MODEL)
    wc = w_conv[0]

    xp = x_prompt.reshape(t_p, D_MODEL)
    xs_ = x_sample.reshape(t_s, D_MODEL)

    proj_p, la_p = _in_proj(xp, mod_p, n1w, w_main, w_alow, w_gk, bgk, 1024, seq_p)
    proj_s, la_s = _in_proj(xs_, mod_s, n1w, w_main, w_alow, w_gk, bgk, t_s, seq_s)

    gla0 = jnp.zeros((n_p, GLA_HEADS, GLA_HK, GLA_HV), f32)
    conv0 = jnp.zeros((n_p, CONV_K - 1, CONV_CH), f32)
    oa_p, gla_p = _gla_scan(proj_p, la_p, gnw, gla0, n_p, seq_p)
    oa_s, gla_s = _gla_step(proj_s, la_s, gnw, state_gla[0])

    hn = jnp.zeros((HN_BUF_FACTOR * (t_p + t_s), D_MODEL), bf16)
    h_p, hn, gate_p, eid_p, ut_p = _post(xp, oa_p, proj_p, conv0, wc, mod_p, n2w, wo,
                                         wr_hi, wr_lo, br, hn, 0, True, seq_p)
    prev_s = (jnp.repeat(state_conv[0][:, 0], seq_s, axis=0), jnp.repeat(state_conv[0][:, 1], seq_s, axis=0))
    h_s, hn, gate_s, eid_s, u_s = _post(xs_, oa_s, proj_s, prev_s, wc, mod_s, n2w, wo,
                                        wr_hi, wr_lo, br, hn, t_p, False, seq_s)
    conv_p = ut_p.reshape(n_p, seq_p // POST_TM, U_TAIL, CONV_CH)[:, -1, U_TAIL - (CONV_K - 1):]
    conv_s = u_s.reshape(n_s, seq_s, CONV_CH)[:, seq_s - (CONV_K - 1):]

    gate = jnp.concatenate([gate_p, gate_s], axis=0)
    top_e = jnp.concatenate([eid_p, eid_s], axis=0)[:, :TOP_K]
    dest, src_tok, items = _routing(top_e)
    x_rows = hn[src_tok]
    out_rows = _moe(x_rows, items, w_up[0], b_up[0], w_down[0], b_down[0])
    og = out_rows[dest.T.reshape(-1)].reshape(TOP_K, t_p + t_s, D_MODEL)

    y_p = _final(h_p, og, gate, mod_p, fw, 0, seq_p)
    y_s = _final(h_s, og, gate, mod_s, fw, t_p, seq_s)

    return (y_p.reshape(n_p, seq_p, D_MODEL), y_s.reshape(n_s, seq_s, D_MODEL),
            gla_p[None], conv_p[None], gla_s[None], conv_s[None])
```

```python
import functools

import jax
import jax.numpy as jnp
from jax import lax
from jax.experimental import pallas as pl
from jax.experimental.pallas import tpu as pltpu

f32 = jnp.float32
bf16 = jnp.bfloat16

D_MODEL = 2048
N_MOD = 6
GLA_HEADS = 4
GLA_DK = 512
GLA_DV = 1024
GLA_HK = 128
GLA_HV = 256
GLA_LOW_RANK = 16
GLA_TAU = 16.0
GLA_CHUNK = 64
CONV_CH = 1024
CONV_K = 3
N_EXPERTS = 32
TOP_K = 4
D_FF = 2048
SWIGLU_LIMIT = 7.0
SWIGLU_ALPHA = 1.702
EPS = 1e-6

LANES = 128
SUBLANES = 8
BF16_ROWS = 2 * SUBLANES
VMEM_LIMIT = 56 * 1024 * 1024

OFF_Q = 0
OFF_K = OFF_Q + GLA_DK
OFF_V = OFF_K + GLA_DK
OFF_G = OFF_V + GLA_DV
OFF_CB = OFF_G + GLA_DV
OFF_CC = OFF_CB + CONV_CH
OFF_CH = OFF_CC + CONV_CH
OFF_GA = OFF_CH + CONV_CH
OFF_GB = OFF_GA + D_MODEL
D_MAIN = OFF_GB + D_MODEL

ROUTER_PAD = -1e30
HN_BUF_FACTOR = 2

HIGHEST = lax.Precision.HIGHEST


def _cparams(sem):
    return pltpu.CompilerParams(dimension_semantics=sem, vmem_limit_bytes=VMEM_LIMIT)


def _dot(a, b):
    return jnp.dot(a, b, preferred_element_type=f32)


def _dot_nt(a, b):
    return lax.dot_general(a, b, (((1,), (1,)), ((), ())), preferred_element_type=f32)


def _dot_exact(a, b):
    return jnp.dot(a, b, precision=HIGHEST, preferred_element_type=f32)


def _sigmoid_tanh(x):
    return 0.5 * jnp.tanh(0.5 * x) + 0.5


def _rms(x):
    return x * lax.rsqrt(jnp.mean(x * x, axis=-1, keepdims=True) + EPS)


ADA_TN = 2048


def _ada_kernel(c_ref, w_ref, b_ref, o_ref):
    c = c_ref[...]
    s = (c * jax.nn.sigmoid(c)).astype(bf16)
    o_ref[...] = _dot(s, w_ref[...].astype(bf16)) + b_ref[...]


def _ada(c, w, b):
    rows = c.shape[0]
    n = w.shape[1]
    return pl.pallas_call(
        _ada_kernel,
        grid=(n // ADA_TN,),
        in_specs=[pl.BlockSpec((rows, D_MODEL), lambda j: (0, 0)),
                  pl.BlockSpec((D_MODEL, ADA_TN), lambda j: (0, j)),
                  pl.BlockSpec((1, ADA_TN), lambda j: (0, j))],
        out_specs=pl.BlockSpec((rows, ADA_TN), lambda j: (0, j)),
        out_shape=jax.ShapeDtypeStruct((rows, n), f32),
        compiler_params=_cparams(("arbitrary",)),
        name="adaln",
    )(c, w, b.reshape(1, n))


PREP_TN = 1024


def _prep_kernel(a_ref, b_ref, low_ref, main_ref, alow_ref):
    j = pl.program_id(0)
    first_shifted = OFF_G // PREP_TN

    @pl.when(j < first_shifted)
    def _():
        main_ref[...] = a_ref[...].T.astype(bf16)

    @pl.when(j >= first_shifted)
    def _():
        wide = jnp.concatenate([a_ref[...], b_ref[...]], axis=0)
        main_ref[...] = wide[GLA_LOW_RANK:GLA_LOW_RANK + PREP_TN].T.astype(bf16)

    @pl.when(j == 0)
    def _():
        low = low_ref[...].T
        lane = lax.broadcasted_iota(jnp.int32, low.shape, 1)
        alow_ref[...] = jnp.where(lane < GLA_LOW_RANK, low, 0.0).astype(bf16)


def _prep_w_in(wt):
    d = wt.shape[1]
    return pl.pallas_call(
        _prep_kernel,
        grid=(D_MAIN // PREP_TN,),
        in_specs=[pl.BlockSpec((PREP_TN, d), lambda j: (j, 0)),
                  pl.BlockSpec((GLA_LOW_RANK, d), lambda j: ((j + 1) * (PREP_TN // GLA_LOW_RANK), 0)),
                  pl.BlockSpec((LANES, d), lambda j: (OFF_G // LANES, 0))],
        out_specs=[pl.BlockSpec((d, PREP_TN), lambda j: (0, j)),
                   pl.BlockSpec((d, LANES), lambda j: (0, 0))],
        out_shape=[jax.ShapeDtypeStruct((d, D_MAIN), bf16),
                   jax.ShapeDtypeStruct((d, LANES), bf16)],
        compiler_params=_cparams(("arbitrary",)),
        name="prep_w_in",
    )(wt, wt, wt)


IN_TN = 1024


def _in_kernel(x_ref, sc_ref, sh_ref, nw_ref, w_ref, wa_ref, wgk_ref, bgk_ref,
               proj_ref, la_ref, xn_ref):
    @pl.when(pl.program_id(1) == 0)
    def _():
        xn = _rms(x_ref[...]) * nw_ref[...] * (1.0 + sc_ref[...]) + sh_ref[...]
        xnb = xn.astype(bf16)
        xn_ref[...] = xnb
        a_low = _dot(xnb, wa_ref[...])
        z = _dot(a_low.astype(bf16), wgk_ref[...]) + bgk_ref[...]
        la_ref[...] = (jnp.minimum(z, 0.0) - jnp.log(1.0 + jnp.exp(-jnp.abs(z)))) * (1.0 / GLA_TAU)

    proj_ref[...] = _dot(xn_ref[...], w_ref[...]).astype(bf16)


MOD_SH1, MOD_SC1, MOD_G1, MOD_SH2, MOD_SC2, MOD_G2 = range(N_MOD)


def _mod_spec(mod, which, tm, seq_len):
    if mod.ndim == 3:
        tiles_per_seq = seq_len // tm
        return pl.BlockSpec((None, 1, D_MODEL), lambda *g: (g[0] // tiles_per_seq, 0, which))
    return pl.BlockSpec((tm, D_MODEL), lambda *g: (g[0], which))


def _in_proj(x2, mod, nw, w_main, w_alow, w_gk, b_gk, tm, seq_len):
    t = x2.shape[0]
    return pl.pallas_call(
        _in_kernel,
        grid=(t // tm, D_MAIN // IN_TN),
        in_specs=[pl.BlockSpec((tm, D_MODEL), lambda m, n: (m, 0)),
                  _mod_spec(mod, MOD_SC1, tm, seq_len), _mod_spec(mod, MOD_SH1, tm, seq_len),
                  pl.BlockSpec((1, D_MODEL), lambda m, n: (0, 0)),
                  pl.BlockSpec((D_MODEL, IN_TN), lambda m, n: (0, n)),
                  pl.BlockSpec((D_MODEL, LANES), lambda m, n: (0, 0)),
                  pl.BlockSpec((LANES, GLA_DK), lambda m, n: (0, 0)),
                  pl.BlockSpec((1, GLA_DK), lambda m, n: (0, 0))],
        out_specs=[pl.BlockSpec((tm, IN_TN), lambda m, n: (m, n)),
                   pl.BlockSpec((tm, GLA_DK), lambda m, n: (m, 0))],
        out_shape=[jax.ShapeDtypeStruct((t, D_MAIN), bf16),
                   jax.ShapeDtypeStruct((t, GLA_DK), f32)],
        scratch_shapes=[pltpu.VMEM((tm, D_MODEL), bf16)],
        compiler_params=_cparams(("arbitrary", "arbitrary")),
        name="in_proj",
    )(x2, mod, mod, nw, w_main, w_alow, w_gk, b_gk)


GLA_R = 256


def _gla_norm_gate(o, nw, g):
    gf = g.astype(f32)
    return _rms(o) * nw * (gf * jax.nn.sigmoid(gf))


def _gla_scan_kernel(q_ref, k_ref, v_ref, g_ref, la_ref, nw_ref, s0_ref, o_ref, sout_ref, s_ref):
    step = pl.program_id(1)

    @pl.when(step == 0)
    def _():
        s_ref[...] = s0_ref[...]

    r, c = GLA_R, GLA_CHUNK
    n_chunks = r // c
    row = lax.broadcasted_iota(jnp.int32, (r, r), 0)
    col = lax.broadcasted_iota(jnp.int32, (r, r), 1)
    same_chunk = (row // c) == (col // c)
    causal = same_chunk & (col <= row)

    la = la_ref[...]
    b = _dot_exact(causal.astype(f32), la)
    b_mid = jnp.concatenate(
        [jnp.broadcast_to(b[i * c + c // 2:i * c + c // 2 + 1], (c, GLA_DK)) for i in range(n_chunks)], axis=0)
    b_last = jnp.concatenate(
        [jnp.broadcast_to(b[i * c + c - 1:i * c + c], (c, GLA_DK)) for i in range(n_chunks)], axis=0)
    q = q_ref[...].astype(f32) * (GLA_HK ** -0.5)
    k = k_ref[...].astype(f32)
    qs = (q * jnp.exp(b - b_mid)).astype(bf16)
    ks = (k * jnp.exp(b_mid - b)).astype(bf16)
    qd = (q * jnp.exp(b)).astype(bf16)
    kd_t = (k * jnp.exp(b_last - b)).T.astype(bf16)
    la_t = la.T
    lane = lax.broadcasted_iota(jnp.int32, (GLA_HK, r), 1)
    nw = nw_ref[...]

    for h in range(GLA_HEADS):
        hk = slice(h * GLA_HK, (h + 1) * GLA_HK)
        hv = slice(h * GLA_HV, (h + 1) * GLA_HV)
        v_h = v_ref[:, hv]
        scores = _dot_nt(qs[:, hk], ks[:, hk])
        o_intra = _dot(jnp.where(causal, scores, 0.0).astype(bf16), v_h)
        s = s_ref[h]
        for i in range(n_chunks):
            rows = slice(i * c, (i + 1) * c)
            in_chunk = (lane // c) == i
            o = o_intra[rows] + _dot(qd[rows, hk], s.astype(bf16))
            o_ref[rows, hv] = _gla_norm_gate(o, nw, g_ref[rows, hv]).astype(bf16)
            decay = jnp.exp(jnp.sum(jnp.where(in_chunk, la_t[hk], 0.0), axis=1, keepdims=True))
            s = decay * s + _dot(jnp.where(in_chunk, kd_t[hk], jnp.zeros_like(kd_t[hk])), v_h)
        s_ref[h] = s

    @pl.when(step == pl.num_programs(1) - 1)
    def _():
        sout_ref[...] = s_ref[...]


def _gla_scan(proj, la, nw, s0, n_seq, seq_len):
    steps = seq_len // GLA_R
    t = n_seq * seq_len

    def rows(b, s):
        return b * steps + s

    return pl.pallas_call(
        _gla_scan_kernel,
        grid=(n_seq, steps),
        in_specs=[pl.BlockSpec((GLA_R, GLA_DK), lambda b, s: (rows(b, s), OFF_Q // GLA_DK)),
                  pl.BlockSpec((GLA_R, GLA_DK), lambda b, s: (rows(b, s), OFF_K // GLA_DK)),
                  pl.BlockSpec((GLA_R, GLA_DV), lambda b, s: (rows(b, s), OFF_V // GLA_DV)),
                  pl.BlockSpec((GLA_R, GLA_DV), lambda b, s: (rows(b, s), OFF_G // GLA_DV)),
                  pl.BlockSpec((GLA_R, GLA_DK), lambda b, s: (rows(b, s), 0)),
                  pl.BlockSpec((1, GLA_HV), lambda b, s: (0, 0)),
                  pl.BlockSpec((None, GLA_HEADS, GLA_HK, GLA_HV), lambda b, s: (b, 0, 0, 0))],
        out_specs=[pl.BlockSpec((GLA_R, GLA_DV), lambda b, s: (rows(b, s), 0)),
                   pl.BlockSpec((None, GLA_HEADS, GLA_HK, GLA_HV), lambda b, s: (b, 0, 0, 0))],
        out_shape=[jax.ShapeDtypeStruct((t, GLA_DV), bf16),
                   jax.ShapeDtypeStruct((n_seq, GLA_HEADS, GLA_HK, GLA_HV), f32)],
        scratch_shapes=[pltpu.VMEM((GLA_HEADS, GLA_HK, GLA_HV), f32)],
        compiler_params=_cparams(("arbitrary", "arbitrary")),
        name="gla_scan",
    )(proj, proj, proj, proj, la, nw, s0)


GS_SEQ = 4
GS_ROWS = 128
GS_B = GS_ROWS // GS_SEQ
GS_STEP_B = 8


def _gla_step_kernel(q_ref, k_ref, v_ref, g_ref, la_ref, nw_ref, s0_ref, o_ref, sout_ref,
                     oacc_ref, qd_ref, kdt_ref, lat_ref):
    j = pl.program_id(1)
    r = GS_ROWS
    row = lax.broadcasted_iota(jnp.int32, (r, r), 0)
    col = lax.broadcasted_iota(jnp.int32, (r, r), 1)

    @pl.when(j == 0)
    def _():
        same_seq = (row // GS_SEQ) == (col // GS_SEQ)
        causal = same_seq & (col <= row)
        la = la_ref[...]
        b = _dot_exact(causal.astype(f32), la)
        b_mid = _dot_exact((same_seq & (col % GS_SEQ <= GS_SEQ // 2)).astype(f32), la)
        b_last = _dot_exact(same_seq.astype(f32), la)
        q = q_ref[...].astype(f32) * (GLA_HK ** -0.5)
        k = k_ref[...].astype(f32)
        qs = (q * jnp.exp(b - b_mid)).astype(bf16)
        ks = (k * jnp.exp(b_mid - b)).astype(bf16)
        qd_ref[...] = (q * jnp.exp(b)).astype(bf16)
        kdt_ref[...] = (k * jnp.exp(b_last - b)).T.astype(bf16)
        lat_ref[...] = la.T
        for h in range(GLA_HEADS):
            hk = slice(h * GLA_HK, (h + 1) * GLA_HK)
            hv = slice(h * GLA_HV, (h + 1) * GLA_HV)
            scores = _dot_nt(qs[:, hk], ks[:, hk])
            oacc_ref[:, hv] = _dot(jnp.where(causal, scores, 0.0).astype(bf16), v_ref[:, hv])

    for bb in range(GS_STEP_B):
        seq = j * GS_STEP_B + bb
        in_seq_lane = (col // GS_SEQ) == seq
        in_seq_row = (row[:, :GLA_HK] // GS_SEQ) == seq
        for h in range(GLA_HEADS):
            hk = slice(h * GLA_HK, (h + 1) * GLA_HK)
            hv = slice(h * GLA_HV, (h + 1) * GLA_HV)
            s0 = s0_ref[bb, h]
            kd_t = kdt_ref[hk, :]
            ds = _dot(jnp.where(in_seq_lane, kd_t, jnp.zeros_like(kd_t)), v_ref[:, hv])
            decay = jnp.exp(jnp.sum(jnp.where(in_seq_lane, lat_ref[hk, :], 0.0), axis=1, keepdims=True))
            sout_ref[bb, h] = decay * s0 + ds
            qd = qd_ref[:, hk]
            oacc_ref[:, hv] += _dot(jnp.where(in_seq_row, qd, jnp.zeros_like(qd)), s0.astype(bf16))

    @pl.when(j == pl.num_programs(1) - 1)
    def _():
        nw = nw_ref[...]
        for h in range(GLA_HEADS):
            hv = slice(h * GLA_HV, (h + 1) * GLA_HV)
            o_ref[:, hv] = _gla_norm_gate(oacc_ref[:, hv], nw, g_ref[:, hv]).astype(bf16)


def _gla_step(proj, la, nw, s0):
    n_seq = s0.shape[0]
    t = n_seq * GS_SEQ
    groups = t // GS_ROWS
    steps = GS_B // GS_STEP_B
    return pl.pallas_call(
        _gla_step_kernel,
        grid=(groups, steps),
        in_specs=[pl.BlockSpec((GS_ROWS, GLA_DK), lambda g, j: (g, OFF_Q // GLA_DK)),
                  pl.BlockSpec((GS_ROWS, GLA_DK), lambda g, j: (g, OFF_K // GLA_DK)),
                  pl.BlockSpec((GS_ROWS, GLA_DV), lambda g, j: (g, OFF_V // GLA_DV)),
                  pl.BlockSpec((GS_ROWS, GLA_DV), lambda g, j: (g, OFF_G // GLA_DV)),
                  pl.BlockSpec((GS_ROWS, GLA_DK), lambda g, j: (g, 0)),
                  pl.BlockSpec((1, GLA_HV), lambda g, j: (0, 0)),
                  pl.BlockSpec((GS_STEP_B, GLA_HEADS, GLA_HK, GLA_HV), lambda g, j: (g * steps + j, 0, 0, 0))],
        out_specs=[pl.BlockSpec((GS_ROWS, GLA_DV), lambda g, j: (g, 0)),
                   pl.BlockSpec((GS_STEP_B, GLA_HEADS, GLA_HK, GLA_HV), lambda g, j: (g * steps + j, 0, 0, 0))],
        out_shape=[jax.ShapeDtypeStruct((t, GLA_DV), bf16),
                   jax.ShapeDtypeStruct((n_seq, GLA_HEADS, GLA_HK, GLA_HV), f32)],
        scratch_shapes=[pltpu.VMEM((GS_ROWS, GLA_DV), f32),
                        pltpu.VMEM((GS_ROWS, GLA_DK), bf16),
                        pltpu.VMEM((GLA_DK, GS_ROWS), bf16),
                        pltpu.VMEM((GLA_DK, GS_ROWS), f32)],
        compiler_params=_cparams(("arbitrary", "arbitrary")),
        name="gla_step",
    )(proj, proj, proj, proj, la, nw, s0)


POST_TM = 512
HALO = BF16_ROWS
U_TAIL = SUBLANES


def _post_kernel(long_seq, tiles_per_seq, *refs):
    if long_seq:
        (x_ref, oa_ref, cb_ref, cc_ref, ch_ref, ga_ref, gb_ref, hcc_ref, hch_ref, cbuf_ref,
         wc_ref, g1_ref, sc_ref, sh_ref, nw_ref, wo_ref, wrh_ref, wrl_ref, br_ref, _hn_alias,
         h_ref, hn_ref, gate_ref, eid_ref, u_ref) = refs
    else:
        (x_ref, oa_ref, cb_ref, cc_ref, ch_ref, ga_ref, gb_ref, p0_ref, p1_ref,
         wc_ref, g1_ref, sc_ref, sh_ref, nw_ref, wo_ref, wrh_ref, wrl_ref, br_ref, _hn_alias,
         h_ref, hn_ref, gate_ref, eid_ref, u_ref) = refs
    tm = x_ref.shape[0]
    u = cc_ref[...].astype(f32) * ch_ref[...].astype(f32)
    row = lax.broadcasted_iota(jnp.int32, (tm, CONV_CH), 0)
    if long_seq:
        pos = row
        first = (pl.program_id(0) % tiles_per_seq) == 0
        halo = hcc_ref[HALO - 2:HALO, :].astype(f32) * hch_ref[HALO - 2:HALO, :].astype(f32)
        cbuf = cbuf_ref[...]
        p0 = jnp.where(first, cbuf[0:1], halo[0:1])
        p1 = jnp.where(first, cbuf[1:2], halo[1:2])
        u_ref[...] = u[tm - U_TAIL:tm]
    else:
        pos = row % GS_SEQ
        p0 = p0_ref[...]
        p1 = p1_ref[...]
        u_ref[...] = u
    u1 = jnp.where(pos == 0, p1, pltpu.roll(u, 1, 0))
    u2 = jnp.where(pos == 0, p0, jnp.where(pos == 1, p1, pltpu.roll(u, 2, 0)))
    wc = wc_ref[...]
    conv = wc[0:1] * u2 + wc[1:2] * u1 + wc[2:3] * u
    ob = (cb_ref[...].astype(f32) * conv).astype(bf16)

    ya = _dot(oa_ref[...], wo_ref[0:GLA_DV, :])
    yb = _dot(ob, wo_ref[GLA_DV:GLA_DV + CONV_CH, :])
    y = _sigmoid_tanh(ga_ref[...].astype(f32)) * ya + _sigmoid_tanh(gb_ref[...].astype(f32)) * yb
    h = x_ref[...] + g1_ref[...] * y
    h_ref[...] = h
    hn = _rms(h) * nw_ref[...] * (1.0 + sc_ref[...]) + sh_ref[...]
    hn_hi = hn.astype(bf16)
    hn_ref[...] = hn_hi
    hn_lo = (hn - hn_hi.astype(f32)).astype(bf16)
    lg = (_dot(hn_hi, wrh_ref[...]) + _dot(hn_hi, wrl_ref[...]) + _dot(hn_lo, wrh_ref[...])
          + br_ref[...])

    lane = lax.broadcasted_iota(jnp.int32, (tm, LANES), 1)
    lane_f = lane.astype(f32)
    vals, idxs = [], []
    for _ in range(TOP_K):
        m = jnp.max(lg, axis=-1, keepdims=True)
        idx = jnp.min(jnp.where(lg == m, lane_f, float(LANES)), axis=-1, keepdims=True)
        vals.append(m)
        idxs.append(idx)
        lg = jnp.where(lane_f == idx, -jnp.inf, lg)
    exps = [jnp.exp(v - vals[0]) for v in vals]
    den = exps[0] + exps[1] + exps[2] + exps[3]
    gate = jnp.zeros((tm, LANES), f32)
    eid = jnp.zeros((tm, LANES), f32)
    for j in range(TOP_K):
        gate = jnp.where(lane == j, exps[j] / den, gate)
        eid = jnp.where(lane == j, idxs[j], eid)
    gate_ref[...] = gate
    eid_ref[...] = eid.astype(jnp.int32)


def _post(x2, oa, proj, prev, wc, mod, nw, wo, wrh, wrl, br, hn_buf, row0, long_seq, seq_len):
    t = x2.shape[0]
    tm = POST_TM
    n_tiles = t // tm
    blk0 = row0 // tm
    tiles_per_seq = seq_len // tm if long_seq else 1

    def colblk(off, width):
        return pl.BlockSpec((tm, width), lambda m: (m, off // width))

    const = lambda shape: pl.BlockSpec(shape, lambda m: tuple(0 for _ in shape))
    if long_seq:
        halo_rows = lambda m: jnp.maximum(m * (tm // HALO) - 1, 0)
        prev_specs = [pl.BlockSpec((HALO, CONV_CH), lambda m: (halo_rows(m), OFF_CC // CONV_CH)),
                      pl.BlockSpec((HALO, CONV_CH), lambda m: (halo_rows(m), OFF_CH // CONV_CH)),
                      pl.BlockSpec((None, CONV_K - 1, CONV_CH), lambda m: (m // tiles_per_seq, 0, 0))]
        prev_args = [proj, proj, prev]
        u_spec = pl.BlockSpec((None, U_TAIL, CONV_CH), lambda m: (m, 0, 0))
        u_shape = jax.ShapeDtypeStruct((n_tiles, U_TAIL, CONV_CH), f32)
    else:
        prev_specs = [pl.BlockSpec((tm, CONV_CH), lambda m: (m, 0))] * 2
        prev_args = list(prev)
        u_spec = pl.BlockSpec((tm, CONV_CH), lambda m: (m, 0))
        u_shape = jax.ShapeDtypeStruct((t, CONV_CH), f32)
    n_in = 7 + len(prev_specs) + 10
    return pl.pallas_call(
        functools.partial(_post_kernel, long_seq, tiles_per_seq),
        grid=(n_tiles,),
        in_specs=[pl.BlockSpec((tm, D_MODEL), lambda m: (m, 0)),
                  pl.BlockSpec((tm, GLA_DV), lambda m: (m, 0)),
                  colblk(OFF_CB, CONV_CH), colblk(OFF_CC, CONV_CH), colblk(OFF_CH, CONV_CH),
                  colblk(OFF_GA, D_MODEL), colblk(OFF_GB, D_MODEL)]
                 + prev_specs
                 + [const((CONV_K, CONV_CH)), _mod_spec(mod, MOD_G1, tm, seq_len),
                    _mod_spec(mod, MOD_SC2, tm, seq_len), _mod_spec(mod, MOD_SH2, tm, seq_len), const((1, D_MODEL)),
                    const((GLA_DV + CONV_CH, D_MODEL)), const((D_MODEL, LANES)), const((D_MODEL, LANES)),
                    const((1, LANES)), pl.BlockSpec(memory_space=pl.ANY)],
        out_specs=[pl.BlockSpec((tm, D_MODEL), lambda m: (m, 0)),
                   pl.BlockSpec((tm, D_MODEL), lambda m: (blk0 + m, 0)),
                   pl.BlockSpec((tm, LANES), lambda m: (m, 0)),
                   pl.BlockSpec((tm, LANES), lambda m: (m, 0)),
                   u_spec],
        out_shape=[jax.ShapeDtypeStruct((t, D_MODEL), f32),
                   jax.ShapeDtypeStruct(hn_buf.shape, bf16),
                   jax.ShapeDtypeStruct((t, LANES), f32),
                   jax.ShapeDtypeStruct((t, LANES), jnp.int32),
                   u_shape],
        input_output_aliases={n_in - 1: 1},
        compiler_params=_cparams(("arbitrary",)),
        name="post_mix",
    )(x2, oa, proj, proj, proj, proj, proj, *prev_args, wc, mod, mod, mod, nw, wo, wrh, wrl, br, hn_buf)


MOE_RB = 128
MOE_TM = 1536
MOE_TN = 512
MOE_KB = MOE_TM // MOE_RB
MOE_NT = D_FF // MOE_TN
MOE_BIG = 8


def _moe_kernel(ie_ref, ib_ref, in_ref, iv_ref, iz_ref,
                x_hbm, wg_ref, wl_ref, wd_ref, bg_ref, bl_ref, bd_ref, out_hbm,
                xbuf, hbuf, obuf, wcast, sem_in, sem_out):
    i = pl.program_id(0)
    s = pl.program_id(1)
    n_items = pl.num_programs(0)
    nb = in_ref[i]

    def x_copy(item, jb, k):
        return pltpu.make_async_copy(
            x_hbm.at[pl.ds((ib_ref[item] + jb) * MOE_RB, MOE_RB), pl.ds(k * MOE_TN, MOE_TN)],
            xbuf.at[k, pl.ds(jb * MOE_RB, MOE_RB)], sem_in)

    def out_copy(item, jb, k):
        return pltpu.make_async_copy(
            obuf.at[k, pl.ds(jb * MOE_RB, MOE_RB)],
            out_hbm.at[pl.ds((ib_ref[item] + jb) * MOE_RB, MOE_RB), pl.ds(k * MOE_TN, MOE_TN)], sem_out)

    def for_blocks(n, fn):
        def body(jb, carry):
            for k in range(MOE_NT):
                fn(jb, k)
            return carry

        lax.fori_loop(0, n, body, 0)

    def for_rows(fn):
        def body(j, carry):
            fn(pl.multiple_of(j * (MOE_BIG * MOE_RB), MOE_BIG * MOE_RB), MOE_BIG * MOE_RB)
            return carry

        lax.fori_loop(0, nb // MOE_BIG, body, 0)
        done = (nb // MOE_BIG) * MOE_BIG
        size = MOE_BIG // 2
        while size >= 1:
            @pl.when((nb & size) != 0)
            def _(done=done, size=size):
                fn(pl.multiple_of(done * MOE_RB, MOE_RB), size * MOE_RB)

            done = done + (nb & size)
            size //= 2

    @pl.when(s == 0)
    def _():
        @pl.when(i == 0)
        def _():
            for_blocks(nb, lambda jb, k: x_copy(i, jb, k).start())

        for_blocks(nb, lambda jb, k: x_copy(i, jb, k).wait())

    @pl.when(s < MOE_NT)
    def _():
        wcast[:, 0:MOE_TN] = wg_ref[...].astype(bf16)
        wcast[:, MOE_TN:2 * MOE_TN] = wl_ref[...].astype(bf16)
        bg = bg_ref[...]
        bl = bl_ref[...]

        def up(r0, size):
            rows = pl.ds(r0, size)
            xb = jnp.concatenate([xbuf[k, rows, :] for k in range(MOE_NT)], axis=1)
            h = _dot(xb, wcast[...])
            hg = jnp.minimum(h[:, 0:MOE_TN] + bg, SWIGLU_LIMIT)
            hl = jnp.clip(h[:, MOE_TN:2 * MOE_TN] + bl, -SWIGLU_LIMIT, SWIGLU_LIMIT)
            hbuf[s, rows, :] = ((hl + 1.0) * hg * jax.nn.sigmoid(SWIGLU_ALPHA * hg)).astype(bf16)

        for_rows(up)

    @pl.when(s == MOE_NT)
    def _():
        @pl.when(i > 0)
        def _():
            prev = jnp.maximum(i - 1, 0)
            for_blocks(in_ref[prev], lambda jb, k: out_copy(prev, jb, k).wait())

        def fill(jb, k):
            obuf[k, pl.ds(pl.multiple_of(jb * MOE_RB, MOE_RB), MOE_RB), :] = jnp.zeros((MOE_RB, MOE_TN), bf16)
            out_copy(i, jb, k).start()

        nz = iz_ref[i]
        for_blocks(nz, fill)
        for_blocks(nz, lambda jb, k: out_copy(i, jb, k).wait())

        @pl.when(i + 1 < n_items)
        def _():
            nxt = jnp.minimum(i + 1, n_items - 1)
            for_blocks(in_ref[nxt], lambda jb, k: x_copy(nxt, jb, k).start())

    @pl.when(s >= MOE_NT)
    def _():
        wcast[:, 0:MOE_TN] = wd_ref[...].astype(bf16)
        bd = bd_ref[...]

        def down(r0, size):
            rows = pl.ds(r0, size)
            hb = jnp.concatenate([hbuf[k, rows, :] for k in range(MOE_NT)], axis=1)
            obuf[s - MOE_NT, rows, :] = (_dot(hb, wcast[:, 0:MOE_TN]) + bd).astype(bf16)

        for_rows(down)

    @pl.when(s == 2 * MOE_NT - 1)
    def _():
        for_blocks(nb, lambda jb, k: out_copy(i, jb, k).start())

        @pl.when(i == n_items - 1)
        def _():
            for_blocks(nb, lambda jb, k: out_copy(i, jb, k).wait())


def _moe(xs, items, w_up, b_up, w_down, b_down):
    rows = xs.shape[0]
    n_items = items[0].shape[0]

    last = MOE_NT - 1

    def up_tile(i, s, tabs):
        valid = tabs[3][i]
        return jnp.minimum(s, last) * valid + last * (1 - valid)

    def down_tile(i, s, tabs):
        valid = tabs[3][i]
        return jnp.maximum(s - MOE_NT, 0) * valid + last * (1 - valid)

    def expert(i, tabs):
        return tabs[0][i]

    grid_spec = pltpu.PrefetchScalarGridSpec(
        num_scalar_prefetch=len(items),
        grid=(n_items, 2 * MOE_NT),
        in_specs=[pl.BlockSpec(memory_space=pl.ANY),
                  pl.BlockSpec((None, D_MODEL, MOE_TN), lambda i, s, *t: (expert(i, t), 0, up_tile(i, s, t))),
                  pl.BlockSpec((None, D_MODEL, MOE_TN), lambda i, s, *t: (expert(i, t), 0, MOE_NT + up_tile(i, s, t))),
                  pl.BlockSpec((None, D_FF, MOE_TN), lambda i, s, *t: (expert(i, t), 0, down_tile(i, s, t))),
                  pl.BlockSpec((None, 1, MOE_TN), lambda i, s, *t: (expert(i, t), 0, up_tile(i, s, t))),
                  pl.BlockSpec((None, 1, MOE_TN), lambda i, s, *t: (expert(i, t), 0, MOE_NT + up_tile(i, s, t))),
                  pl.BlockSpec((None, 1, MOE_TN), lambda i, s, *t: (expert(i, t), 0, down_tile(i, s, t)))],
        out_specs=pl.BlockSpec(memory_space=pl.ANY),
        scratch_shapes=[pltpu.VMEM((MOE_NT, MOE_TM, MOE_TN), bf16),
                        pltpu.VMEM((MOE_NT, MOE_TM, MOE_TN), bf16),
                        pltpu.VMEM((MOE_NT, MOE_TM, MOE_TN), bf16),
                        pltpu.VMEM((D_MODEL, 2 * MOE_TN), bf16),
                        pltpu.SemaphoreType.DMA(()),
                        pltpu.SemaphoreType.DMA(())],
    )
    return pl.pallas_call(
        _moe_kernel,
        grid_spec=grid_spec,
        out_shape=jax.ShapeDtypeStruct((rows, D_MODEL), bf16),
        compiler_params=_cparams(("arbitrary", "arbitrary")),
        name="moe_experts",
    )(*items,
      xs, w_up, w_up, w_down,
      b_up.reshape(N_EXPERTS, 1, 2 * D_FF), b_up.reshape(N_EXPERTS, 1, 2 * D_FF),
      b_down.reshape(N_EXPERTS, 1, D_MODEL))


FIN_TM = 512


def _final_kernel(h_ref, og_ref, gate_ref, g2_ref, fw_ref, y_ref):
    gate = gate_ref[...]
    ff = og_ref[0].astype(f32) * gate[:, 0:1]
    for j in range(1, TOP_K):
        ff = ff + og_ref[j].astype(f32) * gate[:, j:j + 1]
    h = h_ref[...] + g2_ref[...] * ff
    y_ref[...] = _rms(h) * fw_ref[...]


def _final(h, og, gate, mod, fw, row0, seq_len):
    t = h.shape[0]
    tm = FIN_TM
    blk0 = row0 // tm
    return pl.pallas_call(
        _final_kernel,
        grid=(t // tm,),
        in_specs=[pl.BlockSpec((tm, D_MODEL), lambda m: (m, 0)),
                  pl.BlockSpec((TOP_K, tm, D_MODEL), lambda m: (0, blk0 + m, 0)),
                  pl.BlockSpec((tm, LANES), lambda m: (blk0 + m, 0)),
                  _mod_spec(mod, MOD_G2, tm, seq_len),
                  pl.BlockSpec((1, D_MODEL), lambda m: (0, 0))],
        out_specs=pl.BlockSpec((tm, D_MODEL), lambda m: (m, 0)),
        out_shape=jax.ShapeDtypeStruct((t, D_MODEL), f32),
        compiler_params=_cparams(("arbitrary",)),
        name="final_norm",
    )(h, og, gate, mod, fw)


def _count_le(bounds, x):
    return jnp.sum((bounds[None, :] <= x[:, None]).astype(jnp.int32), axis=1)


def _routing(top_e):
    t = top_e.shape[0]
    n_pairs = t * TOP_K
    e_flat = top_e.reshape(-1)
    onehot = (e_flat[:, None] == jnp.arange(N_EXPERTS, dtype=e_flat.dtype)[None, :]).astype(jnp.int32)
    csum = jnp.cumsum(onehot, axis=0)
    counts = csum[-1]
    nblk = (counts + MOE_RB - 1) // MOE_RB
    blk_end = jnp.cumsum(nblk)
    blk_start = blk_end - nblk
    row_start = blk_start * MOE_RB
    dest = jnp.sum(onehot * (csum + row_start[None, :]), axis=1) - 1
    n_rows = (n_pairs // MOE_RB + N_EXPERTS) * MOE_RB
    pair_bits = (n_pairs - 1).bit_length()
    pair_ids = jnp.arange(n_pairs, dtype=jnp.int32)
    order = jnp.sort(e_flat * (1 << pair_bits) + pair_ids) & ((1 << pair_bits) - 1)
    raw_start = jnp.cumsum(counts) - counts
    r = jnp.arange(n_rows, dtype=jnp.int32)
    begun = row_start[None, :] <= r[:, None]

    def at_row(table):
        steps = jnp.diff(table, prepend=0)
        return jnp.sum(jnp.where(begun, steps[None, :], 0), axis=1)

    src_pos = r - at_row(row_start - raw_start)
    is_real = r < at_row(row_start + counts)
    src_pair = order[jnp.clip(src_pos, 0, n_pairs - 1)]
    src_tok = jnp.where(is_real, src_pair // TOP_K, r % t).astype(jnp.int32)

    n_items = N_EXPERTS + n_rows // MOE_TM
    nit = (nblk + MOE_KB - 1) // MOE_KB
    it_end = jnp.cumsum(nit)
    it_start = it_end - nit
    idx = jnp.arange(n_items, dtype=jnp.int32)
    valid = idx < it_end[-1]
    last_valid = jnp.maximum(it_end[-1] - 1, 0)
    idx_c = jnp.minimum(idx, last_valid)
    e_of = jnp.minimum(_count_le(it_end, idx_c), N_EXPERTS - 1).astype(jnp.int32)
    k = idx_c - it_start[e_of]
    item_nb = jnp.where(valid, jnp.clip(nblk[e_of] - k * MOE_KB, 0, MOE_KB), 0).astype(jnp.int32)
    spare = idx - it_end[-1]
    tail_blk = blk_end[-1] + spare * MOE_KB
    item_nz = jnp.where(valid, 0, jnp.clip(n_rows // MOE_RB - tail_blk, 0, MOE_KB)).astype(jnp.int32)
    item_blk = jnp.where(valid, blk_start[e_of] + k * MOE_KB, tail_blk).astype(jnp.int32)
    items = (e_of, item_blk, item_nb, valid.astype(jnp.int32), item_nz)
    return dest.reshape(t, TOP_K), src_tok, items


def kernel(x_prompt, x_sample, state_gla, state_conv, c_prompt, c_sample, w_ada, b_ada, norm1_w,
           w_in, w_gk_up, b_gk, gla_norm_w, w_conv, w_out, norm2_w, w_router, b_router, w_up, b_up,
           w_down, b_down, final_norm_w):
    n_p, seq_p, _ = x_prompt.shape
    n_s, seq_s, _ = x_sample.shape
    t_p, t_s = n_p * seq_p, n_s * seq_s
    assert seq_s == GS_SEQ and w_ada.shape[0] == 1

    c_all = jnp.concatenate([jnp.repeat(c_sample, seq_s, axis=0), c_prompt], axis=0)
    pad = (-c_all.shape[0]) % BF16_ROWS
    c_all = jnp.pad(c_all, ((0, pad), (0, 0)))
    mod = _ada(c_all, w_ada[0], b_ada[0])
    mod_s = mod
    mod_p = mod[t_s:t_s + n_p].reshape(n_p, 1, N_MOD * D_MODEL)

    w_main, w_alow = _prep_w_in(w_in[0].T)
    w_gk = jnp.pad(w_gk_up[0], ((0, LANES - GLA_LOW_RANK), (0, 0))).astype(bf16)
    bgk = b_gk[0].reshape(1, GLA_DK)
    wo = w_out[0].astype(bf16)
    wr = jnp.pad(w_router[0], ((0, 0), (0, LANES - N_EXPERTS)))
    wr_hi = wr.astype(bf16)
    wr_lo = (wr - wr_hi.astype(f32)).astype(bf16)
    br = jnp.pad(b_router[0], (0, LANES - N_EXPERTS), constant_values=ROUTER_PAD).reshape(1, LANES)
    n1w = norm1_w[0].reshape(1, D_MODEL)
    n2w = norm2_w[0].reshape(1, D_MODEL)
    gnw = gla_norm_w[0].reshape(1, GLA_HV)
    fw = final_norm_w.reshape(1, D_MODEL)
    wc = w_conv[0]

    xp = x_prompt.reshape(t_p, D_MODEL)
    xs_ = x_sample.reshape(t_s, D_MODEL)

    proj_p, la_p = _in_proj(xp, mod_p, n1w, w_main, w_alow, w_gk, bgk, 1024, seq_p)
    proj_s, la_s = _in_proj(xs_, mod_s, n1w, w_main, w_alow, w_gk, bgk, t_s, seq_s)

    gla0 = jnp.zeros((n_p, GLA_HEADS, GLA_HK, GLA_HV), f32)
    conv0 = jnp.zeros((n_p, CONV_K - 1, CONV_CH), f32)
    oa_p, gla_p = _gla_scan(proj_p, la_p, gnw, gla0, n_p, seq_p)
    oa_s, gla_s = _gla_step(proj_s, la_s, gnw, state_gla[0])

    hn = jnp.zeros((HN_BUF_FACTOR * (t_p + t_s), D_MODEL), bf16)
    h_p, hn, gate_p, eid_p, ut_p = _post(xp, oa_p, proj_p, conv0, wc, mod_p, n2w, wo,
                                         wr_hi, wr_lo, br, hn, 0, True, seq_p)
    prev_s = (jnp.repeat(state_conv[0][:, 0], seq_s, axis=0), jnp.repeat(state_conv[0][:, 1], seq_s, axis=0))
    h_s, hn, gate_s, eid_s, u_s = _post(xs_, oa_s, proj_s, prev_s, wc, mod_s, n2w, wo,
                                        wr_hi, wr_lo, br, hn, t_p, False, seq_s)
    conv_p = ut_p.reshape(n_p, seq_p // POST_TM, U_TAIL, CONV_CH)[:, -1, U_TAIL - (CONV_K - 1):]
    conv_s = u_s.reshape(n_s, seq_s, CONV_CH)[:, seq_s - (CONV_K - 1):]

    gate = jnp.concatenate([gate_p, gate_s], axis=0)
    top_e = jnp.concatenate([eid_p, eid_s], axis=0)[:, :TOP_K]
    dest, src_tok, items = _routing(top_e)
    x_rows = hn[src_tok]
    out_rows = _moe(x_rows, items, w_up[0], b_up[0], w_down[0], b_down[0])
    og = out_rows[dest.T.reshape(-1)].reshape(TOP_K, t_p + t_s, D_MODEL)

    y_p = _final(h_p, og, gate, mod_p, fw, 0, seq_p)
    y_s = _final(h_s, og, gate, mod_s, fw, t_p, seq_s)

    return (y_p.reshape(n_p, seq_p, D_MODEL), y_s.reshape(n_s, seq_s, D_MODEL),
            gla_p[None], conv_p[None], gla_s[None], conv_s[None])
```
